```python
import jax, jax.numpy as jnp
from jax import lax
import numpy as np

D_MODEL = 1024
BATCH = 2
SEQ = 8192
DEPTH = 2

ML_HEADS = 4
ML_WIDTH = D_MODEL
ML_HEAD_DIM = ML_WIDTH // ML_HEADS
ML_CHUNK = 64
CONV_WIDTH = 4
FOX_HEADS = 8
FOX_WIDTH = D_MODEL
FOX_HEAD_DIM = FOX_WIDTH // FOX_HEADS
Q_BLOCK = 128
POOL_GROUPS = 4
POOL_WIDTH = D_MODEL
POOL_GROUP_DIM = POOL_WIDTH // POOL_GROUPS
POOL_WINDOWS = (2, 4, 8, 16)
N_BRANCH = 3
EPS = 1e-6

SPLIT_SIZES = ((ML_WIDTH,) * 5 + (ML_HEADS,) * 2 + (FOX_WIDTH,) * 4 + (FOX_HEADS,)
               + (POOL_WIDTH,) * 2 + (N_BRANCH * D_MODEL,))
N_IN = sum(SPLIT_SIZES)
SPLIT_POINTS = tuple(np.cumsum(SPLIT_SIZES)[:-1].tolist())

kernel_name = "hybrid_mlstm_fox_pool_gated"


def rmsnorm(x, g):
    xf = x.astype(jnp.float32)
    r = lax.rsqrt(jnp.mean(xf * xf, axis=-1, keepdims=True) + EPS)
    return (xf * r).astype(x.dtype) * g


def causal_depthwise_conv(x, w):
    K = w.shape[0]
    S = x.shape[1]
    xp = jnp.pad(x, ((0, 0), (K - 1, 0), (0, 0)))
    out = xp[:, 0:S] * w[0]
    for kk in range(1, K):
        out = out + xp[:, kk:kk + S] * w[kk]
    return out


def mlstm_chunkwise(q, k, v, i_pre, f_pre):
    B, S, H, dh = q.shape
    L = ML_CHUNK
    nc = S // L
    q = q.astype(jnp.float32) * (dh ** -0.5)
    k = k.astype(jnp.float32)
    v = v.astype(jnp.float32)
    log_f = jax.nn.log_sigmoid(f_pre.astype(jnp.float32))
    i_g = i_pre.astype(jnp.float32)

    def to_chunks(a):
        a = a.reshape((B, nc, L, H) + a.shape[3:])
        return jnp.moveaxis(a, (1, 3), (0, 2))

    tri = jnp.tril(jnp.ones((L, L), dtype=bool))

    def step(carry, inp):
        C, n, m = carry
        qc, kc, vc, ic, fc = inp
        b = jnp.cumsum(fc, axis=-1)
        D = b[..., :, None] - b[..., None, :] + ic[..., None, :]
        D = jnp.where(tri, D, -jnp.inf)
        inter = b + m[..., None]
        m_t = jnp.maximum(inter, jnp.max(D, axis=-1))
        w_inter = jnp.exp(inter - m_t)
        P = jnp.exp(D - m_t[..., None]) * jnp.einsum('bhtd,bhsd->bhts', qc, kc)
        num = (w_inter[..., None] * jnp.einsum('bhvk,bhtk->bhtv', C, qc)
               + jnp.einsum('bhts,bhsv->bhtv', P, vc))
        den = w_inter * jnp.einsum('bhk,bhtk->bht', n, qc) + jnp.sum(P, axis=-1)
        h = num / jnp.maximum(jnp.abs(den), jnp.exp(-m_t))[..., None]
        bL = b[..., -1]
        dec = bL[..., None] - b + ic
        m_new = jnp.maximum(bL + m, jnp.max(dec, axis=-1))
        w_s = jnp.exp(dec - m_new[..., None])
        w_old = jnp.exp(bL + m - m_new)
        C_new = w_old[..., None, None] * C + jnp.einsum('bhsv,bhsk->bhvk', vc * w_s[..., None], kc)
        n_new = w_old[..., None] * n + jnp.einsum('bhs,bhsk->bhk', w_s, kc)
        return (C_new, n_new, m_new), h

    init = (jnp.zeros((B, H, dh, dh), jnp.float32),
            jnp.zeros((B, H, dh), jnp.float32),
            jnp.zeros((B, H), jnp.float32))
    _, hs = lax.scan(step, init, (to_chunks(q), to_chunks(k), to_chunks(v),
                                  to_chunks(i_g), to_chunks(log_f)))
    hs = jnp.moveaxis(hs, (0, 2), (1, 3))
    return hs.reshape(B, S, H, dh)


def forgetting_attention(q, k, v, f_pre):
    B, S, H, dh = q.shape
    scale = dh ** -0.5
    q = jnp.moveaxis(q, 2, 1)
    k = jnp.moveaxis(k, 2, 1)
    v = jnp.moveaxis(v, 2, 1)
    F = jnp.moveaxis(jnp.cumsum(jax.nn.log_sigmoid(f_pre.astype(jnp.float32)), axis=1), 2, 1)
    k_pos = jnp.arange(S)

    def block(i):
        start = i * Q_BLOCK
        qb = lax.dynamic_slice_in_dim(q, start, Q_BLOCK, axis=2)
        Fb = lax.dynamic_slice_in_dim(F, start, Q_BLOCK, axis=2)
        s = (jnp.einsum('bhqd,bhkd->bhqk', qb, k).astype(jnp.float32) * scale
             + Fb[..., :, None] - F[..., None, :])
        q_pos = start + jnp.arange(Q_BLOCK)
        s = jnp.where(k_pos[None, :] <= q_pos[:, None], s, -jnp.inf)
        p = jax.nn.softmax(s, axis=-1).astype(v.dtype)
        return jnp.einsum('bhqk,bhkd->bhqd', p, v)

    out = lax.map(block, jnp.arange(S // Q_BLOCK))
    out = jnp.moveaxis(out, 0, 2).reshape(B, H, S, dh)
    return jnp.moveaxis(out, 1, 2)


def multiscale_pool(u, pool_w, pool_scale):
    B, S, W = u.shape
    uf = u.astype(jnp.float32)
    cs = jnp.concatenate([jnp.zeros((B, 1, W), jnp.float32), jnp.cumsum(uf, axis=1)], axis=1)
    cs = cs.reshape(B, S + 1, POOL_GROUPS, POOL_GROUP_DIM)
    ug = uf.reshape(B, S, POOL_GROUPS, POOL_GROUP_DIM)
    t = jnp.arange(S)[:, None]
    win = jnp.array(POOL_WINDOWS, dtype=jnp.int32)[None, :]
    lo = jnp.maximum(t + 1 - win, 0)
    g_idx = jnp.arange(POOL_GROUPS)[None, :]
    win_sum = cs[:, 1:] - cs[:, lo, g_idx]
    cnt = jnp.minimum(t + 1, win).astype(jnp.float32)
    d = win_sum / cnt[None, :, :, None] - ug
    y = jnp.einsum('bsgc,gcd->bsgd', d, pool_w.astype(jnp.float32)).reshape(B, S, W)
    return (y * pool_scale).astype(u.dtype)


def hybrid_layer(x, norm_g, w_in, conv_w, ml_bi, ml_bf, ml_norm_g, fox_bf,
                 pool_w, pool_scale, w_branch, w_out):
    B, S, _ = x.shape
    h = rmsnorm(x, norm_g)
    proj = jnp.einsum('bsd,dn->bsn', h, w_in)
    (aq, ak, av, ao, az, ai, af, bq, bk, bv, bz, bf, cu, cz, gates) = jnp.split(proj, SPLIT_POINTS, axis=-1)

    qk = jax.nn.silu(causal_depthwise_conv(jnp.concatenate([aq, ak], axis=-1), conv_w))
    aq, ak = jnp.split(qk, 2, axis=-1)
    hA = mlstm_chunkwise(aq.reshape(B, S, ML_HEADS, ML_HEAD_DIM),
                         ak.reshape(B, S, ML_HEADS, ML_HEAD_DIM),
                         av.reshape(B, S, ML_HEADS, ML_HEAD_DIM),
                         ai + ml_bi, af + ml_bf)
    hA = hA * lax.rsqrt(jnp.mean(hA * hA, axis=-1, keepdims=True) + EPS)
    yA = (hA.reshape(B, S, ML_WIDTH).astype(x.dtype) * ml_norm_g
          * jax.nn.sigmoid(ao) * jax.nn.silu(az))

    hB = forgetting_attention(bq.reshape(B, S, FOX_HEADS, FOX_HEAD_DIM),
                              bk.reshape(B, S, FOX_HEADS, FOX_HEAD_DIM),
                              bv.reshape(B, S, FOX_HEADS, FOX_HEAD_DIM),
                              bf + fox_bf)
    yB = hB.reshape(B, S, FOX_WIDTH) * jax.nn.silu(bz)

    yC = multiscale_pool(cu, pool_w, pool_scale) * jax.nn.silu(cz)

    ys = jnp.stack([yA, yB, yC], axis=2)
    yb = jnp.einsum('bsnw,nwd->bsnd', ys, w_branch)
    g = jax.nn.sigmoid(gates.reshape(B, S, N_BRANCH, D_MODEL))
    merged = jnp.sum(g * yb, axis=2)
    return x + jnp.einsum('bsd,de->bse', merged, w_out)


def setup_inputs(seed: int = 0) -> dict:
    key = jax.random.key(seed)
    ks = jax.random.split(key, 16)
    f32 = jnp.float32
    nrm = lambda k, shape: jax.random.normal(k, shape, f32)
    x = nrm(ks[0], (BATCH, SEQ, D_MODEL))
    norm_g = 1.0 + 0.02 * nrm(ks[1], (DEPTH, D_MODEL))
    w_in = nrm(ks[2], (DEPTH, D_MODEL, N_IN)) * D_MODEL ** -0.5
    conv_w = nrm(ks[3], (DEPTH, CONV_WIDTH, 2 * ML_WIDTH)) * CONV_WIDTH ** -0.5
    ml_bi = 0.1 * nrm(ks[4], (DEPTH, ML_HEADS))
    ml_bf = jnp.linspace(3.0, 6.0, ML_HEADS, dtype=f32)[None, :] + 0.1 * nrm(ks[5], (DEPTH, ML_HEADS))
    ml_norm_g = 1.0 + 0.02 * nrm(ks[6], (DEPTH, ML_WIDTH))
    fox_bf = jnp.linspace(0.0, 4.0, FOX_HEADS, dtype=f32)[None, :] + 0.1 * nrm(ks[7], (DEPTH, FOX_HEADS))
    pool_w = nrm(ks[8], (DEPTH, POOL_GROUPS, POOL_GROUP_DIM, POOL_GROUP_DIM)) * POOL_GROUP_DIM ** -0.5
    pool_scale = 1.0 + 0.02 * nrm(ks[9], (DEPTH, POOL_WIDTH))
    w_branch = nrm(ks[10], (DEPTH, N_BRANCH, ML_WIDTH, D_MODEL)) * ML_WIDTH ** -0.5
    w_out = nrm(ks[11], (DEPTH, D_MODEL, D_MODEL)) * D_MODEL ** -0.5
    final_g = 1.0 + 0.02 * nrm(ks[12], (D_MODEL,))
    return {"x": x, "norm_g": norm_g, "w_in": w_in, "conv_w": conv_w, "ml_bi": ml_bi,
            "ml_bf": ml_bf, "ml_norm_g": ml_norm_g, "fox_bf": fox_bf, "pool_w": pool_w,
            "pool_scale": pool_scale, "w_branch": w_branch, "w_out": w_out, "final_g": final_g}


def reference(x, norm_g, w_in, conv_w, ml_bi, ml_bf, ml_norm_g, fox_bf,
              pool_w, pool_scale, w_branch, w_out, final_g):
    for l in range(DEPTH):
        x = hybrid_layer(x, norm_g[l], w_in[l], conv_w[l], ml_bi[l], ml_bf[l], ml_norm_g[l],
                         fox_bf[l], pool_w[l], pool_scale[l], w_branch[l], w_out[l])
    return rmsnorm(x, final_g)
```

```python
import functools

import jax
import jax.numpy as jnp
from jax import lax
from jax.experimental import pallas as pl
from jax.experimental.pallas import tpu as pltpu

D_MODEL = 1024
ML_HEADS = 4
ML_HEAD_DIM = 256
CONV_WIDTH = 4
FOX_HEADS = 8
FOX_HEAD_DIM = 128
POOL_GROUPS = 4
POOL_GROUP_DIM = 256
POOL_WINDOWS = (2, 4, 8, 16)
N_BRANCH = 3
EPS = 1e-6

LANES = 128
HALO = 16
CONV_HALO = 8
NEG = -1e30

C_GATES = 0
C_AQ, C_AK, C_AV, C_AO, C_AZ, C_BQ, C_BK, C_BV, C_BZ, C_CU, C_CZ = range(N_BRANCH, N_BRANCH + 11)
N_MAIN = 14 * D_MODEL
G_AI, G_AF, G_BF = 0, 4, 8

ML_CHUNK = 256
IN_TM, IN_TN = 1024, 1024
FOX_TQ = 512
FOX_TK = 512
MERGE_TM = 512
VMEM_LIMIT = 56 * 1024 * 1024


def _sigmoid(x):
    return 1.0 / (1.0 + jnp.exp(-x))


def _silu(x):
    return x * _sigmoid(x)


def _in_proj_kernel(x_ref, g_ref, w_ref, wg_ref, proj_ref, gate_ref, h_scr):
    @pl.when(pl.program_id(1) == 0)
    def _():
        x = x_ref[...]
        r = lax.rsqrt(jnp.mean(x * x, axis=-1, keepdims=True) + EPS)
        h = ((x * r) * g_ref[...]).astype(jnp.bfloat16)
        h_scr[...] = h
        gate_ref[...] = jnp.dot(h, wg_ref[...], preferred_element_type=jnp.float32)

    proj_ref[...] = jnp.dot(h_scr[...], w_ref[...],
                            preferred_element_type=jnp.float32).astype(jnp.bfloat16)


def _in_proj(x2, norm_g, w_main, w_gate):
    m = x2.shape[0]
    return pl.pallas_call(
        _in_proj_kernel,
        grid=(m // IN_TM, N_MAIN // IN_TN),
        in_specs=[
            pl.BlockSpec((IN_TM, D_MODEL), lambda i, j: (i, 0)),
            pl.BlockSpec((1, D_MODEL), lambda i, j: (0, 0)),
            pl.BlockSpec((D_MODEL, IN_TN), lambda i, j: (0, j)),
            pl.BlockSpec((D_MODEL, LANES), lambda i, j: (0, 0)),
        ],
        out_specs=[
            pl.BlockSpec((IN_TM, IN_TN), lambda i, j: (i, j)),
            pl.BlockSpec((IN_TM, LANES), lambda i, j: (i, 0)),
        ],
        out_shape=[
            jax.ShapeDtypeStruct((m, N_MAIN), jnp.bfloat16),
            jax.ShapeDtypeStruct((m, LANES), jnp.float32),
        ],
        scratch_shapes=[pltpu.VMEM((IN_TM, D_MODEL), jnp.bfloat16)],
        compiler_params=pltpu.CompilerParams(
            dimension_semantics=("arbitrary", "arbitrary"), vmem_limit_bytes=VMEM_LIMIT),
        name="in_proj",
    )(x2, norm_g, w_main, w_gate)


def _gates_kernel(g_ref, bias_ref, col_ref, row_ref, carry_scr):
    @pl.when(pl.program_id(1) == 0)
    def _():
        carry_scr[...] = jnp.zeros_like(carry_scr)

    pre = g_ref[...] + bias_ref[...]
    ls = jnp.minimum(pre, 0.0) - jnp.log1p(jnp.exp(-jnp.abs(pre)))
    row = lax.broadcasted_iota(jnp.int32, ls.shape, 0)
    lane = lax.broadcasted_iota(jnp.int32, ls.shape, 1)
    cs = ls
    k = 1
    while k < ML_CHUNK:
        cs = cs + jnp.where(row >= k, pltpu.roll(cs, k, 0), 0.0)
        k *= 2
    run = cs + carry_scr[...]
    carry_scr[...] = run[ML_CHUNK - 1:ML_CHUNK, :]
    out = jnp.where(lane < G_AF, pre, jnp.where(lane < G_BF, cs, run))
    col_ref[...] = out
    row_ref[0] = out.T[0:16, :]


def _gates(gate_pre, gate_bias, batch, seq):
    m = gate_pre.shape[0]
    nc = seq // ML_CHUNK
    return pl.pallas_call(
        _gates_kernel,
        grid=(batch, nc),
        in_specs=[
            pl.BlockSpec((ML_CHUNK, LANES), lambda b, c: (b * nc + c, 0)),
            pl.BlockSpec((1, LANES), lambda b, c: (0, 0)),
        ],
        out_specs=[
            pl.BlockSpec((ML_CHUNK, LANES), lambda b, c: (b * nc + c, 0)),
            pl.BlockSpec((1, 16, ML_CHUNK), lambda b, c: (b, 0, c)),
        ],
        out_shape=[
            jax.ShapeDtypeStruct((m, LANES), jnp.float32),
            jax.ShapeDtypeStruct((batch, 16, seq), jnp.float32),
        ],
        scratch_shapes=[pltpu.VMEM((1, LANES), jnp.float32)],
        compiler_params=pltpu.CompilerParams(dimension_semantics=("arbitrary", "arbitrary")),
        name="gates",
    )(gate_pre, gate_bias)


def _mlstm_kernel(aq_ref, ak_ref, av_ref, ao_ref, az_ref, cw_ref, gc_ref, gr_ref, ng_ref,
                  y_ref, qbuf, kbuf, ct_scr, m_scr):
    L = ML_CHUNK
    dh = ML_HEAD_DIM

    @pl.when(pl.program_id(1) == 0)
    def _():
        qbuf[0:CONV_HALO, :] = jnp.zeros((CONV_HALO, D_MODEL), jnp.float32)
        kbuf[0:CONV_HALO, :] = jnp.zeros((CONV_HALO, D_MODEL), jnp.float32)
        ct_scr[...] = jnp.zeros_like(ct_scr)
        m_scr[...] = jnp.zeros_like(m_scr)

    qbuf[CONV_HALO:, :] = aq_ref[...].astype(jnp.float32)
    kbuf[CONV_HALO:, :] = ak_ref[...].astype(jnp.float32)

    def conv(buf, w):
        base = CONV_HALO - (CONV_WIDTH - 1)
        acc = buf[base:base + L, :] * w[0:1, :]
        for kk in range(1, CONV_WIDTH):
            acc = acc + buf[base + kk:base + kk + L, :] * w[kk:kk + 1, :]
        return acc

    cw = cw_ref[...]
    qc = (_silu(conv(qbuf, cw[:, :D_MODEL])) * (dh ** -0.5)).astype(jnp.bfloat16)
    kc = _silu(conv(kbuf, cw[:, D_MODEL:]))
    qbuf[0:CONV_HALO, :] = qbuf[L:L + CONV_HALO, :]
    kbuf[0:CONV_HALO, :] = kbuf[L:L + CONV_HALO, :]

    gcol = gc_ref[...]
    grow = gr_ref[0]
    t_idx = lax.broadcasted_iota(jnp.int32, (L, L), 0)
    s_idx = lax.broadcasted_iota(jnp.int32, (L, L), 1)
    causal = s_idx <= t_idx
    ones_col = (lax.broadcasted_iota(jnp.int32, (L, LANES), 1) == 0).astype(jnp.bfloat16)

    for h in range(ML_HEADS):
        sl = slice(h * dh, (h + 1) * dh)
        q = qc[:, sl]
        kf = kc[:, sl]
        kb = kf.astype(jnp.bfloat16)
        v_aug = jnp.concatenate([av_ref[:, sl], ones_col], axis=1)
        i_col = gcol[:, G_AI + h:G_AI + h + 1]
        b_col = gcol[:, G_AF + h:G_AF + h + 1]
        i_row = grow[G_AI + h:G_AI + h + 1, :]
        b_row = grow[G_AF + h:G_AF + h + 1, :]
        m_prev = m_scr[h:h + 1, 0:1]

        d = jnp.where(causal, b_col + (i_row - b_row), NEG)
        inter = b_col + m_prev
        m_t = jnp.maximum(inter, jnp.max(d, axis=-1, keepdims=True))
        w_inter = jnp.exp(inter - m_t)
        e = jnp.exp(d - m_t)
        s = lax.dot_general(q, kb, (((1,), (1,)), ((), ())), preferred_element_type=jnp.float32)
        p = (e * s).astype(jnp.bfloat16)
        ct = ct_scr[h]
        num_aug = (w_inter * jnp.dot(q, ct.astype(jnp.bfloat16), preferred_element_type=jnp.float32)
                   + jnp.dot(p, v_aug, preferred_element_type=jnp.float32))
        num = num_aug[:, :dh]
        den = num_aug[:, dh:dh + 1]
        hh = num / jnp.maximum(jnp.abs(den), jnp.exp(-m_t))
        hh = hh * lax.rsqrt(jnp.mean(hh * hh, axis=-1, keepdims=True) + EPS)
        y = (hh * ng_ref[:, sl] * _sigmoid(ao_ref[:, sl].astype(jnp.float32))
             * _silu(az_ref[:, sl].astype(jnp.float32)))
        y_ref[:, sl] = y.astype(jnp.bfloat16)

        b_last = b_col[L - 1:L, :]
        dec = b_last - b_col + i_col
        m_new = jnp.maximum(b_last + m_prev, jnp.max(dec, axis=0, keepdims=True))
        w_s = jnp.exp(dec - m_new)
        w_old = jnp.exp(b_last + m_prev - m_new)
        vw = (v_aug.astype(jnp.float32) * w_s).astype(jnp.bfloat16)
        upd = jnp.dot(kf.T.astype(jnp.bfloat16), vw, preferred_element_type=jnp.float32)
        ct_scr[h] = w_old * ct + upd
        m_scr[h:h + 1, :] = jnp.broadcast_to(m_new, (1, LANES))


def _mlstm(proj, conv_w, gcol, grow, ml_norm_g, batch, seq):
    m = proj.shape[0]
    L = ML_CHUNK
    nc = seq // L

    def col(c):
        return pl.BlockSpec((L, D_MODEL), lambda b, i, c=c: (b * nc + i, c))

    return pl.pallas_call(
        _mlstm_kernel,
        grid=(batch, nc),
        in_specs=[
            col(C_AQ), col(C_AK), col(C_AV), col(C_AO), col(C_AZ),
            pl.BlockSpec((CONV_WIDTH, 2 * D_MODEL), lambda b, i: (0, 0)),
            pl.BlockSpec((L, LANES), lambda b, i: (b * nc + i, 0)),
            pl.BlockSpec((1, 16, L), lambda b, i: (b, 0, i)),
            pl.BlockSpec((1, D_MODEL), lambda b, i: (0, 0)),
        ],
        out_specs=pl.BlockSpec((L, D_MODEL), lambda b, i: (b * nc + i, 0)),
        out_shape=jax.ShapeDtypeStruct((m, D_MODEL), jnp.bfloat16),
        scratch_shapes=[
            pltpu.VMEM((CONV_HALO + L, D_MODEL), jnp.float32),
            pltpu.VMEM((CONV_HALO + L, D_MODEL), jnp.float32),
            pltpu.VMEM((ML_HEADS, ML_HEAD_DIM, ML_HEAD_DIM + LANES), jnp.float32),
            pltpu.VMEM((8, LANES), jnp.float32),
        ],
        compiler_params=pltpu.CompilerParams(
            dimension_semantics=("arbitrary", "arbitrary"), vmem_limit_bytes=VMEM_LIMIT),
        name="mlstm",
    )(proj, proj, proj, proj, proj, conv_w, gcol, grow, ml_norm_g)


def _fox_kernel(q_ref, k_ref, v_ref, z_ref, g_ref, y_ref, qa_scr, ka_scr, *, seq):
    tq, tk = FOX_TQ, FOX_TK
    h = pl.program_id(1)
    i = pl.program_id(2)
    rows = 512

    @pl.when(i == 0)
    def _():
        lane = lax.broadcasted_iota(jnp.int32, (rows, LANES), 1)

        def build(c, carry):
            r0 = pl.multiple_of(c * rows, rows)
            g = g_ref[pl.ds(r0, rows), :]
            f = jnp.sum(jnp.where(lane == G_BF + h, g, 0.0), axis=1, keepdims=True)
            hi = f.astype(jnp.bfloat16).astype(jnp.float32)
            r1 = f - hi
            mid = r1.astype(jnp.bfloat16).astype(jnp.float32)
            lo = r1 - mid
            one = jnp.ones_like(f)
            zero = jnp.zeros_like(f)

            def lanes(vals):
                out = jnp.zeros((rows, LANES), jnp.float32)
                for idx, val in enumerate(vals):
                    out = jnp.where(lane == idx, val, out)
                return out.astype(jnp.bfloat16)

            qx = lanes([hi, mid, lo, one, one, one])
            kx = lanes([one, one, one, -hi, -mid, -lo])
            qs = (q_ref[pl.ds(r0, rows), :].astype(jnp.float32) * (FOX_HEAD_DIM ** -0.5)).astype(jnp.bfloat16)
            qa_scr[pl.ds(r0, rows), :] = jnp.concatenate([qs, qx], axis=1)
            ka_scr[pl.ds(r0, rows), :] = jnp.concatenate([k_ref[pl.ds(r0, rows), :], kx], axis=1)
            return carry

        lax.fori_loop(0, seq // rows, build, 0)

    q0 = pl.multiple_of(i * tq, tq)
    qa = qa_scr[pl.ds(q0, tq), :]

    def scores(k0):
        ka = ka_scr[pl.ds(k0, tk), :]
        return lax.dot_general(qa, ka, (((1,), (1,)), ((), ())), preferred_element_type=jnp.float32)

    def update(carry, s, k0):
        m, l, acc = carry
        m_new = jnp.maximum(m, jnp.max(s, axis=-1, keepdims=True))
        alpha = jnp.exp(m - m_new)
        p = jnp.exp(s - m_new)
        l = alpha * l + jnp.sum(p, axis=-1, keepdims=True)
        acc = alpha * acc + jnp.dot(p.astype(jnp.bfloat16), v_ref[pl.ds(k0, tk), :],
                                    preferred_element_type=jnp.float32)
        return m_new, l, acc

    def body(j, carry):
        k0 = pl.multiple_of(j * tk, tk)
        return update(carry, scores(k0), k0)

    init = (jnp.full((tq, 1), NEG, jnp.float32), jnp.zeros((tq, 1), jnp.float32),
            jnp.zeros((tq, FOX_HEAD_DIM), jnp.float32))
    carry = lax.fori_loop(0, i, body, init)
    r_idx = lax.broadcasted_iota(jnp.int32, (tq, tk), 0)
    c_idx = lax.broadcasted_iota(jnp.int32, (tq, tk), 1)
    s = jnp.where(c_idx <= r_idx, scores(q0), NEG)
    m, l, acc = update(carry, s, q0)
    y_ref[...] = (acc / l * _silu(z_ref[...].astype(jnp.float32))).astype(jnp.bfloat16)


def _fox(proj, gcol, batch, seq):
    m = proj.shape[0]
    nq = seq // FOX_TQ
    hb = D_MODEL // FOX_HEAD_DIM

    def head(c):
        return pl.BlockSpec((seq, FOX_HEAD_DIM), lambda b, h, i, c=c: (b, c * hb + h))

    return pl.pallas_call(
        functools.partial(_fox_kernel, seq=seq),
        grid=(batch, FOX_HEADS, nq),
        in_specs=[
            head(C_BQ), head(C_BK), head(C_BV),
            pl.BlockSpec((FOX_TQ, FOX_HEAD_DIM), lambda b, h, i: (b * nq + i, C_BZ * hb + h)),
            pl.BlockSpec((seq, LANES), lambda b, h, i: (b, 0)),
        ],
        out_specs=pl.BlockSpec((FOX_TQ, FOX_HEAD_DIM), lambda b, h, i: (b * nq + i, h)),
        out_shape=jax.ShapeDtypeStruct((m, D_MODEL), jnp.bfloat16),
        scratch_shapes=[
            pltpu.VMEM((seq, 2 * FOX_HEAD_DIM), jnp.bfloat16),
            pltpu.VMEM((seq, 2 * FOX_HEAD_DIM), jnp.bfloat16),
        ],
        compiler_params=pltpu.CompilerParams(
            dimension_semantics=("arbitrary", "arbitrary", "arbitrary"), vmem_limit_bytes=VMEM_LIMIT),
        name="fox",
    )(proj, proj, proj, proj, gcol)


def _merge_kernel(ya_ref, yb_ref, cu_ref, halo_ref, cz_ref, gates_ref, x_ref, pw_ref, ps_ref,
                  wb_ref, wo_ref, fg_ref, o_ref, *, seq, final):
    tm = MERGE_TM
    gd = POOL_GROUP_DIM
    i = pl.program_id(0)
    t0 = (i * tm) % seq
    pos = t0 + lax.broadcasted_iota(jnp.int32, (tm, 1), 0)
    halo = jnp.where(t0 == 0, 0.0, halo_ref[...].astype(jnp.float32))
    u = cu_ref[...].astype(jnp.float32)
    ext = jnp.concatenate([halo, u], axis=0)

    ys = []
    for g, win in enumerate(POOL_WINDOWS):
        sl = slice(g * gd, (g + 1) * gd)
        acc = ext[:, sl]
        step = 1
        while step < win:
            acc = acc + pltpu.roll(acc, step, 0)
            step *= 2
        cnt = jnp.minimum(pos + 1, win).astype(jnp.float32)
        d = acc[HALO:, :] / cnt - u[:, sl]
        ys.append(jnp.dot(d.astype(jnp.bfloat16), pw_ref[g], preferred_element_type=jnp.float32))
    yc = jnp.concatenate(ys, axis=1) * ps_ref[...]
    yc = yc * _silu(cz_ref[...].astype(jnp.float32))

    merged = jnp.zeros((tm, D_MODEL), jnp.float32)
    for n, y in enumerate((ya_ref[...], yb_ref[...], yc.astype(jnp.bfloat16))):
        yb = jnp.dot(y, wb_ref[n], preferred_element_type=jnp.float32)
        gate = _sigmoid(gates_ref[:, n * D_MODEL:(n + 1) * D_MODEL].astype(jnp.float32))
        merged = merged + gate * yb
    out = x_ref[...] + jnp.dot(merged.astype(jnp.bfloat16), wo_ref[...],
                               preferred_element_type=jnp.float32)
    if final:
        r = lax.rsqrt(jnp.mean(out * out, axis=-1, keepdims=True) + EPS)
        out = (out * r) * fg_ref[...]
    o_ref[...] = out


def _merge(ya, yb, proj, x2, pool_w, pool_scale, w_branch, w_out, final_g, seq, final):
    m = x2.shape[0]
    tm = MERGE_TM
    hpt = tm // HALO

    def rowblk(c, width=1):
        return pl.BlockSpec((tm, width * D_MODEL), lambda i, c=c: (i, c))

    const2 = lambda i: (0, 0)
    const3 = lambda i: (0, 0, 0)
    return pl.pallas_call(
        functools.partial(_merge_kernel, seq=seq, final=final),
        grid=(m // tm,),
        in_specs=[
            rowblk(0), rowblk(0), rowblk(C_CU),
            pl.BlockSpec((HALO, D_MODEL), lambda i: (jnp.maximum(i * hpt - 1, 0), C_CU)),
            rowblk(C_CZ),
            pl.BlockSpec((tm, N_BRANCH * D_MODEL), lambda i: (i, C_GATES // N_BRANCH)),
            rowblk(0),
            pl.BlockSpec((POOL_GROUPS, POOL_GROUP_DIM, POOL_GROUP_DIM), const3),
            pl.BlockSpec((1, D_MODEL), const2),
            pl.BlockSpec((N_BRANCH, D_MODEL, D_MODEL), const3),
            pl.BlockSpec((D_MODEL, D_MODEL), const2),
            pl.BlockSpec((1, D_MODEL), const2),
        ],
        out_specs=pl.BlockSpec((tm, D_MODEL), lambda i: (i, 0)),
        out_shape=jax.ShapeDtypeStruct((m, D_MODEL), jnp.float32),
        compiler_params=pltpu.CompilerParams(
            dimension_semantics=("arbitrary",), vmem_limit_bytes=VMEM_LIMIT),
        name="merge",
    )(ya, yb, proj, proj, proj, proj, x2, pool_w, pool_scale, w_branch, w_out, final_g)


def _split_w_in(w_in):
    o_ai = 5 * D_MODEL
    o_bq = o_ai + 2 * ML_HEADS
    o_bf = o_bq + 4 * D_MODEL
    o_cu = o_bf + FOX_HEADS
    o_g = o_cu + 2 * D_MODEL
    w_main = jnp.concatenate([w_in[:, o_g:], w_in[:, :o_ai], w_in[:, o_bq:o_bf], w_in[:, o_cu:o_g]], axis=1)
    w_gate = jnp.concatenate([w_in[:, o_ai:o_bq], w_in[:, o_bf:o_cu]], axis=1)
    w_gate = jnp.pad(w_gate, ((0, 0), (0, LANES - w_gate.shape[1])))
    return w_main.astype(jnp.bfloat16), w_gate.astype(jnp.bfloat16)


def kernel(x, norm_g, w_in, conv_w, ml_bi, ml_bf, ml_norm_g, fox_bf, pool_w, pool_scale, w_branch,
           w_out, final_g):
    batch, seq, d = x.shape
    depth = norm_g.shape[0]
    assert d == D_MODEL and seq % max(IN_TM, FOX_TQ, MERGE_TM, ML_CHUNK) == 0
    x2 = x.reshape(batch * seq, d)
    for l in range(depth):
        w_main, w_gate = _split_w_in(w_in[l])
        gate_bias = jnp.pad(jnp.concatenate([ml_bi[l], ml_bf[l], fox_bf[l]]),
                            (0, LANES - 2 * ML_HEADS - FOX_HEADS)).reshape(1, LANES)
        proj, gate_pre = _in_proj(x2, norm_g[l].reshape(1, d), w_main, w_gate)
        gcol, grow = _gates(gate_pre, gate_bias, batch, seq)
        ya = _mlstm(proj, conv_w[l], gcol, grow, ml_norm_g[l].reshape(1, d), batch, seq)
        yb = _fox(proj, gcol, batch, seq)
        x2 = _merge(ya, yb, proj, x2, pool_w[l].astype(jnp.bfloat16), pool_scale[l].reshape(1, d),
                    w_branch[l].astype(jnp.bfloat16), w_out[l].astype(jnp.bfloat16),
                    final_g.reshape(1, d), seq, final=(l == depth - 1))
    return x2.reshape(batch, seq, d)
```

```python
import functools

import jax
import jax.numpy as jnp
from jax import lax
from jax.experimental import pallas as pl
from jax.experimental.pallas import tpu as pltpu

D_MODEL = 1024
ML_HEADS = 4
ML_HEAD_DIM = 256
CONV_WIDTH = 4
FOX_HEADS = 8
FOX_HEAD_DIM = 128
POOL_GROUPS = 4
POOL_GROUP_DIM = 256
POOL_WINDOWS = (2, 4, 8, 16)
N_BRANCH = 3
EPS = 1e-6

LANES = 128
HALO = 16
CONV_HALO = 8
NEG = -1e30
SKIP_MARGIN = 40.0

C_GATES = 0
C_AQ, C_AK, C_AV, C_AO, C_AZ, C_BQ, C_BK, C_BV, C_BZ, C_CU, C_CZ = range(N_BRANCH, N_BRANCH + 11)
N_MAIN = 14 * D_MODEL
G_AI, G_AF, G_BF = 0, 4, 8

ML_CHUNK = 256
IN_TM, IN_TN = 1024, 1024
FOX_TQ = 512
FOX_TK = 512
MERGE_TM = 512
VMEM_LIMIT = 56 * 1024 * 1024


def _sigmoid(x):
    return 1.0 / (1.0 + jnp.exp(-x))


def _silu(x):
    return x * _sigmoid(x)


def _in_proj_kernel(x_ref, g_ref, w_ref, wg_ref, proj_ref, gate_ref, h_scr):
    @pl.when(pl.program_id(1) == 0)
    def _():
        x = x_ref[...]
        r = lax.rsqrt(jnp.mean(x * x, axis=-1, keepdims=True) + EPS)
        h = ((x * r) * g_ref[...]).astype(jnp.bfloat16)
        h_scr[...] = h
        gate_ref[...] = jnp.dot(h, wg_ref[...], preferred_element_type=jnp.float32)

    proj_ref[...] = jnp.dot(h_scr[...], w_ref[...],
                            preferred_element_type=jnp.float32).astype(jnp.bfloat16)


def _in_proj(x2, norm_g, w_main, w_gate):
    m = x2.shape[0]
    return pl.pallas_call(
        _in_proj_kernel,
        grid=(m // IN_TM, N_MAIN // IN_TN),
        in_specs=[
            pl.BlockSpec((IN_TM, D_MODEL), lambda i, j: (i, 0)),
            pl.BlockSpec((1, D_MODEL), lambda i, j: (0, 0)),
            pl.BlockSpec((D_MODEL, IN_TN), lambda i, j: (0, j)),
            pl.BlockSpec((D_MODEL, LANES), lambda i, j: (0, 0)),
        ],
        out_specs=[
            pl.BlockSpec((IN_TM, IN_TN), lambda i, j: (i, j)),
            pl.BlockSpec((IN_TM, LANES), lambda i, j: (i, 0)),
        ],
        out_shape=[
            jax.ShapeDtypeStruct((m, N_MAIN), jnp.bfloat16),
            jax.ShapeDtypeStruct((m, LANES), jnp.float32),
        ],
        scratch_shapes=[pltpu.VMEM((IN_TM, D_MODEL), jnp.bfloat16)],
        compiler_params=pltpu.CompilerParams(
            dimension_semantics=("arbitrary", "arbitrary"), vmem_limit_bytes=VMEM_LIMIT),
        name="in_proj",
    )(x2, norm_g, w_main, w_gate)


def _gates_kernel(g_ref, bias_ref, col_ref, row_ref, carry_scr):
    @pl.when(pl.program_id(1) == 0)
    def _():
        carry_scr[...] = jnp.zeros_like(carry_scr)

    pre = g_ref[...] + bias_ref[...]
    ls = jnp.minimum(pre, 0.0) - jnp.log1p(jnp.exp(-jnp.abs(pre)))
    row = lax.broadcasted_iota(jnp.int32, ls.shape, 0)
    lane = lax.broadcasted_iota(jnp.int32, ls.shape, 1)
    cs = ls
    k = 1
    while k < ML_CHUNK:
        cs = cs + jnp.where(row >= k, pltpu.roll(cs, k, 0), 0.0)
        k *= 2
    run = cs + carry_scr[...]
    carry_scr[...] = run[ML_CHUNK - 1:ML_CHUNK, :]
    out = jnp.where(lane < G_AF, pre, jnp.where(lane < G_BF, cs, run))
    col_ref[...] = out
    row_ref[0] = out.T[0:16, :]


def _gates(gate_pre, gate_bias, batch, seq):
    m = gate_pre.shape[0]
    nc = seq // ML_CHUNK
    return pl.pallas_call(
        _gates_kernel,
        grid=(batch, nc),
        in_specs=[
            pl.BlockSpec((ML_CHUNK, LANES), lambda b, c: (b * nc + c, 0)),
            pl.BlockSpec((1, LANES), lambda b, c: (0, 0)),
        ],
        out_specs=[
            pl.BlockSpec((ML_CHUNK, LANES), lambda b, c: (b * nc + c, 0)),
            pl.BlockSpec((1, 16, ML_CHUNK), lambda b, c: (b, 0, c)),
        ],
        out_shape=[
            jax.ShapeDtypeStruct((m, LANES), jnp.float32),
            jax.ShapeDtypeStruct((batch, 16, seq), jnp.float32),
        ],
        scratch_shapes=[pltpu.VMEM((1, LANES), jnp.float32)],
        compiler_params=pltpu.CompilerParams(dimension_semantics=("arbitrary", "arbitrary")),
        name="gates",
    )(gate_pre, gate_bias)


def _mlstm_kernel(aq_ref, ak_ref, av_ref, ao_ref, az_ref, cw_ref, gc_ref, gr_ref, ng_ref,
                  y_ref, qbuf, kbuf, ct_scr, m_scr):
    L = ML_CHUNK
    dh = ML_HEAD_DIM

    @pl.when(pl.program_id(1) == 0)
    def _():
        qbuf[0:CONV_HALO, :] = jnp.zeros((CONV_HALO, D_MODEL), jnp.float32)
        kbuf[0:CONV_HALO, :] = jnp.zeros((CONV_HALO, D_MODEL), jnp.float32)
        ct_scr[...] = jnp.zeros_like(ct_scr)
        m_scr[...] = jnp.zeros_like(m_scr)

    qbuf[CONV_HALO:, :] = aq_ref[...].astype(jnp.float32)
    kbuf[CONV_HALO:, :] = ak_ref[...].astype(jnp.float32)

    def conv(buf, w):
        base = CONV_HALO - (CONV_WIDTH - 1)
        acc = buf[base:base + L, :] * w[0:1, :]
        for kk in range(1, CONV_WIDTH):
            acc = acc + buf[base + kk:base + kk + L, :] * w[kk:kk + 1, :]
        return acc

    cw = cw_ref[...]
    qc = (_silu(conv(qbuf, cw[:, :D_MODEL])) * (dh ** -0.5)).astype(jnp.bfloat16)
    kc = _silu(conv(kbuf, cw[:, D_MODEL:]))
    qbuf[0:CONV_HALO, :] = qbuf[L:L + CONV_HALO, :]
    kbuf[0:CONV_HALO, :] = kbuf[L:L + CONV_HALO, :]

    gcol = gc_ref[...]
    grow = gr_ref[0]
    t_idx = lax.broadcasted_iota(jnp.int32, (L, L), 0)
    s_idx = lax.broadcasted_iota(jnp.int32, (L, L), 1)
    causal = s_idx <= t_idx
    ones_col = (lax.broadcasted_iota(jnp.int32, (L, LANES), 1) == 0).astype(jnp.bfloat16)

    for h in range(ML_HEADS):
        sl = slice(h * dh, (h + 1) * dh)
        q = qc[:, sl]
        kf = kc[:, sl]
        kb = kf.astype(jnp.bfloat16)
        v_aug = jnp.concatenate([av_ref[:, sl], ones_col], axis=1)
        i_col = gcol[:, G_AI + h:G_AI + h + 1]
        b_col = gcol[:, G_AF + h:G_AF + h + 1]
        i_row = grow[G_AI + h:G_AI + h + 1, :]
        b_row = grow[G_AF + h:G_AF + h + 1, :]
        m_prev = m_scr[h:h + 1, 0:1]

        d = jnp.where(causal, b_col + (i_row - b_row), NEG)
        inter = b_col + m_prev
        m_t = jnp.maximum(inter, jnp.max(d, axis=-1, keepdims=True))
        w_inter = jnp.exp(inter - m_t)
        e = jnp.exp(d - m_t)
        s = lax.dot_general(q, kb, (((1,), (1,)), ((), ())), preferred_element_type=jnp.float32)
        p = (e * s).astype(jnp.bfloat16)
        ct = ct_scr[h]
        num_aug = (w_inter * jnp.dot(q, ct.astype(jnp.bfloat16), preferred_element_type=jnp.float32)
                   + jnp.dot(p, v_aug, preferred_element_type=jnp.float32))
        num = num_aug[:, :dh]
        den = num_aug[:, dh:dh + 1]
        hh = num / jnp.maximum(jnp.abs(den), jnp.exp(-m_t))
        hh = hh * lax.rsqrt(jnp.mean(hh * hh, axis=-1, keepdims=True) + EPS)
        y = (hh * ng_ref[:, sl] * _sigmoid(ao_ref[:, sl].astype(jnp.float32))
             * _silu(az_ref[:, sl].astype(jnp.float32)))
        y_ref[:, sl] = y.astype(jnp.bfloat16)

        b_last = b_col[L - 1:L, :]
        dec = b_last - b_col + i_col
        m_new = jnp.maximum(b_last + m_prev, jnp.max(dec, axis=0, keepdims=True))
        w_s = jnp.exp(dec - m_new)
        w_old = jnp.exp(b_last + m_prev - m_new)
        vw = (v_aug.astype(jnp.float32) * w_s).astype(jnp.bfloat16)
        upd = jnp.dot(kf.T.astype(jnp.bfloat16), vw, preferred_element_type=jnp.float32)
        ct_scr[h] = w_old * ct + upd
        m_scr[h:h + 1, :] = jnp.broadcast_to(m_new, (1, LANES))


def _mlstm(proj, conv_w, gcol, grow, ml_norm_g, batch, seq):
    m = proj.shape[0]
    L = ML_CHUNK
    nc = seq // L

    def col(c):
        return pl.BlockSpec((L, D_MODEL), lambda b, i, c=c: (b * nc + i, c))

    return pl.pallas_call(
        _mlstm_kernel,
        grid=(batch, nc),
        in_specs=[
            col(C_AQ), col(C_AK), col(C_AV), col(C_AO), col(C_AZ),
            pl.BlockSpec((CONV_WIDTH, 2 * D_MODEL), lambda b, i: (0, 0)),
            pl.BlockSpec((L, LANES), lambda b, i: (b * nc + i, 0)),
            pl.BlockSpec((1, 16, L), lambda b, i: (b, 0, i)),
            pl.BlockSpec((1, D_MODEL), lambda b, i: (0, 0)),
        ],
        out_specs=pl.BlockSpec((L, D_MODEL), lambda b, i: (b * nc + i, 0)),
        out_shape=jax.ShapeDtypeStruct((m, D_MODEL), jnp.bfloat16),
        scratch_shapes=[
            pltpu.VMEM((CONV_HALO + L, D_MODEL), jnp.float32),
            pltpu.VMEM((CONV_HALO + L, D_MODEL), jnp.float32),
            pltpu.VMEM((ML_HEADS, ML_HEAD_DIM, ML_HEAD_DIM + LANES), jnp.float32),
            pltpu.VMEM((8, LANES), jnp.float32),
        ],
        compiler_params=pltpu.CompilerParams(
            dimension_semantics=("arbitrary", "arbitrary"), vmem_limit_bytes=VMEM_LIMIT),
        name="mlstm",
    )(proj, proj, proj, proj, proj, conv_w, gcol, grow, ml_norm_g)


def _fox_kernel(q_ref, k_ref, v_ref, z_ref, g_ref, y_ref, qa_scr, ka_scr, st_scr, *, seq):
    tq, tk = FOX_TQ, FOX_TK
    h = pl.program_id(1)
    i = pl.program_id(2)
    rows = tk
    lane1 = lax.broadcasted_iota(jnp.int32, (1, LANES), 1)

    @pl.when(i == 0)
    def _():
        lane = lax.broadcasted_iota(jnp.int32, (rows, LANES), 1)

        def build(c, carry):
            kmax, qn_vec, g_vec, fe_vec = carry
            r0 = pl.multiple_of(c * rows, rows)
            g = g_ref[pl.ds(r0, rows), :]
            f = jnp.sum(jnp.where(lane == G_BF + h, g, 0.0), axis=1, keepdims=True)
            hi = f.astype(jnp.bfloat16).astype(jnp.float32)
            r1 = f - hi
            mid = r1.astype(jnp.bfloat16).astype(jnp.float32)
            lo = r1 - mid
            one = jnp.ones_like(f)

            def lanes(vals):
                out = jnp.zeros((rows, LANES), jnp.float32)
                for idx, val in enumerate(vals):
                    out = jnp.where(lane == idx, val, out)
                return out.astype(jnp.bfloat16)

            qx = lanes([hi, mid, lo, one, one, one])
            kx = lanes([one, one, one, -hi, -mid, -lo])
            qs = (q_ref[pl.ds(r0, rows), :].astype(jnp.float32) * (FOX_HEAD_DIM ** -0.5)).astype(jnp.bfloat16)
            kk = k_ref[pl.ds(r0, rows), :]
            qa_scr[pl.ds(r0, rows), :] = jnp.concatenate([qs, qx], axis=1)
            ka_scr[pl.ds(r0, rows), :] = jnp.concatenate([kk, kx], axis=1)

            qf = qs.astype(jnp.float32)
            kf = kk.astype(jnp.float32)
            qn = jnp.sqrt(jnp.sum(qf * qf, axis=1, keepdims=True))
            kn = jnp.sqrt(jnp.sum(kf * kf, axis=1, keepdims=True))
            diag = jnp.sum(qf * kf, axis=1, keepdims=True)
            here = lane1 == c
            kmax = jnp.maximum(kmax, jnp.max(kn, axis=0, keepdims=True))
            qn_vec = jnp.where(here, jnp.max(qn, axis=0, keepdims=True), qn_vec)
            g_vec = jnp.where(here, jnp.max(f - diag, axis=0, keepdims=True), g_vec)
            fe_vec = jnp.where(here, f[rows - 1:rows, :], fe_vec)
            return kmax, qn_vec, g_vec, fe_vec

        zrow = jnp.zeros((1, LANES), jnp.float32)
        kmax, qn_vec, g_vec, fe_vec = lax.fori_loop(
            0, seq // rows, build, (jnp.zeros((1, 1), jnp.float32), zrow, zrow, zrow))
        st_scr[0:1, :] = qn_vec * kmax + g_vec
        st_scr[1:2, :] = fe_vec

    bound_i = jnp.sum(jnp.where(lane1 == i, st_scr[0:1, :], 0.0), axis=1, keepdims=True)
    needed = (lane1 < i) & jnp.logical_not(st_scr[1:2, :] >= bound_i + SKIP_MARGIN)
    lo = jnp.min(jnp.where(needed, lane1, i))

    q0 = pl.multiple_of(i * tq, tq)
    qa = qa_scr[pl.ds(q0, tq), :]

    def scores(k0):
        ka = ka_scr[pl.ds(k0, tk), :]
        return lax.dot_general(qa, ka, (((1,), (1,)), ((), ())), preferred_element_type=jnp.float32)

    def update(carry, s, k0):
        m, l, acc = carry
        m_new = jnp.maximum(m, jnp.max(s, axis=-1, keepdims=True))
        alpha = jnp.exp(m - m_new)
        p = jnp.exp(s - m_new)
        l = alpha * l + jnp.sum(p, axis=-1, keepdims=True)
        acc = alpha * acc + jnp.dot(p.astype(jnp.bfloat16), v_ref[pl.ds(k0, tk), :],
                                    preferred_element_type=jnp.float32)
        return m_new, l, acc

    def body(j, carry):
        k0 = pl.multiple_of(j * tk, tk)
        return update(carry, scores(k0), k0)

    init = (jnp.full((tq, 1), NEG, jnp.float32), jnp.zeros((tq, 1), jnp.float32),
            jnp.zeros((tq, FOX_HEAD_DIM), jnp.float32))
    carry = lax.fori_loop(lo, i, body, init)
    r_idx = lax.broadcasted_iota(jnp.int32, (tq, tk), 0)
    c_idx = lax.broadcasted_iota(jnp.int32, (tq, tk), 1)
    s = jnp.where(c_idx <= r_idx, scores(q0), NEG)
    m, l, acc = update(carry, s, q0)
    y_ref[...] = (acc / l * _silu(z_ref[...].astype(jnp.float32))).astype(jnp.bfloat16)


def _fox(proj, gcol, batch, seq):
    m = proj.shape[0]
    nq = seq // FOX_TQ
    assert FOX_TQ == FOX_TK and nq <= LANES
    hb = D_MODEL // FOX_HEAD_DIM

    def head(c):
        return pl.BlockSpec((seq, FOX_HEAD_DIM), lambda b, h, i, c=c: (b, c * hb + h))

    return pl.pallas_call(
        functools.partial(_fox_kernel, seq=seq),
        grid=(batch, FOX_HEADS, nq),
        in_specs=[
            head(C_BQ), head(C_BK), head(C_BV),
            pl.BlockSpec((FOX_TQ, FOX_HEAD_DIM), lambda b, h, i: (b * nq + i, C_BZ * hb + h)),
            pl.BlockSpec((seq, LANES), lambda b, h, i: (b, 0)),
        ],
        out_specs=pl.BlockSpec((FOX_TQ, FOX_HEAD_DIM), lambda b, h, i: (b * nq + i, h)),
        out_shape=jax.ShapeDtypeStruct((m, D_MODEL), jnp.bfloat16),
        scratch_shapes=[
            pltpu.VMEM((seq, 2 * FOX_HEAD_DIM), jnp.bfloat16),
            pltpu.VMEM((seq, 2 * FOX_HEAD_DIM), jnp.bfloat16),
            pltpu.VMEM((8, LANES), jnp.float32),
        ],
        compiler_params=pltpu.CompilerParams(
            dimension_semantics=("arbitrary", "arbitrary", "arbitrary"), vmem_limit_bytes=VMEM_LIMIT),
        name="fox",
    )(proj, proj, proj, proj, gcol)


def _merge_kernel(ya_ref, yb_ref, cu_ref, halo_ref, cz_ref, gates_ref, x_ref, pw_ref, ps_ref,
                  wb_ref, wo_ref, fg_ref, o_ref, *, seq, final):
    tm = MERGE_TM
    gd = POOL_GROUP_DIM
    i = pl.program_id(0)
    t0 = (i * tm) % seq
    pos = t0 + lax.broadcasted_iota(jnp.int32, (tm, 1), 0)
    halo = jnp.where(t0 == 0, 0.0, halo_ref[...].astype(jnp.float32))
    u = cu_ref[...].astype(jnp.float32)
    ext = jnp.concatenate([halo, u], axis=0)

    ys = []
    for g, win in enumerate(POOL_WINDOWS):
        sl = slice(g * gd, (g + 1) * gd)
        acc = ext[:, sl]
        step = 1
        while step < win:
            acc = acc + pltpu.roll(acc, step, 0)
            step *= 2
        cnt = jnp.minimum(pos + 1, win).astype(jnp.float32)
        d = acc[HALO:, :] / cnt - u[:, sl]
        ys.append(jnp.dot(d.astype(jnp.bfloat16), pw_ref[g], preferred_element_type=jnp.float32))
    yc = jnp.concatenate(ys, axis=1) * ps_ref[...]
    yc = yc * _silu(cz_ref[...].astype(jnp.float32))

    merged = jnp.zeros((tm, D_MODEL), jnp.float32)
    for n, y in enumerate((ya_ref[...], yb_ref[...], yc.astype(jnp.bfloat16))):
        yb = jnp.dot(y, wb_ref[n], preferred_element_type=jnp.float32)
        gate = _sigmoid(gates_ref[:, n * D_MODEL:(n + 1) * D_MODEL].astype(jnp.float32))
        merged = merged + gate * yb
    out = x_ref[...] + jnp.dot(merged.astype(jnp.bfloat16), wo_ref[...],
                               preferred_element_type=jnp.float32)
    if final:
        r = lax.rsqrt(jnp.mean(out * out, axis=-1, keepdims=True) + EPS)
        out = (out * r) * fg_ref[...]
    o_ref[...] = out


def _merge(ya, yb, proj, x2, pool_w, pool_scale, w_branch, w_out, final_g, seq, final):
    m = x2.shape[0]
    tm = MERGE_TM
    hpt = tm // HALO

    def rowblk(c, width=1):
        return pl.BlockSpec((tm, width * D_MODEL), lambda i, c=c: (i, c))

    const2 = lambda i: (0, 0)
    const3 = lambda i: (0, 0, 0)
    return pl.pallas_call(
        functools.partial(_merge_kernel, seq=seq, final=final),
        grid=(m // tm,),
        in_specs=[
            rowblk(0), rowblk(0), rowblk(C_CU),
            pl.BlockSpec((HALO, D_MODEL), lambda i: (jnp.maximum(i * hpt - 1, 0), C_CU)),
            rowblk(C_CZ),
            pl.BlockSpec((tm, N_BRANCH * D_MODEL), lambda i: (i, C_GATES // N_BRANCH)),
            rowblk(0),
            pl.BlockSpec((POOL_GROUPS, POOL_GROUP_DIM, POOL_GROUP_DIM), const3),
            pl.BlockSpec((1, D_MODEL), const2),
            pl.BlockSpec((N_BRANCH, D_MODEL, D_MODEL), const3),
            pl.BlockSpec((D_MODEL, D_MODEL), const2),
            pl.BlockSpec((1, D_MODEL), const2),
        ],
        out_specs=pl.BlockSpec((tm, D_MODEL), lambda i: (i, 0)),
        out_shape=jax.ShapeDtypeStruct((m, D_MODEL), jnp.float32),
        compiler_params=pltpu.CompilerParams(
            dimension_semantics=("arbitrary",), vmem_limit_bytes=VMEM_LIMIT),
        name="merge",
    )(ya, yb, proj, proj, proj, proj, x2, pool_w, pool_scale, w_branch, w_out, final_g)


def _split_w_in(w_in):
    o_ai = 5 * D_MODEL
    o_bq = o_ai + 2 * ML_HEADS
    o_bf = o_bq + 4 * D_MODEL
    o_cu = o_bf + FOX_HEADS
    o_g = o_cu + 2 * D_MODEL
    w_main = jnp.concatenate([w_in[:, o_g:], w_in[:, :o_ai], w_in[:, o_bq:o_bf], w_in[:, o_cu:o_g]], axis=1)
    w_gate = jnp.concatenate([w_in[:, o_ai:o_bq], w_in[:, o_bf:o_cu]], axis=1)
    w_gate = jnp.pad(w_gate, ((0, 0), (0, LANES - w_gate.shape[1])))
    return w_main.astype(jnp.bfloat16), w_gate.astype(jnp.bfloat16)


def kernel(x, norm_g, w_in, conv_w, ml_bi, ml_bf, ml_norm_g, fox_bf, pool_w, pool_scale, w_branch,
           w_out, final_g):
    batch, seq, d = x.shape
    depth = norm_g.shape[0]
    assert d == D_MODEL and seq % max(IN_TM, FOX_TQ, MERGE_TM, ML_CHUNK) == 0
    x2 = x.reshape(batch * seq, d)
    for l in range(depth):
        w_main, w_gate = _split_w_in(w_in[l])
        gate_bias = jnp.pad(jnp.concatenate([ml_bi[l], ml_bf[l], fox_bf[l]]),
                            (0, LANES - 2 * ML_HEADS - FOX_HEADS)).reshape(1, LANES)
        proj, gate_pre = _in_proj(x2, norm_g[l].reshape(1, d), w_main, w_gate)
        gcol, grow = _gates(gate_pre, gate_bias, batch, seq)
        ya = _mlstm(proj, conv_w[l], gcol, grow, ml_norm_g[l].reshape(1, d), batch, seq)
        yb = _fox(proj, gcol, batch, seq)
        x2 = _merge(ya, yb, proj, x2, pool_w[l].astype(jnp.bfloat16), pool_scale[l].reshape(1, d),
                    w_branch[l].astype(jnp.bfloat16), w_out[l].astype(jnp.bfloat16),
                    final_g.reshape(1, d), seq, final=(l == depth - 1))
    return x2.reshape(batch, seq, d)
```

```python
import functools

import jax
import jax.numpy as jnp
from jax import lax
from jax.experimental import pallas as pl
from jax.experimental.pallas import tpu as pltpu

D_MODEL = 1024
ML_HEADS = 4
ML_HEAD_DIM = 256
CONV_WIDTH = 4
FOX_HEADS = 8
FOX_HEAD_DIM = 128
POOL_GROUPS = 4
POOL_GROUP_DIM = 256
POOL_WINDOWS = (2, 4, 8, 16)
N_BRANCH = 3
EPS = 1e-6

LANES = 128
HALO = 16
CONV_HALO = 8
NEG = -1e30
SKIP_MARGIN = 32.0

C_GATES = 0
C_AQ, C_AK, C_AV, C_AO, C_AZ, C_BQ, C_BK, C_BV, C_BZ, C_CU, C_CZ = range(N_BRANCH, N_BRANCH + 11)
N_MAIN = 14 * D_MODEL
G_AI, G_AF, G_BF = 0, 4, 8

ML_CHUNK = 256
IN_TM, IN_TN = 1024, 1024
FOX_TQ = 512
FOX_TK = 512
LOG2E = 1.4426950408889634
MERGE_TM = 512
VMEM_LIMIT = 56 * 1024 * 1024


def _sigmoid(x):
    return 1.0 / (1.0 + jnp.exp(-x))


def _silu(x):
    return x * _sigmoid(x)


def _in_proj_kernel(x_ref, g_ref, w_ref, wg_ref, proj_ref, gate_ref, h_scr):
    @pl.when(pl.program_id(1) == 0)
    def _():
        x = x_ref[...]
        r = lax.rsqrt(jnp.mean(x * x, axis=-1, keepdims=True) + EPS)
        h = ((x * r) * g_ref[...]).astype(jnp.bfloat16)
        h_scr[...] = h
        gate_ref[...] = jnp.dot(h, wg_ref[...], preferred_element_type=jnp.float32)

    proj_ref[...] = jnp.dot(h_scr[...], w_ref[...],
                            preferred_element_type=jnp.float32).astype(jnp.bfloat16)


def _in_proj(x2, norm_g, w_main, w_gate, layer):
    m = x2.shape[0]
    return pl.pallas_call(
        _in_proj_kernel,
        grid=(m // IN_TM, N_MAIN // IN_TN),
        in_specs=[
            pl.BlockSpec((IN_TM, D_MODEL), lambda i, j: (i, 0)),
            pl.BlockSpec((1, D_MODEL), lambda i, j: (0, 0)),
            pl.BlockSpec((None, D_MODEL, IN_TN), lambda i, j: (layer, 0, j)),
            pl.BlockSpec((None, D_MODEL, LANES), lambda i, j: (layer, 0, 0)),
        ],
        out_specs=[
            pl.BlockSpec((IN_TM, IN_TN), lambda i, j: (i, j)),
            pl.BlockSpec((IN_TM, LANES), lambda i, j: (i, 0)),
        ],
        out_shape=[
            jax.ShapeDtypeStruct((m, N_MAIN), jnp.bfloat16),
            jax.ShapeDtypeStruct((m, LANES), jnp.float32),
        ],
        scratch_shapes=[pltpu.VMEM((IN_TM, D_MODEL), jnp.bfloat16)],
        compiler_params=pltpu.CompilerParams(
            dimension_semantics=("arbitrary", "arbitrary"), vmem_limit_bytes=VMEM_LIMIT),
        name="in_proj",
    )(x2, norm_g, w_main, w_gate)


def _gates_kernel(g_ref, bias_ref, col_ref, row_ref, carry_scr):
    @pl.when(pl.program_id(1) == 0)
    def _():
        carry_scr[...] = jnp.zeros_like(carry_scr)

    pre = g_ref[...] + bias_ref[...]
    ls = jnp.minimum(pre, 0.0) - jnp.log1p(jnp.exp(-jnp.abs(pre)))
    row = lax.broadcasted_iota(jnp.int32, ls.shape, 0)
    lane = lax.broadcasted_iota(jnp.int32, ls.shape, 1)
    cs = ls
    k = 1
    while k < ML_CHUNK:
        cs = cs + jnp.where(row >= k, pltpu.roll(cs, k, 0), 0.0)
        k *= 2
    run = cs + carry_scr[...]
    carry_scr[...] = run[ML_CHUNK - 1:ML_CHUNK, :]
    out = jnp.where(lane < G_AF, pre, jnp.where(lane < G_BF, cs, run))
    col_ref[...] = out
    row_ref[0] = out.T[0:16, :]


def _gates(gate_pre, gate_bias, batch, seq):
    m = gate_pre.shape[0]
    nc = seq // ML_CHUNK
    return pl.pallas_call(
        _gates_kernel,
        grid=(batch, nc),
        in_specs=[
            pl.BlockSpec((ML_CHUNK, LANES), lambda b, c: (b * nc + c, 0)),
            pl.BlockSpec((1, LANES), lambda b, c: (0, 0)),
        ],
        out_specs=[
            pl.BlockSpec((ML_CHUNK, LANES), lambda b, c: (b * nc + c, 0)),
            pl.BlockSpec((1, 16, ML_CHUNK), lambda b, c: (b, 0, c)),
        ],
        out_shape=[
            jax.ShapeDtypeStruct((m, LANES), jnp.float32),
            jax.ShapeDtypeStruct((batch, 16, seq), jnp.float32),
        ],
        scratch_shapes=[pltpu.VMEM((1, LANES), jnp.float32)],
        compiler_params=pltpu.CompilerParams(dimension_semantics=("arbitrary", "arbitrary")),
        name="gates",
    )(gate_pre, gate_bias)


def _mlstm_kernel(aq_ref, ak_ref, av_ref, ao_ref, az_ref, cw_ref, gc_ref, gr_ref, ng_ref,
                  y_ref, qbuf, kbuf, ct_scr, m_scr):
    L = ML_CHUNK
    dh = ML_HEAD_DIM

    @pl.when(pl.program_id(1) == 0)
    def _():
        qbuf[0:CONV_HALO, :] = jnp.zeros((CONV_HALO, D_MODEL), jnp.float32)
        kbuf[0:CONV_HALO, :] = jnp.zeros((CONV_HALO, D_MODEL), jnp.float32)
        ct_scr[...] = jnp.zeros_like(ct_scr)
        m_scr[...] = jnp.zeros_like(m_scr)

    qbuf[CONV_HALO:, :] = aq_ref[...].astype(jnp.float32)
    kbuf[CONV_HALO:, :] = ak_ref[...].astype(jnp.float32)

    def conv(buf, w):
        base = CONV_HALO - (CONV_WIDTH - 1)
        acc = buf[base:base + L, :] * w[0:1, :]
        for kk in range(1, CONV_WIDTH):
            acc = acc + buf[base + kk:base + kk + L, :] * w[kk:kk + 1, :]
        return acc

    cw = cw_ref[...]
    qc = (_silu(conv(qbuf, cw[:, :D_MODEL])) * (dh ** -0.5)).astype(jnp.bfloat16)
    kc = _silu(conv(kbuf, cw[:, D_MODEL:]))
    qbuf[0:CONV_HALO, :] = qbuf[L:L + CONV_HALO, :]
    kbuf[0:CONV_HALO, :] = kbuf[L:L + CONV_HALO, :]

    gcol = gc_ref[...]
    grow = gr_ref[0]
    t_idx = lax.broadcasted_iota(jnp.int32, (L, L), 0)
    s_idx = lax.broadcasted_iota(jnp.int32, (L, L), 1)
    causal = s_idx <= t_idx
    ones_col = (lax.broadcasted_iota(jnp.int32, (L, LANES), 1) == 0).astype(jnp.bfloat16)

    for h in range(ML_HEADS):
        sl = slice(h * dh, (h + 1) * dh)
        q = qc[:, sl]
        kf = kc[:, sl]
        kb = kf.astype(jnp.bfloat16)
        v_aug = jnp.concatenate([av_ref[:, sl], ones_col], axis=1)
        i_col = gcol[:, G_AI + h:G_AI + h + 1]
        b_col = gcol[:, G_AF + h:G_AF + h + 1]
        i_row = grow[G_AI + h:G_AI + h + 1, :]
        b_row = grow[G_AF + h:G_AF + h + 1, :]
        m_prev = m_scr[h:h + 1, 0:1]

        d = jnp.where(causal, b_col + (i_row - b_row), NEG)
        inter = b_col + m_prev
        m_t = jnp.maximum(inter, jnp.max(d, axis=-1, keepdims=True))
        w_inter = jnp.exp(inter - m_t)
        e = jnp.exp(d - m_t)
        s = lax.dot_general(q, kb, (((1,), (1,)), ((), ())), preferred_element_type=jnp.float32)
        p = (e * s).astype(jnp.bfloat16)
        ct = ct_scr[h]
        num_aug = (w_inter * jnp.dot(q, ct.astype(jnp.bfloat16), preferred_element_type=jnp.float32)
                   + jnp.dot(p, v_aug, preferred_element_type=jnp.float32))
        num = num_aug[:, :dh]
        den = num_aug[:, dh:dh + 1]
        hh = num / jnp.maximum(jnp.abs(den), jnp.exp(-m_t))
        hh = hh * lax.rsqrt(jnp.mean(hh * hh, axis=-1, keepdims=True) + EPS)
        y = (hh * ng_ref[:, sl] * _sigmoid(ao_ref[:, sl].astype(jnp.float32))
             * _silu(az_ref[:, sl].astype(jnp.float32)))
        y_ref[:, sl] = y.astype(jnp.bfloat16)

        b_last = b_col[L - 1:L, :]
        dec = b_last - b_col + i_col
        m_new = jnp.maximum(b_last + m_prev, jnp.max(dec, axis=0, keepdims=True))
        w_s = jnp.exp(dec - m_new)
        w_old = jnp.exp(b_last + m_prev - m_new)
        vw = (v_aug.astype(jnp.float32) * w_s).astype(jnp.bfloat16)
        upd = jnp.dot(kf.T.astype(jnp.bfloat16), vw, preferred_element_type=jnp.float32)
        ct_scr[h] = w_old * ct + upd
        m_scr[h:h + 1, :] = jnp.broadcast_to(m_new, (1, LANES))


def _mlstm(proj, conv_w, gcol, grow, ml_norm_g, batch, seq):
    m = proj.shape[0]
    L = ML_CHUNK
    nc = seq // L

    def col(c):
        return pl.BlockSpec((L, D_MODEL), lambda b, i, c=c: (b * nc + i, c))

    return pl.pallas_call(
        _mlstm_kernel,
        grid=(batch, nc),
        in_specs=[
            col(C_AQ), col(C_AK), col(C_AV), col(C_AO), col(C_AZ),
            pl.BlockSpec((CONV_WIDTH, 2 * D_MODEL), lambda b, i: (0, 0)),
            pl.BlockSpec((L, LANES), lambda b, i: (b * nc + i, 0)),
            pl.BlockSpec((1, 16, L), lambda b, i: (b, 0, i)),
            pl.BlockSpec((1, D_MODEL), lambda b, i: (0, 0)),
        ],
        out_specs=pl.BlockSpec((L, D_MODEL), lambda b, i: (b * nc + i, 0)),
        out_shape=jax.ShapeDtypeStruct((m, D_MODEL), jnp.bfloat16),
        scratch_shapes=[
            pltpu.VMEM((CONV_HALO + L, D_MODEL), jnp.float32),
            pltpu.VMEM((CONV_HALO + L, D_MODEL), jnp.float32),
            pltpu.VMEM((ML_HEADS, ML_HEAD_DIM, ML_HEAD_DIM + LANES), jnp.float32),
            pltpu.VMEM((8, LANES), jnp.float32),
        ],
        compiler_params=pltpu.CompilerParams(
            dimension_semantics=("arbitrary", "arbitrary"), vmem_limit_bytes=VMEM_LIMIT),
        name="mlstm",
    )(proj, proj, proj, proj, proj, conv_w, gcol, grow, ml_norm_g)


def _fox_kernel(q_ref, k_ref, v_ref, z_ref, g_ref, y_ref, qa_scr, ka_scr, st_scr, *, seq):
    tq, tk = FOX_TQ, FOX_TK
    h = pl.program_id(1)
    i = pl.program_id(2)
    lane1 = lax.broadcasted_iota(jnp.int32, (1, LANES), 1)

    @pl.when(i == 0)
    def _():
        lane = lax.broadcasted_iota(jnp.int32, (tk, LANES), 1)
        ones_q = ((lane >= 3) & (lane < 6)).astype(jnp.float32)
        ones_k = (lane < 3).astype(jnp.float32)

        def build(c, carry):
            kmax, qn_vec, g_vec, fe_vec = carry
            r0 = pl.multiple_of(c * tk, tk)
            g = g_ref[pl.ds(r0, tk), :]
            f = jnp.sum(jnp.where(lane == G_BF + h, g, 0.0), axis=1, keepdims=True) * LOG2E
            hi = f.astype(jnp.bfloat16).astype(jnp.float32)
            r1 = f - hi
            mid = r1.astype(jnp.bfloat16).astype(jnp.float32)
            lo = r1 - mid
            qx = jnp.where(lane == 0, hi, jnp.where(lane == 1, mid, jnp.where(lane == 2, lo, ones_q)))
            kx = jnp.where(lane == 3, -hi, jnp.where(lane == 4, -mid, jnp.where(lane == 5, -lo, ones_k)))
            qs = (q_ref[pl.ds(r0, tk), :].astype(jnp.float32)
                  * (FOX_HEAD_DIM ** -0.5 * LOG2E)).astype(jnp.bfloat16)
            kk = k_ref[pl.ds(r0, tk), :]
            qa_scr[pl.ds(r0, tk), :] = jnp.concatenate([qs, qx.astype(jnp.bfloat16)], axis=1)
            ka_scr[pl.ds(r0, tk), :] = jnp.concatenate([kk, kx.astype(jnp.bfloat16)], axis=1)

            qf = qs.astype(jnp.float32)
            kf = kk.astype(jnp.float32)
            qn = jnp.sqrt(jnp.sum(qf * qf, axis=1, keepdims=True))
            kn = jnp.sqrt(jnp.sum(kf * kf, axis=1, keepdims=True))
            diag = jnp.sum(qf * kf, axis=1, keepdims=True)
            here = lane1 == c
            kmax = jnp.maximum(kmax, jnp.max(kn, axis=0, keepdims=True))
            qn_vec = jnp.where(here, jnp.max(qn, axis=0, keepdims=True), qn_vec)
            g_vec = jnp.where(here, jnp.max(f - diag, axis=0, keepdims=True), g_vec)
            fe_vec = jnp.where(here, f[tk - 1:tk, :], fe_vec)
            return kmax, qn_vec, g_vec, fe_vec

        zrow = jnp.zeros((1, LANES), jnp.float32)
        kmax, qn_vec, g_vec, fe_vec = lax.fori_loop(
            0, seq // tk, build, (jnp.zeros((1, 1), jnp.float32), zrow, zrow, zrow))
        st_scr[0:1, :] = qn_vec * kmax + g_vec
        st_scr[1:2, :] = fe_vec

    bound = jnp.sum(jnp.where(lane1 == i, st_scr[0:1, :], 0.0), axis=1, keepdims=True)
    needed = (lane1 < i) & jnp.logical_not(st_scr[1:2, :] >= bound + SKIP_MARGIN * LOG2E)
    lo = jnp.min(jnp.where(needed, lane1, i))

    q0 = pl.multiple_of(i * tq, tq)
    qa = qa_scr[pl.ds(q0, tq), :]

    def scores(j):
        k0 = pl.multiple_of(j * tk, tk)
        return lax.dot_general(qa, ka_scr[pl.ds(k0, tk), :], (((1,), (1,)), ((), ())),
                               preferred_element_type=jnp.float32)

    def attend(carry, s, j):
        m, l, acc = carry
        k0 = pl.multiple_of(j * tk, tk)
        m_new = jnp.maximum(m, jnp.max(s, axis=-1, keepdims=True))
        alpha = jnp.exp2(m - m_new)
        p = jnp.exp2(s - m_new)
        l = alpha * l + jnp.sum(p, axis=-1, keepdims=True)
        acc = alpha * acc + jnp.dot(p.astype(jnp.bfloat16), v_ref[pl.ds(k0, tk), :],
                                    preferred_element_type=jnp.float32)
        return m_new, l, acc

    def body(j, carry):
        return attend(carry, scores(j), j)

    init = (jnp.full((tq, 1), NEG, jnp.float32), jnp.zeros((tq, 1), jnp.float32),
            jnp.zeros((tq, FOX_HEAD_DIM), jnp.float32))
    carry = lax.fori_loop(lo, i, body, init)
    r_idx = lax.broadcasted_iota(jnp.int32, (tq, tk), 0)
    c_idx = lax.broadcasted_iota(jnp.int32, (tq, tk), 1)
    _, l, acc = attend(carry, jnp.where(c_idx <= r_idx, scores(i), NEG), i)
    y_ref[...] = (acc / l * _silu(z_ref[...].astype(jnp.float32))).astype(jnp.bfloat16)


def _fox(proj, gcol, batch, seq):
    m = proj.shape[0]
    nq = seq // FOX_TQ
    assert FOX_TQ == FOX_TK and nq <= LANES
    hb = D_MODEL // FOX_HEAD_DIM

    def head(c):
        return pl.BlockSpec((seq, FOX_HEAD_DIM), lambda b, h, i, c=c: (b, c * hb + h))

    return pl.pallas_call(
        functools.partial(_fox_kernel, seq=seq),
        grid=(batch, FOX_HEADS, nq),
        in_specs=[
            head(C_BQ), head(C_BK), head(C_BV),
            pl.BlockSpec((FOX_TQ, FOX_HEAD_DIM), lambda b, h, i: (b * nq + i, C_BZ * hb + h)),
            pl.BlockSpec((seq, LANES), lambda b, h, i: (b, 0)),
        ],
        out_specs=pl.BlockSpec((FOX_TQ, FOX_HEAD_DIM), lambda b, h, i: (b * nq + i, h)),
        out_shape=jax.ShapeDtypeStruct((m, D_MODEL), jnp.bfloat16),
        scratch_shapes=[
            pltpu.VMEM((seq, 2 * FOX_HEAD_DIM), jnp.bfloat16),
            pltpu.VMEM((seq, 2 * FOX_HEAD_DIM), jnp.bfloat16),
            pltpu.VMEM((8, LANES), jnp.float32),
        ],
        compiler_params=pltpu.CompilerParams(
            dimension_semantics=("arbitrary", "arbitrary", "arbitrary"), vmem_limit_bytes=VMEM_LIMIT),
        name="fox",
    )(proj, proj, proj, proj, gcol)


def _merge_kernel(ya_ref, yb_ref, cu_ref, halo_ref, cz_ref, gates_ref, x_ref, pw_ref, ps_ref,
                  wb_ref, wo_ref, fg_ref, o_ref, *, seq, final):
    tm = MERGE_TM
    gd = POOL_GROUP_DIM
    i = pl.program_id(0)
    t0 = (i * tm) % seq
    pos = t0 + lax.broadcasted_iota(jnp.int32, (tm, 1), 0)
    halo = jnp.where(t0 == 0, 0.0, halo_ref[...].astype(jnp.float32))
    u = cu_ref[...].astype(jnp.float32)
    ext = jnp.concatenate([halo, u], axis=0)

    ys = []
    for g, win in enumerate(POOL_WINDOWS):
        sl = slice(g * gd, (g + 1) * gd)
        acc = ext[:, sl]
        step = 1
        while step < win:
            acc = acc + pltpu.roll(acc, step, 0)
            step *= 2
        cnt = jnp.minimum(pos + 1, win).astype(jnp.float32)
        d = acc[HALO:, :] / cnt - u[:, sl]
        ys.append(jnp.dot(d.astype(jnp.bfloat16), pw_ref[g], preferred_element_type=jnp.float32))
    yc = jnp.concatenate(ys, axis=1) * ps_ref[...]
    yc = yc * _silu(cz_ref[...].astype(jnp.float32))

    merged = jnp.zeros((tm, D_MODEL), jnp.float32)
    for n, y in enumerate((ya_ref[...], yb_ref[...], yc.astype(jnp.bfloat16))):
        yb = jnp.dot(y, wb_ref[n], preferred_element_type=jnp.float32)
        gate = _sigmoid(gates_ref[:, n * D_MODEL:(n + 1) * D_MODEL].astype(jnp.float32))
        merged = merged + gate * yb
    out = x_ref[...] + jnp.dot(merged.astype(jnp.bfloat16), wo_ref[...],
                               preferred_element_type=jnp.float32)
    if final:
        r = lax.rsqrt(jnp.mean(out * out, axis=-1, keepdims=True) + EPS)
        out = (out * r) * fg_ref[...]
    o_ref[...] = out


def _merge(ya, yb, proj, x2, pool_w, pool_scale, w_branch, w_out, final_g, seq, final, layer):
    m = x2.shape[0]
    tm = MERGE_TM
    hpt = tm // HALO

    def rowblk(c, width=1):
        return pl.BlockSpec((tm, width * D_MODEL), lambda i, c=c: (i, c))

    const2 = lambda i: (0, 0)
    layer3 = lambda i: (layer, 0, 0)
    layer4 = lambda i: (layer, 0, 0, 0)
    return pl.pallas_call(
        functools.partial(_merge_kernel, seq=seq, final=final),
        grid=(m // tm,),
        in_specs=[
            rowblk(0), rowblk(0), rowblk(C_CU),
            pl.BlockSpec((HALO, D_MODEL), lambda i: (jnp.maximum(i * hpt - 1, 0), C_CU)),
            rowblk(C_CZ),
            pl.BlockSpec((tm, N_BRANCH * D_MODEL), lambda i: (i, C_GATES // N_BRANCH)),
            rowblk(0),
            pl.BlockSpec((None, POOL_GROUPS, POOL_GROUP_DIM, POOL_GROUP_DIM), layer4),
            pl.BlockSpec((1, D_MODEL), const2),
            pl.BlockSpec((None, N_BRANCH, D_MODEL, D_MODEL), layer4),
            pl.BlockSpec((None, D_MODEL, D_MODEL), layer3),
            pl.BlockSpec((1, D_MODEL), const2),
        ],
        out_specs=pl.BlockSpec((tm, D_MODEL), lambda i: (i, 0)),
        out_shape=jax.ShapeDtypeStruct((m, D_MODEL), jnp.float32),
        compiler_params=pltpu.CompilerParams(
            dimension_semantics=("arbitrary",), vmem_limit_bytes=VMEM_LIMIT),
        name="merge",
    )(ya, yb, proj, proj, proj, proj, x2, pool_w, pool_scale, w_branch, w_out, final_g)


def _split_w_in(w_in):
    o_ai = 5 * D_MODEL
    o_bq = o_ai + 2 * ML_HEADS
    o_bf = o_bq + 4 * D_MODEL
    o_cu = o_bf + FOX_HEADS
    o_g = o_cu + 2 * D_MODEL
    w = w_in.astype(jnp.bfloat16)
    w_main = jnp.concatenate([w[..., o_g:], w[..., :o_ai], w[..., o_bq:o_bf], w[..., o_cu:o_g]], axis=-1)
    w_gate = jnp.concatenate([w[..., o_ai:o_bq], w[..., o_bf:o_cu]], axis=-1)
    w_gate = jnp.pad(w_gate, ((0, 0), (0, 0), (0, LANES - w_gate.shape[-1])))
    return w_main, w_gate


def kernel(x, norm_g, w_in, conv_w, ml_bi, ml_bf, ml_norm_g, fox_bf, pool_w, pool_scale, w_branch,
           w_out, final_g):
    batch, seq, d = x.shape
    depth = norm_g.shape[0]
    assert d == D_MODEL and seq % max(IN_TM, FOX_TQ, MERGE_TM, ML_CHUNK) == 0
    x2 = x.reshape(batch * seq, d)
    w_main, w_gate = _split_w_in(w_in)
    pool_wb = pool_w.astype(jnp.bfloat16)
    w_branchb = w_branch.astype(jnp.bfloat16)
    w_outb = w_out.astype(jnp.bfloat16)
    for l in range(depth):
        gate_bias = jnp.pad(jnp.concatenate([ml_bi[l], ml_bf[l], fox_bf[l]]),
                            (0, LANES - 2 * ML_HEADS - FOX_HEADS)).reshape(1, LANES)
        proj, gate_pre = _in_proj(x2, norm_g[l].reshape(1, d), w_main, w_gate, l)
        gcol, grow = _gates(gate_pre, gate_bias, batch, seq)
        ya = _mlstm(proj, conv_w[l], gcol, grow, ml_norm_g[l].reshape(1, d), batch, seq)
        yb = _fox(proj, gcol, batch, seq)
        x2 = _merge(ya, yb, proj, x2, pool_wb, pool_scale[l].reshape(1, d), w_branchb, w_outb,
                    final_g.reshape(1, d), seq, final=(l == depth - 1), layer=l)
    return x2.reshape(batch, seq, d)
```

```python
import functools

import jax
import jax.numpy as jnp
from jax import lax
from jax.experimental import pallas as pl
from jax.experimental.pallas import tpu as pltpu

D_MODEL = 1024
ML_HEADS = 4
ML_HEAD_DIM = 256
CONV_WIDTH = 4
FOX_HEADS = 8
FOX_HEAD_DIM = 128
POOL_GROUPS = 4
POOL_GROUP_DIM = 256
POOL_WINDOWS = (2, 4, 8, 16)
N_BRANCH = 3
EPS = 1e-6

LANES = 128
HALO = 16
CONV_HALO = 8
NEG = -1e30
SKIP_MARGIN = 32.0

C_GATES = 0
C_AQ, C_AK, C_AV, C_AO, C_AZ, C_BQ, C_BK, C_BV, C_BZ, C_CU, C_CZ = range(N_BRANCH, N_BRANCH + 11)
N_MAIN = 14 * D_MODEL
G_AI, G_AF, G_BF = 0, 4, 8

ML_CHUNK = 256
IN_TM, IN_TN = 1024, 1024
FOX_TQ = 512
FOX_TK = 512
FOX_PAIR = 2
LOG2E = 1.4426950408889634
MERGE_TM = 512
VMEM_LIMIT = 56 * 1024 * 1024


def _sigmoid(x):
    return 1.0 / (1.0 + jnp.exp(-x))


def _silu(x):
    return x * _sigmoid(x)


def _in_proj_kernel(x_ref, g_ref, w_ref, wg_ref, proj_ref, gate_ref, h_scr):
    @pl.when(pl.program_id(1) == 0)
    def _():
        x = x_ref[...]
        r = lax.rsqrt(jnp.mean(x * x, axis=-1, keepdims=True) + EPS)
        h = ((x * r) * g_ref[...]).astype(jnp.bfloat16)
        h_scr[...] = h
        gate_ref[...] = jnp.dot(h, wg_ref[...], preferred_element_type=jnp.float32)

    proj_ref[...] = jnp.dot(h_scr[...], w_ref[...],
                            preferred_element_type=jnp.float32).astype(jnp.bfloat16)


def _in_proj(x2, norm_g, w_main, w_gate, layer):
    m = x2.shape[0]
    return pl.pallas_call(
        _in_proj_kernel,
        grid=(m // IN_TM, N_MAIN // IN_TN),
        in_specs=[
            pl.BlockSpec((IN_TM, D_MODEL), lambda i, j: (i, 0)),
            pl.BlockSpec((1, D_MODEL), lambda i, j: (0, 0)),
            pl.BlockSpec((None, D_MODEL, IN_TN), lambda i, j: (layer, 0, j)),
            pl.BlockSpec((None, D_MODEL, LANES), lambda i, j: (layer, 0, 0)),
        ],
        out_specs=[
            pl.BlockSpec((IN_TM, IN_TN), lambda i, j: (i, j)),
            pl.BlockSpec((IN_TM, LANES), lambda i, j: (i, 0)),
        ],
        out_shape=[
            jax.ShapeDtypeStruct((m, N_MAIN), jnp.bfloat16),
            jax.ShapeDtypeStruct((m, LANES), jnp.float32),
        ],
        scratch_shapes=[pltpu.VMEM((IN_TM, D_MODEL), jnp.bfloat16)],
        compiler_params=pltpu.CompilerParams(
            dimension_semantics=("arbitrary", "arbitrary"), vmem_limit_bytes=VMEM_LIMIT),
        name="in_proj",
    )(x2, norm_g, w_main, w_gate)


def _gates_kernel(g_ref, bias_ref, col_ref, row_ref, carry_scr):
    @pl.when(pl.program_id(1) == 0)
    def _():
        carry_scr[...] = jnp.zeros_like(carry_scr)

    pre = g_ref[...] + bias_ref[...]
    ls = jnp.minimum(pre, 0.0) - jnp.log1p(jnp.exp(-jnp.abs(pre)))
    row = lax.broadcasted_iota(jnp.int32, ls.shape, 0)
    lane = lax.broadcasted_iota(jnp.int32, ls.shape, 1)
    cs = ls
    k = 1
    while k < ML_CHUNK:
        cs = cs + jnp.where(row >= k, pltpu.roll(cs, k, 0), 0.0)
        k *= 2
    run = cs + carry_scr[...]
    carry_scr[...] = run[ML_CHUNK - 1:ML_CHUNK, :]
    out = jnp.where(lane < G_AF, pre, jnp.where(lane < G_BF, cs, run))
    col_ref[...] = out
    row_ref[0] = out.T[0:16, :]


def _gates(gate_pre, gate_bias, batch, seq):
    m = gate_pre.shape[0]
    nc = seq // ML_CHUNK
    return pl.pallas_call(
        _gates_kernel,
        grid=(batch, nc),
        in_specs=[
            pl.BlockSpec((ML_CHUNK, LANES), lambda b, c: (b * nc + c, 0)),
            pl.BlockSpec((1, LANES), lambda b, c: (0, 0)),
        ],
        out_specs=[
            pl.BlockSpec((ML_CHUNK, LANES), lambda b, c: (b * nc + c, 0)),
            pl.BlockSpec((1, 16, ML_CHUNK), lambda b, c: (b, 0, c)),
        ],
        out_shape=[
            jax.ShapeDtypeStruct((m, LANES), jnp.float32),
            jax.ShapeDtypeStruct((batch, 16, seq), jnp.float32),
        ],
        scratch_shapes=[pltpu.VMEM((1, LANES), jnp.float32)],
        compiler_params=pltpu.CompilerParams(dimension_semantics=("arbitrary", "arbitrary")),
        name="gates",
    )(gate_pre, gate_bias)


def _mlstm_kernel(aq_ref, ak_ref, av_ref, ao_ref, az_ref, cw_ref, gc_ref, gr_ref, ng_ref,
                  y_ref, qbuf, kbuf, ct_scr, m_scr):
    L = ML_CHUNK
    dh = ML_HEAD_DIM

    @pl.when(pl.program_id(1) == 0)
    def _():
        qbuf[0:CONV_HALO, :] = jnp.zeros((CONV_HALO, D_MODEL), jnp.float32)
        kbuf[0:CONV_HALO, :] = jnp.zeros((CONV_HALO, D_MODEL), jnp.float32)
        ct_scr[...] = jnp.zeros_like(ct_scr)
        m_scr[...] = jnp.zeros_like(m_scr)

    qbuf[CONV_HALO:, :] = aq_ref[...].astype(jnp.float32)
    kbuf[CONV_HALO:, :] = ak_ref[...].astype(jnp.float32)

    def conv(buf, w):
        base = CONV_HALO - (CONV_WIDTH - 1)
        acc = buf[base:base + L, :] * w[0:1, :]
        for kk in range(1, CONV_WIDTH):
            acc = acc + buf[base + kk:base + kk + L, :] * w[kk:kk + 1, :]
        return acc

    cw = cw_ref[...]
    qc = (_silu(conv(qbuf, cw[:, :D_MODEL])) * (dh ** -0.5)).astype(jnp.bfloat16)
    kc = _silu(conv(kbuf, cw[:, D_MODEL:]))
    qbuf[0:CONV_HALO, :] = qbuf[L:L + CONV_HALO, :]
    kbuf[0:CONV_HALO, :] = kbuf[L:L + CONV_HALO, :]

    gcol = gc_ref[...]
    grow = gr_ref[0]
    t_idx = lax.broadcasted_iota(jnp.int32, (L, L), 0)
    s_idx = lax.broadcasted_iota(jnp.int32, (L, L), 1)
    causal = s_idx <= t_idx
    ones_col = (lax.broadcasted_iota(jnp.int32, (L, LANES), 1) == 0).astype(jnp.bfloat16)

    for h in range(ML_HEADS):
        sl = slice(h * dh, (h + 1) * dh)
        q = qc[:, sl]
        kf = kc[:, sl]
        kb = kf.astype(jnp.bfloat16)
        v_aug = jnp.concatenate([av_ref[:, sl], ones_col], axis=1)
        i_col = gcol[:, G_AI + h:G_AI + h + 1]
        b_col = gcol[:, G_AF + h:G_AF + h + 1]
        i_row = grow[G_AI + h:G_AI + h + 1, :]
        b_row = grow[G_AF + h:G_AF + h + 1, :]
        m_prev = m_scr[h:h + 1, 0:1]

        d = jnp.where(causal, b_col + (i_row - b_row), NEG)
        inter = b_col + m_prev
        m_t = jnp.maximum(inter, jnp.max(d, axis=-1, keepdims=True))
        w_inter = jnp.exp(inter - m_t)
        e = jnp.exp(d - m_t)
        s = lax.dot_general(q, kb, (((1,), (1,)), ((), ())), preferred_element_type=jnp.float32)
        p = (e * s).astype(jnp.bfloat16)
        ct = ct_scr[h]
        num_aug = (w_inter * jnp.dot(q, ct.astype(jnp.bfloat16), preferred_element_type=jnp.float32)
                   + jnp.dot(p, v_aug, preferred_element_type=jnp.float32))
        num = num_aug[:, :dh]
        den = num_aug[:, dh:dh + 1]
        hh = num / jnp.maximum(jnp.abs(den), jnp.exp(-m_t))
        hh = hh * lax.rsqrt(jnp.mean(hh * hh, axis=-1, keepdims=True) + EPS)
        y = (hh * ng_ref[:, sl] * _sigmoid(ao_ref[:, sl].astype(jnp.float32))
             * _silu(az_ref[:, sl].astype(jnp.float32)))
        y_ref[:, sl] = y.astype(jnp.bfloat16)

        b_last = b_col[L - 1:L, :]
        dec = b_last - b_col + i_col
        m_new = jnp.maximum(b_last + m_prev, jnp.max(dec, axis=0, keepdims=True))
        w_s = jnp.exp(dec - m_new)
        w_old = jnp.exp(b_last + m_prev - m_new)
        vw = (v_aug.astype(jnp.float32) * w_s).astype(jnp.bfloat16)
        upd = jnp.dot(kf.T.astype(jnp.bfloat16), vw, preferred_element_type=jnp.float32)
        ct_scr[h] = w_old * ct + upd
        m_scr[h:h + 1, :] = jnp.broadcast_to(m_new, (1, LANES))


def _mlstm(proj, conv_w, gcol, grow, ml_norm_g, batch, seq):
    m = proj.shape[0]
    L = ML_CHUNK
    nc = seq // L

    def col(c):
        return pl.BlockSpec((L, D_MODEL), lambda b, i, c=c: (b * nc + i, c))

    return pl.pallas_call(
        _mlstm_kernel,
        grid=(batch, nc),
        in_specs=[
            col(C_AQ), col(C_AK), col(C_AV), col(C_AO), col(C_AZ),
            pl.BlockSpec((CONV_WIDTH, 2 * D_MODEL), lambda b, i: (0, 0)),
            pl.BlockSpec((L, LANES), lambda b, i: (b * nc + i, 0)),
            pl.BlockSpec((1, 16, L), lambda b, i: (b, 0, i)),
            pl.BlockSpec((1, D_MODEL), lambda b, i: (0, 0)),
        ],
        out_specs=pl.BlockSpec((L, D_MODEL), lambda b, i: (b * nc + i, 0)),
        out_shape=jax.ShapeDtypeStruct((m, D_MODEL), jnp.bfloat16),
        scratch_shapes=[
            pltpu.VMEM((CONV_HALO + L, D_MODEL), jnp.float32),
            pltpu.VMEM((CONV_HALO + L, D_MODEL), jnp.float32),
            pltpu.VMEM((ML_HEADS, ML_HEAD_DIM, ML_HEAD_DIM + LANES), jnp.float32),
            pltpu.VMEM((8, LANES), jnp.float32),
        ],
        compiler_params=pltpu.CompilerParams(
            dimension_semantics=("arbitrary", "arbitrary"), vmem_limit_bytes=VMEM_LIMIT),
        name="mlstm",
    )(proj, proj, proj, proj, proj, conv_w, gcol, grow, ml_norm_g)


def _fox_kernel(q_ref, k_ref, v_ref, z_ref, g_ref, y_ref, qa_scr, ka_scr, st_scr, *, seq):
    tq, tk = FOX_TQ, FOX_TK
    h = pl.program_id(1)
    i = pl.program_id(2)
    lane1 = lax.broadcasted_iota(jnp.int32, (1, LANES), 1)

    @pl.when(i == 0)
    def _():
        lane = lax.broadcasted_iota(jnp.int32, (tk, LANES), 1)
        ones_q = ((lane >= 3) & (lane < 6)).astype(jnp.float32)
        ones_k = (lane < 3).astype(jnp.float32)

        def build(c, carry):
            kmax, qn_vec, g_vec, fe_vec = carry
            r0 = pl.multiple_of(c * tk, tk)
            g = g_ref[pl.ds(r0, tk), :]
            f = jnp.sum(jnp.where(lane == G_BF + h, g, 0.0), axis=1, keepdims=True) * LOG2E
            hi = f.astype(jnp.bfloat16).astype(jnp.float32)
            r1 = f - hi
            mid = r1.astype(jnp.bfloat16).astype(jnp.float32)
            lo = r1 - mid
            qx = jnp.where(lane == 0, hi, jnp.where(lane == 1, mid, jnp.where(lane == 2, lo, ones_q)))
            kx = jnp.where(lane == 3, -hi, jnp.where(lane == 4, -mid, jnp.where(lane == 5, -lo, ones_k)))
            qs = (q_ref[pl.ds(r0, tk), :].astype(jnp.float32)
                  * (FOX_HEAD_DIM ** -0.5 * LOG2E)).astype(jnp.bfloat16)
            kk = k_ref[pl.ds(r0, tk), :]
            qa_scr[pl.ds(r0, tk), :] = jnp.concatenate([qs, qx.astype(jnp.bfloat16)], axis=1)
            ka_scr[pl.ds(r0, tk), :] = jnp.concatenate([kk, kx.astype(jnp.bfloat16)], axis=1)

            qf = qs.astype(jnp.float32)
            kf = kk.astype(jnp.float32)
            qn = jnp.sqrt(jnp.sum(qf * qf, axis=1, keepdims=True))
            kn = jnp.sqrt(jnp.sum(kf * kf, axis=1, keepdims=True))
            diag = jnp.sum(qf * kf, axis=1, keepdims=True)
            here = lane1 == c
            kmax = jnp.maximum(kmax, jnp.max(kn, axis=0, keepdims=True))
            qn_vec = jnp.where(here, jnp.max(qn, axis=0, keepdims=True), qn_vec)
            g_vec = jnp.where(here, jnp.max(f - diag, axis=0, keepdims=True), g_vec)
            fe_vec = jnp.where(here, f[tk - 1:tk, :], fe_vec)
            return kmax, qn_vec, g_vec, fe_vec

        zrow = jnp.zeros((1, LANES), jnp.float32)
        kmax, qn_vec, g_vec, fe_vec = lax.fori_loop(
            0, seq // tk, build, (jnp.zeros((1, 1), jnp.float32), zrow, zrow, zrow))
        st_scr[0:1, :] = qn_vec * kmax + g_vec
        st_scr[1:2, :] = fe_vec

    def back_blocks(blk):
        bound = jnp.sum(jnp.where(lane1 == blk, st_scr[0:1, :], 0.0), axis=1, keepdims=True)
        needed = (lane1 < blk) & jnp.logical_not(st_scr[1:2, :] >= bound + SKIP_MARGIN * LOG2E)
        return blk - jnp.min(jnp.where(needed, lane1, blk))

    blks = [i * FOX_PAIR + u for u in range(FOX_PAIR)]
    n_back = functools.reduce(jnp.maximum, [back_blocks(b) for b in blks])
    qas = [qa_scr[pl.ds(pl.multiple_of(b * tq, tq), tq), :] for b in blks]

    def attend(carry, t, diagonal):
        m, l, acc = carry
        k0s, parts = [], []
        for u in range(FOX_PAIR):
            k0 = pl.multiple_of(jnp.maximum(blks[u] - t, 0) * tk, tk)
            s = lax.dot_general(qas[u], ka_scr[pl.ds(k0, tk), :], (((1,), (1,)), ((), ())),
                                preferred_element_type=jnp.float32)
            if diagonal:
                r_idx = lax.broadcasted_iota(jnp.int32, (tq, tk), 0)
                c_idx = lax.broadcasted_iota(jnp.int32, (tq, tk), 1)
                s = jnp.where(c_idx <= r_idx, s, NEG)
            elif u < FOX_PAIR - 1:
                s = jnp.where(t <= blks[u], s, NEG)
            k0s.append(k0)
            parts.append(s)
        s = jnp.concatenate(parts, axis=0)
        m_new = jnp.maximum(m, jnp.max(s, axis=-1, keepdims=True))
        alpha = jnp.exp2(m - m_new)
        p = jnp.exp2(s - m_new)
        l = alpha * l + jnp.sum(p, axis=-1, keepdims=True)
        p = p.astype(jnp.bfloat16)
        pv = jnp.concatenate(
            [jnp.dot(p[u * tq:(u + 1) * tq], v_ref[pl.ds(k0s[u], tk), :], preferred_element_type=jnp.float32)
             for u in range(FOX_PAIR)], axis=0)
        return m_new, l, alpha * acc + pv

    rows = FOX_PAIR * tq
    init = (jnp.full((rows, 1), NEG, jnp.float32), jnp.zeros((rows, 1), jnp.float32),
            jnp.zeros((rows, FOX_HEAD_DIM), jnp.float32))
    carry = attend(init, 0, True)
    _, l, acc = lax.fori_loop(1, n_back + 1, lambda t, c: attend(c, t, False), carry)
    y_ref[...] = (acc / l * _silu(z_ref[...].astype(jnp.float32))).astype(jnp.bfloat16)


def _fox(proj, gcol, batch, seq):
    m = proj.shape[0]
    rows = FOX_PAIR * FOX_TQ
    nq = seq // rows
    assert FOX_TQ == FOX_TK and seq % rows == 0 and seq // FOX_TQ <= LANES
    hb = D_MODEL // FOX_HEAD_DIM

    def head(c):
        return pl.BlockSpec((seq, FOX_HEAD_DIM), lambda b, h, i, c=c: (b, c * hb + h))

    return pl.pallas_call(
        functools.partial(_fox_kernel, seq=seq),
        grid=(batch, FOX_HEADS, nq),
        in_specs=[
            head(C_BQ), head(C_BK), head(C_BV),
            pl.BlockSpec((rows, FOX_HEAD_DIM), lambda b, h, i: (b * nq + i, C_BZ * hb + h)),
            pl.BlockSpec((seq, LANES), lambda b, h, i: (b, 0)),
        ],
        out_specs=pl.BlockSpec((rows, FOX_HEAD_DIM), lambda b, h, i: (b * nq + i, h)),
        out_shape=jax.ShapeDtypeStruct((m, D_MODEL), jnp.bfloat16),
        scratch_shapes=[
            pltpu.VMEM((seq, 2 * FOX_HEAD_DIM), jnp.bfloat16),
            pltpu.VMEM((seq, 2 * FOX_HEAD_DIM), jnp.bfloat16),
            pltpu.VMEM((8, LANES), jnp.float32),
        ],
        compiler_params=pltpu.CompilerParams(
            dimension_semantics=("arbitrary", "arbitrary", "arbitrary"), vmem_limit_bytes=VMEM_LIMIT),
        name="fox",
    )(proj, proj, proj, proj, gcol)


def _merge_kernel(ya_ref, yb_ref, cu_ref, halo_ref, cz_ref, gates_ref, x_ref, pw_ref, ps_ref,
                  wb_ref, wo_ref, fg_ref, o_ref, *, seq, final):
    tm = MERGE_TM
    gd = POOL_GROUP_DIM
    i = pl.program_id(0)
    t0 = (i * tm) % seq
    pos = t0 + lax.broadcasted_iota(jnp.int32, (tm, 1), 0)
    halo = jnp.where(t0 == 0, 0.0, halo_ref[...].astype(jnp.float32))
    u = cu_ref[...].astype(jnp.float32)
    ext = jnp.concatenate([halo, u], axis=0)

    ys = []
    for g, win in enumerate(POOL_WINDOWS):
        sl = slice(g * gd, (g + 1) * gd)
        acc = ext[:, sl]
        step = 1
        while step < win:
            acc = acc + pltpu.roll(acc, step, 0)
            step *= 2
        cnt = jnp.minimum(pos + 1, win).astype(jnp.float32)
        d = acc[HALO:, :] / cnt - u[:, sl]
        ys.append(jnp.dot(d.astype(jnp.bfloat16), pw_ref[g], preferred_element_type=jnp.float32))
    yc = jnp.concatenate(ys, axis=1) * ps_ref[...]
    yc = yc * _silu(cz_ref[...].astype(jnp.float32))

    merged = jnp.zeros((tm, D_MODEL), jnp.float32)
    for n, y in enumerate((ya_ref[...], yb_ref[...], yc.astype(jnp.bfloat16))):
        yb = jnp.dot(y, wb_ref[n], preferred_element_type=jnp.float32)
        gate = _sigmoid(gates_ref[:, n * D_MODEL:(n + 1) * D_MODEL].astype(jnp.float32))
        merged = merged + gate * yb
    out = x_ref[...] + jnp.dot(merged.astype(jnp.bfloat16), wo_ref[...],
                               preferred_element_type=jnp.float32)
    if final:
        r = lax.rsqrt(jnp.mean(out * out, axis=-1, keepdims=True) + EPS)
        out = (out * r) * fg_ref[...]
    o_ref[...] = out


def _merge(ya, yb, proj, x2, pool_w, pool_scale, w_branch, w_out, final_g, seq, final, layer):
    m = x2.shape[0]
    tm = MERGE_TM
    hpt = tm // HALO

    def rowblk(c, width=1):
        return pl.BlockSpec((tm, width * D_MODEL), lambda i, c=c: (i, c))

    const2 = lambda i: (0, 0)
    layer3 = lambda i: (layer, 0, 0)
    layer4 = lambda i: (layer, 0, 0, 0)
    return pl.pallas_call(
        functools.partial(_merge_kernel, seq=seq, final=final),
        grid=(m // tm,),
        in_specs=[
            rowblk(0), rowblk(0), rowblk(C_CU),
            pl.BlockSpec((HALO, D_MODEL), lambda i: (jnp.maximum(i * hpt - 1, 0), C_CU)),
            rowblk(C_CZ),
            pl.BlockSpec((tm, N_BRANCH * D_MODEL), lambda i: (i, C_GATES // N_BRANCH)),
            rowblk(0),
            pl.BlockSpec((None, POOL_GROUPS, POOL_GROUP_DIM, POOL_GROUP_DIM), layer4),
            pl.BlockSpec((1, D_MODEL), const2),
            pl.BlockSpec((None, N_BRANCH, D_MODEL, D_MODEL), layer4),
            pl.BlockSpec((None, D_MODEL, D_MODEL), layer3),
            pl.BlockSpec((1, D_MODEL), const2),
        ],
        out_specs=pl.BlockSpec((tm, D_MODEL), lambda i: (i, 0)),
        out_shape=jax.ShapeDtypeStruct((m, D_MODEL), jnp.float32),
        compiler_params=pltpu.CompilerParams(
            dimension_semantics=("arbitrary",), vmem_limit_bytes=VMEM_LIMIT),
        name="merge",
    )(ya, yb, proj, proj, proj, proj, x2, pool_w, pool_scale, w_branch, w_out, final_g)


O_AI = 5 * D_MODEL
O_BQ = O_AI + 2 * ML_HEADS
O_BF = O_BQ + 4 * D_MODEL
O_CU = O_BF + FOX_HEADS
O_G = O_CU + 2 * D_MODEL
N_IN = O_G + N_BRANCH * D_MODEL
W_SEGMENTS = ((O_G, N_BRANCH), (0, 5), (O_BQ, 4), (O_CU, 2))
W_PREP_ROWS = 128


def _w_prep_kernel(w_ref, main_ref, gate_ref):
    dst = 0
    for src, nblk in W_SEGMENTS:
        for n in range(nblk):
            a = src + n * D_MODEL
            a0 = a // LANES * LANES
            a1 = min(-(-(a + D_MODEL) // LANES) * LANES, N_IN)
            win = w_ref[:, a0:a1]
            main_ref[:, dst:dst + D_MODEL] = win[:, a - a0:a - a0 + D_MODEL].astype(jnp.bfloat16)
            dst += D_MODEL
    assert O_AI % LANES == 0 and O_BF % LANES == 2 * ML_HEADS
    lane = lax.broadcasted_iota(jnp.int32, (W_PREP_ROWS, LANES), 1)
    ga = w_ref[:, O_AI:O_AI + LANES]
    gb = w_ref[:, O_BF - 2 * ML_HEADS:O_BF - 2 * ML_HEADS + LANES]
    gate = jnp.where(lane < 2 * ML_HEADS, ga, jnp.where(lane < 2 * ML_HEADS + FOX_HEADS, gb, 0.0))
    gate_ref[...] = gate.astype(jnp.bfloat16)


def _split_w_in(w_in):
    depth = w_in.shape[0]
    assert w_in.shape[1:] == (D_MODEL, N_IN)
    return pl.pallas_call(
        _w_prep_kernel,
        grid=(depth, D_MODEL // W_PREP_ROWS),
        in_specs=[pl.BlockSpec((None, W_PREP_ROWS, N_IN), lambda l, r: (l, r, 0))],
        out_specs=[
            pl.BlockSpec((None, W_PREP_ROWS, N_MAIN), lambda l, r: (l, r, 0)),
            pl.BlockSpec((None, W_PREP_ROWS, LANES), lambda l, r: (l, r, 0)),
        ],
        out_shape=[
            jax.ShapeDtypeStruct((depth, D_MODEL, N_MAIN), jnp.bfloat16),
            jax.ShapeDtypeStruct((depth, D_MODEL, LANES), jnp.bfloat16),
        ],
        compiler_params=pltpu.CompilerParams(
            dimension_semantics=("arbitrary", "arbitrary"), vmem_limit_bytes=VMEM_LIMIT),
        name="w_prep",
    )(w_in)


def kernel(x, norm_g, w_in, conv_w, ml_bi, ml_bf, ml_norm_g, fox_bf, pool_w, pool_scale, w_branch,
           w_out, final_g):
    batch, seq, d = x.shape
    depth = norm_g.shape[0]
    assert d == D_MODEL and seq % max(IN_TM, FOX_TQ, MERGE_TM, ML_CHUNK) == 0
    x2 = x.reshape(batch * seq, d)
    w_main, w_gate = _split_w_in(w_in)
    pool_wb = pool_w.astype(jnp.bfloat16)
    w_branchb = w_branch.astype(jnp.bfloat16)
    w_outb = w_out.astype(jnp.bfloat16)
    for l in range(depth):
        gate_bias = jnp.pad(jnp.concatenate([ml_bi[l], ml_bf[l], fox_bf[l]]),
                            (0, LANES - 2 * ML_HEADS - FOX_HEADS)).reshape(1, LANES)
        proj, gate_pre = _in_proj(x2, norm_g[l].reshape(1, d), w_main, w_gate, l)
        gcol, grow = _gates(gate_pre, gate_bias, batch, seq)
        ya = _mlstm(proj, conv_w[l], gcol, grow, ml_norm_g[l].reshape(1, d), batch, seq)
        yb = _fox(proj, gcol, batch, seq)
        x2 = _merge(ya, yb, proj, x2, pool_wb, pool_scale[l].reshape(1, d), w_branchb, w_outb,
                    final_g.reshape(1, d), seq, final=(l == depth - 1), layer=l)
    return x2.reshape(batch, seq, d)
```

```python
import functools

import jax
import jax.numpy as jnp
from jax import lax
from jax.experimental import pallas as pl
from jax.experimental.pallas import tpu as pltpu

D_MODEL = 1024
ML_HEADS = 4
ML_HEAD_DIM = 256
CONV_WIDTH = 4
FOX_HEADS = 8
FOX_HEAD_DIM = 128
POOL_GROUPS = 4
POOL_GROUP_DIM = 256
POOL_WINDOWS = (2, 4, 8, 16)
N_BRANCH = 3
EPS = 1e-6

LANES = 128
HALO = 16
CONV_HALO = 8
NEG = -1e30
SKIP_MARGIN = 32.0

C_GATES = 0
C_AQ, C_AK, C_AV, C_AO, C_AZ, C_BQ, C_BK, C_BV, C_BZ, C_CU, C_CZ = range(N_BRANCH, N_BRANCH + 11)
N_MAIN = 14 * D_MODEL
G_AI, G_AF, G_BF = 0, 4, 8

ML_CHUNK = 256
IN_TM, IN_TN = 2048, 1024
FOX_TQ = 512
FOX_TK = 512
FOX_PAIR = 2
LOG2E = 1.4426950408889634
MERGE_TM = 512
VMEM_LIMIT = 56 * 1024 * 1024


def _sigmoid(x):
    return 1.0 / (1.0 + jnp.exp(-x))


def _silu(x):
    return x * _sigmoid(x)


def _in_proj_kernel(x_ref, g_ref, w_ref, wg_ref, proj_ref, gate_ref, h_scr):
    @pl.when(pl.program_id(1) == 0)
    def _():
        x = x_ref[...]
        r = lax.rsqrt(jnp.mean(x * x, axis=-1, keepdims=True) + EPS)
        h = ((x * r) * g_ref[...]).astype(jnp.bfloat16)
        h_scr[...] = h
        gate_ref[...] = jnp.dot(h, wg_ref[...], preferred_element_type=jnp.float32)

    proj_ref[...] = jnp.dot(h_scr[...], w_ref[...],
                            preferred_element_type=jnp.float32).astype(jnp.bfloat16)


def _in_proj(x2, norm_g, w_main, w_gate, layer):
    m = x2.shape[0]
    return pl.pallas_call(
        _in_proj_kernel,
        grid=(m // IN_TM, N_MAIN // IN_TN),
        in_specs=[
            pl.BlockSpec((IN_TM, D_MODEL), lambda i, j: (i, 0)),
            pl.BlockSpec((1, D_MODEL), lambda i, j: (0, 0)),
            pl.BlockSpec((None, D_MODEL, IN_TN), lambda i, j: (layer, 0, j)),
            pl.BlockSpec((None, D_MODEL, LANES), lambda i, j: (layer, 0, 0)),
        ],
        out_specs=[
            pl.BlockSpec((IN_TM, IN_TN), lambda i, j: (i, j)),
            pl.BlockSpec((IN_TM, LANES), lambda i, j: (i, 0)),
        ],
        out_shape=[
            jax.ShapeDtypeStruct((m, N_MAIN), jnp.bfloat16),
            jax.ShapeDtypeStruct((m, LANES), jnp.float32),
        ],
        scratch_shapes=[pltpu.VMEM((IN_TM, D_MODEL), jnp.bfloat16)],
        compiler_params=pltpu.CompilerParams(
            dimension_semantics=("arbitrary", "arbitrary"), vmem_limit_bytes=VMEM_LIMIT),
        name="in_proj",
    )(x2, norm_g, w_main, w_gate)


def _gates_kernel(g_ref, bias_ref, col_ref, row_ref, carry_scr):
    @pl.when(pl.program_id(1) == 0)
    def _():
        carry_scr[...] = jnp.zeros_like(carry_scr)

    pre = g_ref[...] + bias_ref[...]
    ls = jnp.minimum(pre, 0.0) - jnp.log1p(jnp.exp(-jnp.abs(pre)))
    row = lax.broadcasted_iota(jnp.int32, ls.shape, 0)
    lane = lax.broadcasted_iota(jnp.int32, ls.shape, 1)
    cs = ls
    k = 1
    while k < ML_CHUNK:
        cs = cs + jnp.where(row >= k, pltpu.roll(cs, k, 0), 0.0)
        k *= 2
    run = cs + carry_scr[...]
    carry_scr[...] = run[ML_CHUNK - 1:ML_CHUNK, :]
    out = jnp.where(lane < G_AF, pre, jnp.where(lane < G_BF, cs, run))
    col_ref[...] = out
    row_ref[0] = out.T[0:16, :]


def _gates(gate_pre, gate_bias, batch, seq):
    m = gate_pre.shape[0]
    nc = seq // ML_CHUNK
    return pl.pallas_call(
        _gates_kernel,
        grid=(batch, nc),
        in_specs=[
            pl.BlockSpec((ML_CHUNK, LANES), lambda b, c: (b * nc + c, 0)),
            pl.BlockSpec((1, LANES), lambda b, c: (0, 0)),
        ],
        out_specs=[
            pl.BlockSpec((ML_CHUNK, LANES), lambda b, c: (b * nc + c, 0)),
            pl.BlockSpec((1, 16, ML_CHUNK), lambda b, c: (b, 0, c)),
        ],
        out_shape=[
            jax.ShapeDtypeStruct((m, LANES), jnp.float32),
            jax.ShapeDtypeStruct((batch, 16, seq), jnp.float32),
        ],
        scratch_shapes=[pltpu.VMEM((1, LANES), jnp.float32)],
        compiler_params=pltpu.CompilerParams(dimension_semantics=("arbitrary", "arbitrary")),
        name="gates",
    )(gate_pre, gate_bias)


def _mlstm_kernel(aq_ref, ak_ref, av_ref, ao_ref, az_ref, cw_ref, gc_ref, gr_ref, ng_ref,
                  y_ref, qbuf, kbuf, ct_scr, m_scr):
    L = ML_CHUNK
    dh = ML_HEAD_DIM

    @pl.when(pl.program_id(1) == 0)
    def _():
        qbuf[0:CONV_HALO, :] = jnp.zeros((CONV_HALO, D_MODEL), jnp.float32)
        kbuf[0:CONV_HALO, :] = jnp.zeros((CONV_HALO, D_MODEL), jnp.float32)
        ct_scr[...] = jnp.zeros_like(ct_scr)
        m_scr[...] = jnp.zeros_like(m_scr)

    qbuf[CONV_HALO:, :] = aq_ref[...].astype(jnp.float32)
    kbuf[CONV_HALO:, :] = ak_ref[...].astype(jnp.float32)

    def conv(buf, w):
        base = CONV_HALO - (CONV_WIDTH - 1)
        acc = buf[base:base + L, :] * w[0:1, :]
        for kk in range(1, CONV_WIDTH):
            acc = acc + buf[base + kk:base + kk + L, :] * w[kk:kk + 1, :]
        return acc

    cw = cw_ref[...]
    qc = (_silu(conv(qbuf, cw[:, :D_MODEL])) * (dh ** -0.5)).astype(jnp.bfloat16)
    kc = _silu(conv(kbuf, cw[:, D_MODEL:]))
    qbuf[0:CONV_HALO, :] = qbuf[L:L + CONV_HALO, :]
    kbuf[0:CONV_HALO, :] = kbuf[L:L + CONV_HALO, :]

    gcol = gc_ref[...]
    grow = gr_ref[0]
    t_idx = lax.broadcasted_iota(jnp.int32, (L, L), 0)
    s_idx = lax.broadcasted_iota(jnp.int32, (L, L), 1)
    causal = s_idx <= t_idx
    ones_col = (lax.broadcasted_iota(jnp.int32, (L, LANES), 1) == 0).astype(jnp.bfloat16)

    for h in range(ML_HEADS):
        sl = slice(h * dh, (h + 1) * dh)
        q = qc[:, sl]
        kf = kc[:, sl]
        kb = kf.astype(jnp.bfloat16)
        v_aug = jnp.concatenate([av_ref[:, sl], ones_col], axis=1)
        i_col = gcol[:, G_AI + h:G_AI + h + 1]
        b_col = gcol[:, G_AF + h:G_AF + h + 1]
        i_row = grow[G_AI + h:G_AI + h + 1, :]
        b_row = grow[G_AF + h:G_AF + h + 1, :]
        m_prev = m_scr[h:h + 1, 0:1]

        d = jnp.where(causal, b_col + (i_row - b_row), NEG)
        inter = b_col + m_prev
        m_t = jnp.maximum(inter, jnp.max(d, axis=-1, keepdims=True))
        w_inter = jnp.exp(inter - m_t)
        e = jnp.exp(d - m_t)
        s = lax.dot_general(q, kb, (((1,), (1,)), ((), ())), preferred_element_type=jnp.float32)
        p = (e * s).astype(jnp.bfloat16)
        ct = ct_scr[h]
        num_aug = (w_inter * jnp.dot(q, ct.astype(jnp.bfloat16), preferred_element_type=jnp.float32)
                   + jnp.dot(p, v_aug, preferred_element_type=jnp.float32))
        num = num_aug[:, :dh]
        den = num_aug[:, dh:dh + 1]
        hh = num / jnp.maximum(jnp.abs(den), jnp.exp(-m_t))
        hh = hh * lax.rsqrt(jnp.mean(hh * hh, axis=-1, keepdims=True) + EPS)
        y = (hh * ng_ref[:, sl] * _sigmoid(ao_ref[:, sl].astype(jnp.float32))
             * _silu(az_ref[:, sl].astype(jnp.float32)))
        y_ref[:, sl] = y.astype(jnp.bfloat16)

        b_last = b_col[L - 1:L, :]
        dec = b_last - b_col + i_col
        m_new = jnp.maximum(b_last + m_prev, jnp.max(dec, axis=0, keepdims=True))
        w_s = jnp.exp(dec - m_new)
        w_old = jnp.exp(b_last + m_prev - m_new)
        vw = (v_aug.astype(jnp.float32) * w_s).astype(jnp.bfloat16)
        upd = jnp.dot(kf.T.astype(jnp.bfloat16), vw, preferred_element_type=jnp.float32)
        ct_scr[h] = w_old * ct + upd
        m_scr[h:h + 1, :] = jnp.broadcast_to(m_new, (1, LANES))


def _mlstm(proj, conv_w, gcol, grow, ml_norm_g, batch, seq):
    m = proj.shape[0]
    L = ML_CHUNK
    nc = seq // L

    def col(c):
        return pl.BlockSpec((L, D_MODEL), lambda b, i, c=c: (b * nc + i, c))

    return pl.pallas_call(
        _mlstm_kernel,
        grid=(batch, nc),
        in_specs=[
            col(C_AQ), col(C_AK), col(C_AV), col(C_AO), col(C_AZ),
            pl.BlockSpec((CONV_WIDTH, 2 * D_MODEL), lambda b, i: (0, 0)),
            pl.BlockSpec((L, LANES), lambda b, i: (b * nc + i, 0)),
            pl.BlockSpec((1, 16, L), lambda b, i: (b, 0, i)),
            pl.BlockSpec((1, D_MODEL), lambda b, i: (0, 0)),
        ],
        out_specs=pl.BlockSpec((L, D_MODEL), lambda b, i: (b * nc + i, 0)),
        out_shape=jax.ShapeDtypeStruct((m, D_MODEL), jnp.bfloat16),
        scratch_shapes=[
            pltpu.VMEM((CONV_HALO + L, D_MODEL), jnp.float32),
            pltpu.VMEM((CONV_HALO + L, D_MODEL), jnp.float32),
            pltpu.VMEM((ML_HEADS, ML_HEAD_DIM, ML_HEAD_DIM + LANES), jnp.float32),
            pltpu.VMEM((8, LANES), jnp.float32),
        ],
        compiler_params=pltpu.CompilerParams(
            dimension_semantics=("arbitrary", "arbitrary"), vmem_limit_bytes=VMEM_LIMIT),
        name="mlstm",
    )(proj, proj, proj, proj, proj, conv_w, gcol, grow, ml_norm_g)


def _fox_kernel(q_ref, k_ref, v_ref, z_ref, g_ref, y_ref, qa_scr, ka_scr, st_scr, *, seq):
    tq, tk = FOX_TQ, FOX_TK
    h = pl.program_id(1)
    i = pl.program_id(2)
    lane1 = lax.broadcasted_iota(jnp.int32, (1, LANES), 1)

    @pl.when(i == 0)
    def _():
        lane = lax.broadcasted_iota(jnp.int32, (tk, LANES), 1)
        ones_q = ((lane >= 3) & (lane < 6)).astype(jnp.float32)
        ones_k = (lane < 3).astype(jnp.float32)

        def build(c, carry):
            kmax, qn_vec, g_vec, fe_vec = carry
            r0 = pl.multiple_of(c * tk, tk)
            g = g_ref[pl.ds(r0, tk), :]
            f = jnp.sum(jnp.where(lane == G_BF + h, g, 0.0), axis=1, keepdims=True) * LOG2E
            hi = f.astype(jnp.bfloat16).astype(jnp.float32)
            r1 = f - hi
            mid = r1.astype(jnp.bfloat16).astype(jnp.float32)
            lo = r1 - mid
            qx = jnp.where(lane == 0, hi, jnp.where(lane == 1, mid, jnp.where(lane == 2, lo, ones_q)))
            kx = jnp.where(lane == 3, -hi, jnp.where(lane == 4, -mid, jnp.where(lane == 5, -lo, ones_k)))
            qs = (q_ref[pl.ds(r0, tk), :].astype(jnp.float32)
                  * (FOX_HEAD_DIM ** -0.5 * LOG2E)).astype(jnp.bfloat16)
            kk = k_ref[pl.ds(r0, tk), :]
            qa_scr[pl.ds(r0, tk), :] = jnp.concatenate([qs, qx.astype(jnp.bfloat16)], axis=1)
            ka_scr[pl.ds(r0, tk), :] = jnp.concatenate([kk, kx.astype(jnp.bfloat16)], axis=1)

            qf = qs.astype(jnp.float32)
            kf = kk.astype(jnp.float32)
            qn = jnp.sqrt(jnp.sum(qf * qf, axis=1, keepdims=True))
            kn = jnp.sqrt(jnp.sum(kf * kf, axis=1, keepdims=True))
            diag = jnp.sum(qf * kf, axis=1, keepdims=True)
            here = lane1 == c
            kmax = jnp.maximum(kmax, jnp.max(kn, axis=0, keepdims=True))
            qn_vec = jnp.where(here, jnp.max(qn, axis=0, keepdims=True), qn_vec)
            g_vec = jnp.where(here, jnp.max(f - diag, axis=0, keepdims=True), g_vec)
            fe_vec = jnp.where(here, f[tk - 1:tk, :], fe_vec)
            return kmax, qn_vec, g_vec, fe_vec

        zrow = jnp.zeros((1, LANES), jnp.float32)
        kmax, qn_vec, g_vec, fe_vec = lax.fori_loop(
            0, seq // tk, build, (jnp.zeros((1, 1), jnp.float32), zrow, zrow, zrow))
        st_scr[0:1, :] = qn_vec * kmax + g_vec
        st_scr[1:2, :] = fe_vec

    def back_blocks(blk):
        bound = jnp.sum(jnp.where(lane1 == blk, st_scr[0:1, :], 0.0), axis=1, keepdims=True)
        needed = (lane1 < blk) & jnp.logical_not(st_scr[1:2, :] >= bound + SKIP_MARGIN * LOG2E)
        return blk - jnp.min(jnp.where(needed, lane1, blk))

    blks = [i * FOX_PAIR + u for u in range(FOX_PAIR)]
    n_back = functools.reduce(jnp.maximum, [back_blocks(b) for b in blks])
    qas = [qa_scr[pl.ds(pl.multiple_of(b * tq, tq), tq), :] for b in blks]

    def attend(carry, t, diagonal):
        m, l, acc = carry
        k0s, parts = [], []
        for u in range(FOX_PAIR):
            k0 = pl.multiple_of(jnp.maximum(blks[u] - t, 0) * tk, tk)
            s = lax.dot_general(qas[u], ka_scr[pl.ds(k0, tk), :], (((1,), (1,)), ((), ())),
                                preferred_element_type=jnp.float32)
            if diagonal:
                r_idx = lax.broadcasted_iota(jnp.int32, (tq, tk), 0)
                c_idx = lax.broadcasted_iota(jnp.int32, (tq, tk), 1)
                s = jnp.where(c_idx <= r_idx, s, NEG)
            elif u < FOX_PAIR - 1:
                s = jnp.where(t <= blks[u], s, NEG)
            k0s.append(k0)
            parts.append(s)
        s = jnp.concatenate(parts, axis=0)
        m_new = jnp.maximum(m, jnp.max(s, axis=-1, keepdims=True))
        alpha = jnp.exp2(m - m_new)
        p = jnp.exp2(s - m_new)
        l = alpha * l + jnp.sum(p, axis=-1, keepdims=True)
        p = p.astype(jnp.bfloat16)
        pv = jnp.concatenate(
            [jnp.dot(p[u * tq:(u + 1) * tq], v_ref[pl.ds(k0s[u], tk), :], preferred_element_type=jnp.float32)
             for u in range(FOX_PAIR)], axis=0)
        return m_new, l, alpha * acc + pv

    rows = FOX_PAIR * tq
    init = (jnp.full((rows, 1), NEG, jnp.float32), jnp.zeros((rows, 1), jnp.float32),
            jnp.zeros((rows, FOX_HEAD_DIM), jnp.float32))
    carry = attend(init, 0, True)
    _, l, acc = lax.fori_loop(1, n_back + 1, lambda t, c: attend(c, t, False), carry)
    y_ref[...] = (acc / l * _silu(z_ref[...].astype(jnp.float32))).astype(jnp.bfloat16)


def _fox(proj, gcol, batch, seq):
    m = proj.shape[0]
    rows = FOX_PAIR * FOX_TQ
    nq = seq // rows
    assert FOX_TQ == FOX_TK and seq % rows == 0 and seq // FOX_TQ <= LANES
    hb = D_MODEL // FOX_HEAD_DIM

    def head(c):
        return pl.BlockSpec((seq, FOX_HEAD_DIM), lambda b, h, i, c=c: (b, c * hb + h))

    return pl.pallas_call(
        functools.partial(_fox_kernel, seq=seq),
        grid=(batch, FOX_HEADS, nq),
        in_specs=[
            head(C_BQ), head(C_BK), head(C_BV),
            pl.BlockSpec((rows, FOX_HEAD_DIM), lambda b, h, i: (b * nq + i, C_BZ * hb + h)),
            pl.BlockSpec((seq, LANES), lambda b, h, i: (b, 0)),
        ],
        out_specs=pl.BlockSpec((rows, FOX_HEAD_DIM), lambda b, h, i: (b * nq + i, h)),
        out_shape=jax.ShapeDtypeStruct((m, D_MODEL), jnp.bfloat16),
        scratch_shapes=[
            pltpu.VMEM((seq, 2 * FOX_HEAD_DIM), jnp.bfloat16),
            pltpu.VMEM((seq, 2 * FOX_HEAD_DIM), jnp.bfloat16),
            pltpu.VMEM((8, LANES), jnp.float32),
        ],
        compiler_params=pltpu.CompilerParams(
            dimension_semantics=("arbitrary", "arbitrary", "arbitrary"), vmem_limit_bytes=VMEM_LIMIT),
        name="fox",
    )(proj, proj, proj, proj, gcol)


def _merge_kernel(ya_ref, yb_ref, cu_ref, halo_ref, cz_ref, gates_ref, x_ref, pw_ref, ps_ref,
                  wb_ref, wo_ref, fg_ref, o_ref, *, seq, final):
    tm = MERGE_TM
    gd = POOL_GROUP_DIM
    i = pl.program_id(0)
    t0 = (i * tm) % seq
    pos = t0 + lax.broadcasted_iota(jnp.int32, (tm, 1), 0)
    halo = jnp.where(t0 == 0, 0.0, halo_ref[...].astype(jnp.float32))
    u = cu_ref[...].astype(jnp.float32)
    ext = jnp.concatenate([halo, u], axis=0)

    ys = []
    for g, win in enumerate(POOL_WINDOWS):
        sl = slice(g * gd, (g + 1) * gd)
        acc = ext[:, sl]
        step = 1
        while step < win:
            acc = acc + pltpu.roll(acc, step, 0)
            step *= 2
        cnt = jnp.minimum(pos + 1, win).astype(jnp.float32)
        d = acc[HALO:, :] / cnt - u[:, sl]
        ys.append(jnp.dot(d.astype(jnp.bfloat16), pw_ref[g], preferred_element_type=jnp.float32))
    yc = jnp.concatenate(ys, axis=1) * ps_ref[...]
    yc = yc * _silu(cz_ref[...].astype(jnp.float32))

    merged = jnp.zeros((tm, D_MODEL), jnp.float32)
    for n, y in enumerate((ya_ref[...], yb_ref[...], yc.astype(jnp.bfloat16))):
        yb = jnp.dot(y, wb_ref[n], preferred_element_type=jnp.float32)
        gate = _sigmoid(gates_ref[:, n * D_MODEL:(n + 1) * D_MODEL].astype(jnp.float32))
        merged = merged + gate * yb
    out = x_ref[...] + jnp.dot(merged.astype(jnp.bfloat16), wo_ref[...],
                               preferred_element_type=jnp.float32)
    if final:
        r = lax.rsqrt(jnp.mean(out * out, axis=-1, keepdims=True) + EPS)
        out = (out * r) * fg_ref[...]
    o_ref[...] = out


def _merge(ya, yb, proj, x2, pool_w, pool_scale, w_branch, w_out, final_g, seq, final, layer):
    m = x2.shape[0]
    tm = MERGE_TM
    hpt = tm // HALO

    def rowblk(c, width=1):
        return pl.BlockSpec((tm, width * D_MODEL), lambda i, c=c: (i, c))

    const2 = lambda i: (0, 0)
    layer3 = lambda i: (layer, 0, 0)
    layer4 = lambda i: (layer, 0, 0, 0)
    return pl.pallas_call(
        functools.partial(_merge_kernel, seq=seq, final=final),
        grid=(m // tm,),
        in_specs=[
            rowblk(0), rowblk(0), rowblk(C_CU),
            pl.BlockSpec((HALO, D_MODEL), lambda i: (jnp.maximum(i * hpt - 1, 0), C_CU)),
            rowblk(C_CZ),
            pl.BlockSpec((tm, N_BRANCH * D_MODEL), lambda i: (i, C_GATES // N_BRANCH)),
            rowblk(0),
            pl.BlockSpec((None, POOL_GROUPS, POOL_GROUP_DIM, POOL_GROUP_DIM), layer4),
            pl.BlockSpec((1, D_MODEL), const2),
            pl.BlockSpec((None, N_BRANCH, D_MODEL, D_MODEL), layer4),
            pl.BlockSpec((None, D_MODEL, D_MODEL), layer3),
            pl.BlockSpec((1, D_MODEL), const2),
        ],
        out_specs=pl.BlockSpec((tm, D_MODEL), lambda i: (i, 0)),
        out_shape=jax.ShapeDtypeStruct((m, D_MODEL), jnp.float32),
        compiler_params=pltpu.CompilerParams(
            dimension_semantics=("arbitrary",), vmem_limit_bytes=VMEM_LIMIT),
        name="merge",
    )(ya, yb, proj, proj, proj, proj, x2, pool_w, pool_scale, w_branch, w_out, final_g)


O_AI = 5 * D_MODEL
O_BQ = O_AI + 2 * ML_HEADS
O_BF = O_BQ + 4 * D_MODEL
O_CU = O_BF + FOX_HEADS
O_G = O_CU + 2 * D_MODEL
N_IN = O_G + N_BRANCH * D_MODEL
W_SEGMENTS = ((O_G, N_BRANCH), (0, 5), (O_BQ, 4), (O_CU, 2))
W_PREP_ROWS = 128


def _w_prep_kernel(wt_ref, main_ref, gate_ref):
    dst = 0
    for src, nblk in W_SEGMENTS:
        for n in range(nblk):
            a = src + n * D_MODEL
            main_ref[:, dst:dst + D_MODEL] = wt_ref[a:a + D_MODEL, :].T.astype(jnp.bfloat16)
            dst += D_MODEL
    n_gate = 2 * ML_HEADS + FOX_HEADS
    gate = jnp.concatenate([wt_ref[O_AI:O_BQ, :], wt_ref[O_BF:O_CU, :],
                            jnp.zeros((LANES - n_gate, W_PREP_ROWS), jnp.float32)], axis=0)
    gate_ref[...] = gate.T.astype(jnp.bfloat16)


def _split_w_in(w_in):
    depth = w_in.shape[0]
    assert w_in.shape[1:] == (D_MODEL, N_IN)
    return pl.pallas_call(
        _w_prep_kernel,
        grid=(depth, D_MODEL // W_PREP_ROWS),
        in_specs=[pl.BlockSpec((None, N_IN, W_PREP_ROWS), lambda l, r: (l, 0, r))],
        out_specs=[
            pl.BlockSpec((None, W_PREP_ROWS, N_MAIN), lambda l, r: (l, r, 0)),
            pl.BlockSpec((None, W_PREP_ROWS, LANES), lambda l, r: (l, r, 0)),
        ],
        out_shape=[
            jax.ShapeDtypeStruct((depth, D_MODEL, N_MAIN), jnp.bfloat16),
            jax.ShapeDtypeStruct((depth, D_MODEL, LANES), jnp.bfloat16),
        ],
        compiler_params=pltpu.CompilerParams(
            dimension_semantics=("arbitrary", "arbitrary"), vmem_limit_bytes=VMEM_LIMIT),
        name="w_prep",
    )(jnp.swapaxes(w_in, 1, 2))


def kernel(x, norm_g, w_in, conv_w, ml_bi, ml_bf, ml_norm_g, fox_bf, pool_w, pool_scale, w_branch,
           w_out, final_g):
    batch, seq, d = x.shape
    depth = norm_g.shape[0]
    assert d == D_MODEL and seq % max(IN_TM, FOX_TQ, MERGE_TM, ML_CHUNK) == 0
    x2 = x.reshape(batch * seq, d)
    w_main, w_gate = _split_w_in(w_in)
    pool_wb = pool_w.astype(jnp.bfloat16)
    w_branchb = w_branch.astype(jnp.bfloat16)
    w_outb = w_out.astype(jnp.bfloat16)
    for l in range(depth):
        gate_bias = jnp.pad(jnp.concatenate([ml_bi[l], ml_bf[l], fox_bf[l]]),
                            (0, LANES - 2 * ML_HEADS - FOX_HEADS)).reshape(1, LANES)
        proj, gate_pre = _in_proj(x2, norm_g[l].reshape(1, d), w_main, w_gate, l)
        gcol, grow = _gates(gate_pre, gate_bias, batch, seq)
        ya = _mlstm(proj, conv_w[l], gcol, grow, ml_norm_g[l].reshape(1, d), batch, seq)
        yb = _fox(proj, gcol, batch, seq)
        x2 = _merge(ya, yb, proj, x2, pool_wb, pool_scale[l].reshape(1, d), w_branchb, w_outb,
                    final_g.reshape(1, d), seq, final=(l == depth - 1), layer=l)
    return x2.reshape(batch, seq, d)
```

```python
import functools

import jax
import jax.numpy as jnp
from jax import lax
from jax.experimental import pallas as pl
from jax.experimental.pallas import tpu as pltpu

D_MODEL = 1024
ML_HEADS = 4
ML_HEAD_DIM = 256
CONV_WIDTH = 4
FOX_HEADS = 8
FOX_HEAD_DIM = 128
POOL_GROUPS = 4
POOL_GROUP_DIM = 256
POOL_WINDOWS = (2, 4, 8, 16)
N_BRANCH = 3
EPS = 1e-6

LANES = 128
HALO = 16
CONV_HALO = 8
NEG = -1e30
SKIP_MARGIN = 32.0

C_GATES = 0
C_AQ, C_AK, C_AV, C_AO, C_AZ, C_BQ, C_BK, C_BV, C_BZ, C_CU, C_CZ = range(N_BRANCH, N_BRANCH + 11)
N_MAIN = 14 * D_MODEL
G_AI, G_AF, G_BF = 0, 4, 8

ML_CHUNK = 256
IN_TM, IN_TN = 2048, 1024
FOX_TQ = 256
FOX_TK = 256
FOX_PAIR = 8
LOG2E = 1.4426950408889634
MERGE_TM = 512
VMEM_LIMIT = 56 * 1024 * 1024


def _sigmoid(x):
    return 1.0 / (1.0 + jnp.exp(-x))


def _silu(x):
    return x * _sigmoid(x)


def _in_proj_kernel(x_ref, g_ref, w_ref, wg_ref, proj_ref, gate_ref, h_scr):
    @pl.when(pl.program_id(1) == 0)
    def _():
        x = x_ref[...]
        r = lax.rsqrt(jnp.mean(x * x, axis=-1, keepdims=True) + EPS)
        h = ((x * r) * g_ref[...]).astype(jnp.bfloat16)
        h_scr[...] = h
        gate_ref[...] = jnp.dot(h, wg_ref[...], preferred_element_type=jnp.float32)

    proj_ref[...] = jnp.dot(h_scr[...], w_ref[...],
                            preferred_element_type=jnp.float32).astype(jnp.bfloat16)


def _in_proj(x2, norm_g, w_main, w_gate, layer):
    m = x2.shape[0]
    return pl.pallas_call(
        _in_proj_kernel,
        grid=(m // IN_TM, N_MAIN // IN_TN),
        in_specs=[
            pl.BlockSpec((IN_TM, D_MODEL), lambda i, j: (i, 0)),
            pl.BlockSpec((1, D_MODEL), lambda i, j: (0, 0)),
            pl.BlockSpec((None, D_MODEL, IN_TN), lambda i, j: (layer, 0, j)),
            pl.BlockSpec((None, D_MODEL, LANES), lambda i, j: (layer, 0, 0)),
        ],
        out_specs=[
            pl.BlockSpec((IN_TM, IN_TN), lambda i, j: (i, j)),
            pl.BlockSpec((IN_TM, LANES), lambda i, j: (i, 0)),
        ],
        out_shape=[
            jax.ShapeDtypeStruct((m, N_MAIN), jnp.bfloat16),
            jax.ShapeDtypeStruct((m, LANES), jnp.float32),
        ],
        scratch_shapes=[pltpu.VMEM((IN_TM, D_MODEL), jnp.bfloat16)],
        compiler_params=pltpu.CompilerParams(
            dimension_semantics=("arbitrary", "arbitrary"), vmem_limit_bytes=VMEM_LIMIT),
        name="in_proj",
    )(x2, norm_g, w_main, w_gate)


def _gates_kernel(g_ref, bias_ref, col_ref, row_ref, carry_scr):
    @pl.when(pl.program_id(1) == 0)
    def _():
        carry_scr[...] = jnp.zeros_like(carry_scr)

    pre = g_ref[...] + bias_ref[...]
    ls = jnp.minimum(pre, 0.0) - jnp.log1p(jnp.exp(-jnp.abs(pre)))
    row = lax.broadcasted_iota(jnp.int32, ls.shape, 0)
    lane = lax.broadcasted_iota(jnp.int32, ls.shape, 1)
    cs = ls
    k = 1
    while k < ML_CHUNK:
        cs = cs + jnp.where(row >= k, pltpu.roll(cs, k, 0), 0.0)
        k *= 2
    run = cs + carry_scr[...]
    carry_scr[...] = run[ML_CHUNK - 1:ML_CHUNK, :]
    out = jnp.where(lane < G_AF, pre, jnp.where(lane < G_BF, cs, run))
    col_ref[...] = out
    row_ref[0] = out.T[0:16, :]


def _gates(gate_pre, gate_bias, batch, seq):
    m = gate_pre.shape[0]
    nc = seq // ML_CHUNK
    return pl.pallas_call(
        _gates_kernel,
        grid=(batch, nc),
        in_specs=[
            pl.BlockSpec((ML_CHUNK, LANES), lambda b, c: (b * nc + c, 0)),
            pl.BlockSpec((1, LANES), lambda b, c: (0, 0)),
        ],
        out_specs=[
            pl.BlockSpec((ML_CHUNK, LANES), lambda b, c: (b * nc + c, 0)),
            pl.BlockSpec((1, 16, ML_CHUNK), lambda b, c: (b, 0, c)),
        ],
        out_shape=[
            jax.ShapeDtypeStruct((m, LANES), jnp.float32),
            jax.ShapeDtypeStruct((batch, 16, seq), jnp.float32),
        ],
        scratch_shapes=[pltpu.VMEM((1, LANES), jnp.float32)],
        compiler_params=pltpu.CompilerParams(dimension_semantics=("arbitrary", "arbitrary")),
        name="gates",
    )(gate_pre, gate_bias)


def _mlstm_kernel(aq_ref, ak_ref, av_ref, ao_ref, az_ref, cw_ref, gc_ref, gr_ref, ng_ref,
                  y_ref, qbuf, kbuf, ct_scr, m_scr):
    L = ML_CHUNK
    dh = ML_HEAD_DIM

    @pl.when(pl.program_id(1) == 0)
    def _():
        qbuf[0:CONV_HALO, :] = jnp.zeros((CONV_HALO, D_MODEL), jnp.float32)
        kbuf[0:CONV_HALO, :] = jnp.zeros((CONV_HALO, D_MODEL), jnp.float32)
        ct_scr[...] = jnp.zeros_like(ct_scr)
        m_scr[...] = jnp.zeros_like(m_scr)

    qbuf[CONV_HALO:, :] = aq_ref[...].astype(jnp.float32)
    kbuf[CONV_HALO:, :] = ak_ref[...].astype(jnp.float32)

    def conv(buf, w):
        base = CONV_HALO - (CONV_WIDTH - 1)
        acc = buf[base:base + L, :] * w[0:1, :]
        for kk in range(1, CONV_WIDTH):
            acc = acc + buf[base + kk:base + kk + L, :] * w[kk:kk + 1, :]
        return acc

    cw = cw_ref[...]
    qc = (_silu(conv(qbuf, cw[:, :D_MODEL])) * (dh ** -0.5)).astype(jnp.bfloat16)
    kc = _silu(conv(kbuf, cw[:, D_MODEL:]))
    qbuf[0:CONV_HALO, :] = qbuf[L:L + CONV_HALO, :]
    kbuf[0:CONV_HALO, :] = kbuf[L:L + CONV_HALO, :]

    gcol = gc_ref[...]
    grow = gr_ref[0]
    t_idx = lax.broadcasted_iota(jnp.int32, (L, L), 0)
    s_idx = lax.broadcasted_iota(jnp.int32, (L, L), 1)
    causal = s_idx <= t_idx
    ones_col = (lax.broadcasted_iota(jnp.int32, (L, LANES), 1) == 0).astype(jnp.bfloat16)

    for h in range(ML_HEADS):
        sl = slice(h * dh, (h + 1) * dh)
        q = qc[:, sl]
        kf = kc[:, sl]
        kb = kf.astype(jnp.bfloat16)
        v_aug = jnp.concatenate([av_ref[:, sl], ones_col], axis=1)
        i_col = gcol[:, G_AI + h:G_AI + h + 1]
        b_col = gcol[:, G_AF + h:G_AF + h + 1]
        i_row = grow[G_AI + h:G_AI + h + 1, :]
        b_row = grow[G_AF + h:G_AF + h + 1, :]
        m_prev = m_scr[h:h + 1, 0:1]

        d = jnp.where(causal, b_col + (i_row - b_row), NEG)
        inter = b_col + m_prev
        m_t = jnp.maximum(inter, jnp.max(d, axis=-1, keepdims=True))
        w_inter = jnp.exp(inter - m_t)
        e = jnp.exp(d - m_t)
        s = lax.dot_general(q, kb, (((1,), (1,)), ((), ())), preferred_element_type=jnp.float32)
        p = (e * s).astype(jnp.bfloat16)
        ct = ct_scr[h]
        num_aug = (w_inter * jnp.dot(q, ct.astype(jnp.bfloat16), preferred_element_type=jnp.float32)
                   + jnp.dot(p, v_aug, preferred_element_type=jnp.float32))
        num = num_aug[:, :dh]
        den = num_aug[:, dh:dh + 1]
        hh = num / jnp.maximum(jnp.abs(den), jnp.exp(-m_t))
        hh = hh * lax.rsqrt(jnp.mean(hh * hh, axis=-1, keepdims=True) + EPS)
        y = (hh * ng_ref[:, sl] * _sigmoid(ao_ref[:, sl].astype(jnp.float32))
             * _silu(az_ref[:, sl].astype(jnp.float32)))
        y_ref[:, sl] = y.astype(jnp.bfloat16)

        b_last = b_col[L - 1:L, :]
        dec = b_last - b_col + i_col
        m_new = jnp.maximum(b_last + m_prev, jnp.max(dec, axis=0, keepdims=True))
        w_s = jnp.exp(dec - m_new)
        w_old = jnp.exp(b_last + m_prev - m_new)
        vw = (v_aug.astype(jnp.float32) * w_s).astype(jnp.bfloat16)
        upd = jnp.dot(kf.T.astype(jnp.bfloat16), vw, preferred_element_type=jnp.float32)
        ct_scr[h] = w_old * ct + upd
        m_scr[h:h + 1, :] = jnp.broadcast_to(m_new, (1, LANES))


def _mlstm(proj, conv_w, gcol, grow, ml_norm_g, batch, seq):
    m = proj.shape[0]
    L = ML_CHUNK
    nc = seq // L

    def col(c):
        return pl.BlockSpec((L, D_MODEL), lambda b, i, c=c: (b * nc + i, c))

    return pl.pallas_call(
        _mlstm_kernel,
        grid=(batch, nc),
        in_specs=[
            col(C_AQ), col(C_AK), col(C_AV), col(C_AO), col(C_AZ),
            pl.BlockSpec((CONV_WIDTH, 2 * D_MODEL), lambda b, i: (0, 0)),
            pl.BlockSpec((L, LANES), lambda b, i: (b * nc + i, 0)),
            pl.BlockSpec((1, 16, L), lambda b, i: (b, 0, i)),
            pl.BlockSpec((1, D_MODEL), lambda b, i: (0, 0)),
        ],
        out_specs=pl.BlockSpec((L, D_MODEL), lambda b, i: (b * nc + i, 0)),
        out_shape=jax.ShapeDtypeStruct((m, D_MODEL), jnp.bfloat16),
        scratch_shapes=[
            pltpu.VMEM((CONV_HALO + L, D_MODEL), jnp.float32),
            pltpu.VMEM((CONV_HALO + L, D_MODEL), jnp.float32),
            pltpu.VMEM((ML_HEADS, ML_HEAD_DIM, ML_HEAD_DIM + LANES), jnp.float32),
            pltpu.VMEM((8, LANES), jnp.float32),
        ],
        compiler_params=pltpu.CompilerParams(
            dimension_semantics=("arbitrary", "arbitrary"), vmem_limit_bytes=VMEM_LIMIT),
        name="mlstm",
    )(proj, proj, proj, proj, proj, conv_w, gcol, grow, ml_norm_g)


def _fox_kernel(q_ref, k_ref, v_ref, z_ref, g_ref, y_ref, qa_scr, ka_scr, vt_scr, st_scr, *, seq):
    tq, tk = FOX_TQ, FOX_TK
    h = pl.program_id(1)
    i = pl.program_id(2)
    lane1 = lax.broadcasted_iota(jnp.int32, (1, LANES), 1)

    @pl.when(i == 0)
    def _():
        lane = lax.broadcasted_iota(jnp.int32, (tk, LANES), 1)
        ones_q = ((lane >= 3) & (lane < 6)).astype(jnp.float32)
        ones_k = (lane < 3).astype(jnp.float32)

        def build(c, carry):
            kmax, qn_vec, g_vec, fe_vec = carry
            r0 = pl.multiple_of(c * tk, tk)
            g = g_ref[pl.ds(r0, tk), :]
            f = jnp.sum(jnp.where(lane == G_BF + h, g, 0.0), axis=1, keepdims=True) * LOG2E
            hi = f.astype(jnp.bfloat16).astype(jnp.float32)
            r1 = f - hi
            mid = r1.astype(jnp.bfloat16).astype(jnp.float32)
            lo = r1 - mid
            qx = jnp.where(lane == 0, hi, jnp.where(lane == 1, mid, jnp.where(lane == 2, lo, ones_q)))
            kx = jnp.where(lane == 3, -hi, jnp.where(lane == 4, -mid, jnp.where(lane == 5, -lo, ones_k)))
            qs = (q_ref[pl.ds(r0, tk), :].astype(jnp.float32)
                  * (FOX_HEAD_DIM ** -0.5 * LOG2E)).astype(jnp.bfloat16)
            kk = k_ref[pl.ds(r0, tk), :]
            qa_scr[pl.ds(r0, tk), :] = jnp.concatenate([qs, qx.astype(jnp.bfloat16)], axis=1)
            ka_scr[pl.ds(r0, tk), :] = jnp.concatenate([kk, kx.astype(jnp.bfloat16)], axis=1)
            vt_scr[c] = v_ref[pl.ds(r0, tk), :].astype(jnp.float32).T.astype(jnp.bfloat16)

            qf = qs.astype(jnp.float32)
            kf = kk.astype(jnp.float32)
            qn = jnp.sqrt(jnp.sum(qf * qf, axis=1, keepdims=True))
            kn = jnp.sqrt(jnp.sum(kf * kf, axis=1, keepdims=True))
            diag = jnp.sum(qf * kf, axis=1, keepdims=True)
            here = lane1 == c
            kmax = jnp.maximum(kmax, jnp.max(kn, axis=0, keepdims=True))
            qn_vec = jnp.where(here, jnp.max(qn, axis=0, keepdims=True), qn_vec)
            g_vec = jnp.where(here, jnp.max(f - diag, axis=0, keepdims=True), g_vec)
            fe_vec = jnp.where(here, f[tk - 1:tk, :], fe_vec)
            return kmax, qn_vec, g_vec, fe_vec

        zrow = jnp.zeros((1, LANES), jnp.float32)
        kmax, qn_vec, g_vec, fe_vec = lax.fori_loop(
            0, seq // tk, build, (jnp.zeros((1, 1), jnp.float32), zrow, zrow, zrow))
        st_scr[0:1, :] = qn_vec * kmax + g_vec
        st_scr[1:2, :] = fe_vec

    def back_blocks(blk):
        bound = jnp.sum(jnp.where(lane1 == blk, st_scr[0:1, :], 0.0), axis=1, keepdims=True)
        needed = (lane1 < blk) & jnp.logical_not(st_scr[1:2, :] >= bound + SKIP_MARGIN * LOG2E)
        return blk - jnp.min(jnp.where(needed, lane1, blk))

    blks = [i * FOX_PAIR + u for u in range(FOX_PAIR)]
    n_back = functools.reduce(jnp.maximum, [back_blocks(b) for b in blks])
    qas = [qa_scr[pl.ds(pl.multiple_of(b * tq, tq), tq), :] for b in blks]

    def attend(carry, t, mode):
        m, l, acc = carry
        kbs, parts = [], []
        for u in range(FOX_PAIR):
            kb = jnp.maximum(blks[u] - t, 0)
            k0 = pl.multiple_of(kb * tk, tk)
            s = lax.dot_general(ka_scr[pl.ds(k0, tk), :], qas[u], (((1,), (1,)), ((), ())),
                                preferred_element_type=jnp.float32)
            if mode == "diagonal":
                key_idx = lax.broadcasted_iota(jnp.int32, (tk, tq), 0)
                qry_idx = lax.broadcasted_iota(jnp.int32, (tk, tq), 1)
                s = jnp.where(key_idx <= qry_idx, s, NEG)
            elif mode == "edge" and u < FOX_PAIR - 1:
                s = jnp.where(t <= blks[u], s, NEG)
            kbs.append(kb)
            parts.append(s)
        s = jnp.concatenate(parts, axis=1)
        m_new = jnp.maximum(m, jnp.max(s, axis=0, keepdims=True))
        alpha = jnp.exp2(m - m_new)
        p = jnp.exp2(s - m_new)
        l = alpha * l + jnp.sum(p, axis=0, keepdims=True)
        p = p.astype(jnp.bfloat16)
        pv = jnp.concatenate(
            [jnp.dot(vt_scr[kbs[u]], p[:, u * tq:(u + 1) * tq], preferred_element_type=jnp.float32)
             for u in range(FOX_PAIR)], axis=1)
        return m_new, l, alpha * acc + pv

    rows = FOX_PAIR * tq
    init = (jnp.full((1, rows), NEG, jnp.float32), jnp.zeros((1, rows), jnp.float32),
            jnp.zeros((FOX_HEAD_DIM, rows), jnp.float32))
    carry = attend(init, 0, "diagonal")
    n_plain = jnp.minimum(n_back, blks[0])
    carry = lax.fori_loop(1, n_plain + 1, lambda t, c: attend(c, t, "plain"), carry)
    _, l, acc = lax.fori_loop(n_plain + 1, n_back + 1, lambda t, c: attend(c, t, "edge"), carry)
    y_ref[...] = ((acc / l).T * _silu(z_ref[...].astype(jnp.float32))).astype(jnp.bfloat16)


def _fox(proj, gcol, batch, seq):
    m = proj.shape[0]
    rows = FOX_PAIR * FOX_TQ
    nq = seq // rows
    assert FOX_TQ == FOX_TK and seq % rows == 0 and seq // FOX_TQ <= LANES
    hb = D_MODEL // FOX_HEAD_DIM

    def head(c):
        return pl.BlockSpec((seq, FOX_HEAD_DIM), lambda b, h, i, c=c: (b, c * hb + h))

    return pl.pallas_call(
        functools.partial(_fox_kernel, seq=seq),
        grid=(batch, FOX_HEADS, nq),
        in_specs=[
            head(C_BQ), head(C_BK), head(C_BV),
            pl.BlockSpec((rows, FOX_HEAD_DIM), lambda b, h, i: (b * nq + i, C_BZ * hb + h)),
            pl.BlockSpec((seq, LANES), lambda b, h, i: (b, 0)),
        ],
        out_specs=pl.BlockSpec((rows, FOX_HEAD_DIM), lambda b, h, i: (b * nq + i, h)),
        out_shape=jax.ShapeDtypeStruct((m, D_MODEL), jnp.bfloat16),
        scratch_shapes=[
            pltpu.VMEM((seq, 2 * FOX_HEAD_DIM), jnp.bfloat16),
            pltpu.VMEM((seq, 2 * FOX_HEAD_DIM), jnp.bfloat16),
            pltpu.VMEM((seq // FOX_TK, FOX_HEAD_DIM, FOX_TK), jnp.bfloat16),
            pltpu.VMEM((8, LANES), jnp.float32),
        ],
        compiler_params=pltpu.CompilerParams(
            dimension_semantics=("arbitrary", "arbitrary", "arbitrary"), vmem_limit_bytes=VMEM_LIMIT),
        name="fox",
    )(proj, proj, proj, proj, gcol)


def _merge_kernel(ya_ref, yb_ref, cu_ref, halo_ref, cz_ref, gates_ref, x_ref, pw_ref, ps_ref,
                  wb_ref, wo_ref, fg_ref, o_ref, *, seq, final):
    tm = MERGE_TM
    gd = POOL_GROUP_DIM
    i = pl.program_id(0)
    t0 = (i * tm) % seq
    pos = t0 + lax.broadcasted_iota(jnp.int32, (tm, 1), 0)
    halo = jnp.where(t0 == 0, 0.0, halo_ref[...].astype(jnp.float32))
    u = cu_ref[...].astype(jnp.float32)
    ext = jnp.concatenate([halo, u], axis=0)

    ys = []
    for g, win in enumerate(POOL_WINDOWS):
        sl = slice(g * gd, (g + 1) * gd)
        acc = ext[:, sl]
        step = 1
        while step < win:
            acc = acc + pltpu.roll(acc, step, 0)
            step *= 2
        cnt = jnp.minimum(pos + 1, win).astype(jnp.float32)
        d = acc[HALO:, :] / cnt - u[:, sl]
        ys.append(jnp.dot(d.astype(jnp.bfloat16), pw_ref[g], preferred_element_type=jnp.float32))
    yc = jnp.concatenate(ys, axis=1) * ps_ref[...]
    yc = yc * _silu(cz_ref[...].astype(jnp.float32))

    merged = jnp.zeros((tm, D_MODEL), jnp.float32)
    for n, y in enumerate((ya_ref[...], yb_ref[...], yc.astype(jnp.bfloat16))):
        yb = jnp.dot(y, wb_ref[n], preferred_element_type=jnp.float32)
        gate = _sigmoid(gates_ref[:, n * D_MODEL:(n + 1) * D_MODEL].astype(jnp.float32))
        merged = merged + gate * yb
    out = x_ref[...] + jnp.dot(merged.astype(jnp.bfloat16), wo_ref[...],
                               preferred_element_type=jnp.float32)
    if final:
        r = lax.rsqrt(jnp.mean(out * out, axis=-1, keepdims=True) + EPS)
        out = (out * r) * fg_ref[...]
    o_ref[...] = out


def _merge(ya, yb, proj, x2, pool_w, pool_scale, w_branch, w_out, final_g, seq, final, layer):
    m = x2.shape[0]
    tm = MERGE_TM
    hpt = tm // HALO

    def rowblk(c, width=1):
        return pl.BlockSpec((tm, width * D_MODEL), lambda i, c=c: (i, c))

    const2 = lambda i: (0, 0)
    layer3 = lambda i: (layer, 0, 0)
    layer4 = lambda i: (layer, 0, 0, 0)
    return pl.pallas_call(
        functools.partial(_merge_kernel, seq=seq, final=final),
        grid=(m // tm,),
        in_specs=[
            rowblk(0), rowblk(0), rowblk(C_CU),
            pl.BlockSpec((HALO, D_MODEL), lambda i: (jnp.maximum(i * hpt - 1, 0), C_CU)),
            rowblk(C_CZ),
            pl.BlockSpec((tm, N_BRANCH * D_MODEL), lambda i: (i, C_GATES // N_BRANCH)),
            rowblk(0),
            pl.BlockSpec((None, POOL_GROUPS, POOL_GROUP_DIM, POOL_GROUP_DIM), layer4),
            pl.BlockSpec((1, D_MODEL), const2),
            pl.BlockSpec((None, N_BRANCH, D_MODEL, D_MODEL), layer4),
            pl.BlockSpec((None, D_MODEL, D_MODEL), layer3),
            pl.BlockSpec((1, D_MODEL), const2),
        ],
        out_specs=pl.BlockSpec((tm, D_MODEL), lambda i: (i, 0)),
        out_shape=jax.ShapeDtypeStruct((m, D_MODEL), jnp.float32),
        compiler_params=pltpu.CompilerParams(
            dimension_semantics=("arbitrary",), vmem_limit_bytes=VMEM_LIMIT),
        name="merge",
    )(ya, yb, proj, proj, proj, proj, x2, pool_w, pool_scale, w_branch, w_out, final_g)


O_AI = 5 * D_MODEL
O_BQ = O_AI + 2 * ML_HEADS
O_BF = O_BQ + 4 * D_MODEL
O_CU = O_BF + FOX_HEADS
O_G = O_CU + 2 * D_MODEL
N_IN = O_G + N_BRANCH * D_MODEL
W_SEGMENTS = ((O_G, N_BRANCH), (0, 5), (O_BQ, 4), (O_CU, 2))
W_PREP_ROWS = 128


def _w_prep_kernel(wt_ref, main_ref, gate_ref):
    dst = 0
    for src, nblk in W_SEGMENTS:
        for n in range(nblk):
            a = src + n * D_MODEL
            main_ref[:, dst:dst + D_MODEL] = wt_ref[a:a + D_MODEL, :].T.astype(jnp.bfloat16)
            dst += D_MODEL
    n_gate = 2 * ML_HEADS + FOX_HEADS
    gate = jnp.concatenate([wt_ref[O_AI:O_BQ, :], wt_ref[O_BF:O_CU, :],
                            jnp.zeros((LANES - n_gate, W_PREP_ROWS), jnp.float32)], axis=0)
    gate_ref[...] = gate.T.astype(jnp.bfloat16)


def _split_w_in(w_in):
    depth = w_in.shape[0]
    assert w_in.shape[1:] == (D_MODEL, N_IN)
    return pl.pallas_call(
        _w_prep_kernel,
        grid=(depth, D_MODEL // W_PREP_ROWS),
        in_specs=[pl.BlockSpec((None, N_IN, W_PREP_ROWS), lambda l, r: (l, 0, r))],
        out_specs=[
            pl.BlockSpec((None, W_PREP_ROWS, N_MAIN), lambda l, r: (l, r, 0)),
            pl.BlockSpec((None, W_PREP_ROWS, LANES), lambda l, r: (l, r, 0)),
        ],
        out_shape=[
            jax.ShapeDtypeStruct((depth, D_MODEL, N_MAIN), jnp.bfloat16),
            jax.ShapeDtypeStruct((depth, D_MODEL, LANES), jnp.bfloat16),
        ],
        compiler_params=pltpu.CompilerParams(
            dimension_semantics=("arbitrary", "arbitrary"), vmem_limit_bytes=VMEM_LIMIT),
        name="w_prep",
    )(jnp.swapaxes(w_in, 1, 2))


def kernel(x, norm_g, w_in, conv_w, ml_bi, ml_bf, ml_norm_g, fox_bf, pool_w, pool_scale, w_branch,
           w_out, final_g):
    batch, seq, d = x.shape
    depth = norm_g.shape[0]
    assert d == D_MODEL and seq % max(IN_TM, FOX_TQ, MERGE_TM, ML_CHUNK) == 0
    x2 = x.reshape(batch * seq, d)
    w_main, w_gate = _split_w_in(w_in)
    pool_wb = pool_w.astype(jnp.bfloat16)
    w_branchb = w_branch.astype(jnp.bfloat16)
    w_outb = w_out.astype(jnp.bfloat16)
    for l in range(depth):
        gate_bias = jnp.pad(jnp.concatenate([ml_bi[l], ml_bf[l], fox_bf[l]]),
                            (0, LANES - 2 * ML_HEADS - FOX_HEADS)).reshape(1, LANES)
        proj, gate_pre = _in_proj(x2, norm_g[l].reshape(1, d), w_main, w_gate, l)
        gcol, grow = _gates(gate_pre, gate_bias, batch, seq)
        ya = _mlstm(proj, conv_w[l], gcol, grow, ml_norm_g[l].reshape(1, d), batch, seq)
        yb = _fox(proj, gcol, batch, seq)
        x2 = _merge(ya, yb, proj, x2, pool_wb, pool_scale[l].reshape(1, d), w_branchb, w_outb,
                    final_g.reshape(1, d), seq, final=(l == depth - 1), layer=l)
    return x2.reshape(batch, seq, d)
```

```python
import functools

import jax
import jax.numpy as jnp
from jax import lax
from jax.experimental import pallas as pl
from jax.experimental.pallas import tpu as pltpu

D_MODEL = 1024
ML_HEADS = 4
ML_HEAD_DIM = 256
CONV_WIDTH = 4
FOX_HEADS = 8
FOX_HEAD_DIM = 128
POOL_GROUPS = 4
POOL_GROUP_DIM = 256
POOL_WINDOWS = (2, 4, 8, 16)
N_BRANCH = 3
EPS = 1e-6

LANES = 128
HALO = 16
CONV_HALO = 8
NEG = -1e30
SKIP_MARGIN = 32.0

C_GATES = 0
C_AQ, C_AK, C_AV, C_AO, C_AZ, C_BQ, C_BK, C_BV, C_BZ, C_CU, C_CZ = range(N_BRANCH, N_BRANCH + 11)
N_MAIN = 14 * D_MODEL
G_AI, G_AF, G_BF = 0, 4, 8
G_FHI, G_FMID, G_FLO, G_ONE = 16, 24, 32, 127

ML_CHUNK = 256
IN_TM, IN_TN = 2048, 1024
FOX_TQ = 256
FOX_TK = 256
FOX_PAIR = 8
LOG2E = 1.4426950408889634
MERGE_TM = 512
VMEM_LIMIT = 56 * 1024 * 1024


def _sigmoid(x):
    return 1.0 / (1.0 + jnp.exp(-x))


def _silu(x):
    return x * _sigmoid(x)


def _in_proj_kernel(x_ref, g_ref, w_ref, wg_ref, proj_ref, gate_ref, h_scr):
    @pl.when(pl.program_id(1) == 0)
    def _():
        x = x_ref[...]
        r = lax.rsqrt(jnp.mean(x * x, axis=-1, keepdims=True) + EPS)
        h = ((x * r) * g_ref[...]).astype(jnp.bfloat16)
        h_scr[...] = h
        gate_ref[...] = jnp.dot(h, wg_ref[...], preferred_element_type=jnp.float32)

    proj_ref[...] = jnp.dot(h_scr[...], w_ref[...],
                            preferred_element_type=jnp.float32).astype(jnp.bfloat16)


def _in_proj(x2, norm_g, w_main, w_gate, layer):
    m = x2.shape[0]
    return pl.pallas_call(
        _in_proj_kernel,
        grid=(m // IN_TM, N_MAIN // IN_TN),
        in_specs=[
            pl.BlockSpec((IN_TM, D_MODEL), lambda i, j: (i, 0)),
            pl.BlockSpec((1, D_MODEL), lambda i, j: (0, 0)),
            pl.BlockSpec((None, D_MODEL, IN_TN), lambda i, j: (layer, 0, j)),
            pl.BlockSpec((None, D_MODEL, LANES), lambda i, j: (layer, 0, 0)),
        ],
        out_specs=[
            pl.BlockSpec((IN_TM, IN_TN), lambda i, j: (i, j)),
            pl.BlockSpec((IN_TM, LANES), lambda i, j: (i, 0)),
        ],
        out_shape=[
            jax.ShapeDtypeStruct((m, N_MAIN), jnp.bfloat16),
            jax.ShapeDtypeStruct((m, LANES), jnp.float32),
        ],
        scratch_shapes=[pltpu.VMEM((IN_TM, D_MODEL), jnp.bfloat16)],
        compiler_params=pltpu.CompilerParams(
            dimension_semantics=("arbitrary", "arbitrary"), vmem_limit_bytes=VMEM_LIMIT),
        name="in_proj",
    )(x2, norm_g, w_main, w_gate)


def _gates_kernel(g_ref, bias_ref, col_ref, row_ref, carry_scr):
    @pl.when(pl.program_id(1) == 0)
    def _():
        carry_scr[...] = jnp.zeros_like(carry_scr)

    pre = g_ref[...] + bias_ref[...]
    ls = jnp.minimum(pre, 0.0) - jnp.log1p(jnp.exp(-jnp.abs(pre)))
    row = lax.broadcasted_iota(jnp.int32, ls.shape, 0)
    lane = lax.broadcasted_iota(jnp.int32, ls.shape, 1)
    cs = ls
    k = 1
    while k < ML_CHUNK:
        cs = cs + jnp.where(row >= k, pltpu.roll(cs, k, 0), 0.0)
        k *= 2
    run = cs + carry_scr[...]
    carry_scr[...] = run[ML_CHUNK - 1:ML_CHUNK, :]
    out = jnp.where(lane < G_AF, pre, jnp.where(lane < G_BF, cs, run))
    row_ref[0] = out.T[0:16, :]
    f2 = run * LOG2E
    hi = f2.astype(jnp.bfloat16).astype(jnp.float32)
    r1 = f2 - hi
    mid = r1.astype(jnp.bfloat16).astype(jnp.float32)
    lo = r1 - mid
    pieces = jnp.where(lane < G_FMID, pltpu.roll(hi, G_FHI - G_BF, 1),
                       jnp.where(lane < G_FLO, pltpu.roll(mid, G_FMID - G_BF, 1),
                                 jnp.where(lane < G_FLO + FOX_HEADS, pltpu.roll(lo, G_FLO - G_BF, 1),
                                           jnp.where(lane == G_ONE, 1.0, 0.0))))
    col_ref[...] = jnp.where(lane < G_FHI, out, pieces)


def _gates(gate_pre, gate_bias, batch, seq):
    m = gate_pre.shape[0]
    nc = seq // ML_CHUNK
    return pl.pallas_call(
        _gates_kernel,
        grid=(batch, nc),
        in_specs=[
            pl.BlockSpec((ML_CHUNK, LANES), lambda b, c: (b * nc + c, 0)),
            pl.BlockSpec((1, LANES), lambda b, c: (0, 0)),
        ],
        out_specs=[
            pl.BlockSpec((ML_CHUNK, LANES), lambda b, c: (b * nc + c, 0)),
            pl.BlockSpec((1, 16, ML_CHUNK), lambda b, c: (b, 0, c)),
        ],
        out_shape=[
            jax.ShapeDtypeStruct((m, LANES), jnp.float32),
            jax.ShapeDtypeStruct((batch, 16, seq), jnp.float32),
        ],
        scratch_shapes=[pltpu.VMEM((1, LANES), jnp.float32)],
        compiler_params=pltpu.CompilerParams(dimension_semantics=("arbitrary", "arbitrary")),
        name="gates",
    )(gate_pre, gate_bias)


def _mlstm_kernel(aq_ref, ak_ref, av_ref, ao_ref, az_ref, cw_ref, gc_ref, gr_ref, ng_ref,
                  y_ref, qbuf, kbuf, ct_scr, m_scr):
    L = ML_CHUNK
    dh = ML_HEAD_DIM

    @pl.when(pl.program_id(1) == 0)
    def _():
        qbuf[0:CONV_HALO, :] = jnp.zeros((CONV_HALO, D_MODEL), jnp.float32)
        kbuf[0:CONV_HALO, :] = jnp.zeros((CONV_HALO, D_MODEL), jnp.float32)
        ct_scr[...] = jnp.zeros_like(ct_scr)
        m_scr[...] = jnp.zeros_like(m_scr)

    qbuf[CONV_HALO:, :] = aq_ref[...].astype(jnp.float32)
    kbuf[CONV_HALO:, :] = ak_ref[...].astype(jnp.float32)

    def conv(buf, w):
        base = CONV_HALO - (CONV_WIDTH - 1)
        acc = buf[base:base + L, :] * w[0:1, :]
        for kk in range(1, CONV_WIDTH):
            acc = acc + buf[base + kk:base + kk + L, :] * w[kk:kk + 1, :]
        return acc

    cw = cw_ref[...]
    qc = (_silu(conv(qbuf, cw[:, :D_MODEL])) * (dh ** -0.5)).astype(jnp.bfloat16)
    kc = _silu(conv(kbuf, cw[:, D_MODEL:]))
    qbuf[0:CONV_HALO, :] = qbuf[L:L + CONV_HALO, :]
    kbuf[0:CONV_HALO, :] = kbuf[L:L + CONV_HALO, :]

    gcol = gc_ref[...]
    grow = gr_ref[0]
    t_idx = lax.broadcasted_iota(jnp.int32, (L, L), 0)
    s_idx = lax.broadcasted_iota(jnp.int32, (L, L), 1)
    causal = s_idx <= t_idx
    ones_col = (lax.broadcasted_iota(jnp.int32, (L, LANES), 1) == 0).astype(jnp.bfloat16)

    for h in range(ML_HEADS):
        sl = slice(h * dh, (h + 1) * dh)
        q = qc[:, sl]
        kf = kc[:, sl]
        kb = kf.astype(jnp.bfloat16)
        v_aug = jnp.concatenate([av_ref[:, sl], ones_col], axis=1)
        i_col = gcol[:, G_AI + h:G_AI + h + 1]
        b_col = gcol[:, G_AF + h:G_AF + h + 1]
        i_row = grow[G_AI + h:G_AI + h + 1, :]
        b_row = grow[G_AF + h:G_AF + h + 1, :]
        m_prev = m_scr[h:h + 1, 0:1]

        d = jnp.where(causal, b_col + (i_row - b_row), NEG)
        inter = b_col + m_prev
        m_t = jnp.maximum(inter, jnp.max(d, axis=-1, keepdims=True))
        w_inter = jnp.exp(inter - m_t)
        e = jnp.exp(d - m_t)
        s = lax.dot_general(q, kb, (((1,), (1,)), ((), ())), preferred_element_type=jnp.float32)
        p = (e * s).astype(jnp.bfloat16)
        ct = ct_scr[h]
        num_aug = (w_inter * jnp.dot(q, ct.astype(jnp.bfloat16), preferred_element_type=jnp.float32)
                   + jnp.dot(p, v_aug, preferred_element_type=jnp.float32))
        num = num_aug[:, :dh]
        den = num_aug[:, dh:dh + 1]
        hh = num / jnp.maximum(jnp.abs(den), jnp.exp(-m_t))
        hh = hh * lax.rsqrt(jnp.mean(hh * hh, axis=-1, keepdims=True) + EPS)
        y = (hh * ng_ref[:, sl] * _sigmoid(ao_ref[:, sl].astype(jnp.float32))
             * _silu(az_ref[:, sl].astype(jnp.float32)))
        y_ref[:, sl] = y.astype(jnp.bfloat16)

        b_last = b_col[L - 1:L, :]
        dec = b_last - b_col + i_col
        m_new = jnp.maximum(b_last + m_prev, jnp.max(dec, axis=0, keepdims=True))
        w_s = jnp.exp(dec - m_new)
        w_old = jnp.exp(b_last + m_prev - m_new)
        vw = (v_aug.astype(jnp.float32) * w_s).astype(jnp.bfloat16)
        upd = jnp.dot(kf.T.astype(jnp.bfloat16), vw, preferred_element_type=jnp.float32)
        ct_scr[h] = w_old * ct + upd
        m_scr[h:h + 1, :] = jnp.broadcast_to(m_new, (1, LANES))


def _mlstm(proj, conv_w, gcol, grow, ml_norm_g, batch, seq):
    m = proj.shape[0]
    L = ML_CHUNK
    nc = seq // L

    def col(c):
        return pl.BlockSpec((L, D_MODEL), lambda b, i, c=c: (b * nc + i, c))

    return pl.pallas_call(
        _mlstm_kernel,
        grid=(batch, nc),
        in_specs=[
            col(C_AQ), col(C_AK), col(C_AV), col(C_AO), col(C_AZ),
            pl.BlockSpec((CONV_WIDTH, 2 * D_MODEL), lambda b, i: (0, 0)),
            pl.BlockSpec((L, LANES), lambda b, i: (b * nc + i, 0)),
            pl.BlockSpec((1, 16, L), lambda b, i: (b, 0, i)),
            pl.BlockSpec((1, D_MODEL), lambda b, i: (0, 0)),
        ],
        out_specs=pl.BlockSpec((L, D_MODEL), lambda b, i: (b * nc + i, 0)),
        out_shape=jax.ShapeDtypeStruct((m, D_MODEL), jnp.bfloat16),
        scratch_shapes=[
            pltpu.VMEM((CONV_HALO + L, D_MODEL), jnp.float32),
            pltpu.VMEM((CONV_HALO + L, D_MODEL), jnp.float32),
            pltpu.VMEM((ML_HEADS, ML_HEAD_DIM, ML_HEAD_DIM + LANES), jnp.float32),
            pltpu.VMEM((8, LANES), jnp.float32),
        ],
        compiler_params=pltpu.CompilerParams(
            dimension_semantics=("arbitrary", "arbitrary"), vmem_limit_bytes=VMEM_LIMIT),
        name="mlstm",
    )(proj, proj, proj, proj, proj, conv_w, gcol, grow, ml_norm_g)


def _fox_kernel(q_ref, k_ref, v_ref, z_ref, g_ref, y_ref, qa_scr, ka_scr, vt_scr, st_scr, *, seq):
    tq, tk = FOX_TQ, FOX_TK
    h = pl.program_id(1)
    i = pl.program_id(2)
    lane1 = lax.broadcasted_iota(jnp.int32, (1, LANES), 1)

    @pl.when(i == 0)
    def _():
        src = lax.broadcasted_iota(jnp.int32, (LANES, LANES), 0)
        dst = lax.broadcasted_iota(jnp.int32, (LANES, LANES), 1)
        piece = jnp.where(src == G_FHI + h, 0, jnp.where(src == G_FMID + h, 1,
                                                         jnp.where(src == G_FLO + h, 2, -1)))
        is_piece = piece >= 0
        one = src == G_ONE
        pq = ((is_piece & ((dst == piece) | (dst == 6))) | (one & (dst >= 3) & (dst < 6)))
        pk = (one & (dst < 3)).astype(jnp.float32) - (is_piece & (dst == piece + 3)).astype(jnp.float32)
        pq = pq.astype(jnp.float32).astype(jnp.bfloat16)
        pk = pk.astype(jnp.bfloat16)

        def build(c, carry):
            kmax, qn_vec, g_vec, fe_vec = carry
            r0 = pl.multiple_of(c * tk, tk)
            g = g_ref[pl.ds(r0, tk), :].astype(jnp.bfloat16)
            qx = jnp.dot(g, pq, preferred_element_type=jnp.float32)
            kx = jnp.dot(g, pk, preferred_element_type=jnp.float32)
            f = qx[:, 6:7]
            qs = q_ref[pl.ds(r0, tk), :]
            kk = k_ref[pl.ds(r0, tk), :]
            qa_scr[pl.ds(r0, tk), :] = jnp.concatenate([qs, qx.astype(jnp.bfloat16)], axis=1)
            ka_scr[pl.ds(r0, tk), :] = jnp.concatenate([kk, kx.astype(jnp.bfloat16)], axis=1)
            vt_scr[c] = v_ref[pl.ds(r0, tk), :].astype(jnp.float32).T.astype(jnp.bfloat16)

            qf = qs.astype(jnp.float32)
            kf = kk.astype(jnp.float32)
            qn = jnp.sum(qf * qf, axis=1, keepdims=True)
            kn = jnp.sum(kf * kf, axis=1, keepdims=True)
            diag = jnp.sum(qf * kf, axis=1, keepdims=True)
            here = lane1 == c
            kmax = jnp.maximum(kmax, jnp.max(kn, axis=0, keepdims=True))
            qn_vec = jnp.where(here, jnp.max(qn, axis=0, keepdims=True), qn_vec)
            g_vec = jnp.where(here, jnp.max(f - diag, axis=0, keepdims=True), g_vec)
            fe_vec = jnp.where(here, f[tk - 1:tk, :], fe_vec)
            return kmax, qn_vec, g_vec, fe_vec

        zrow = jnp.zeros((1, LANES), jnp.float32)
        kmax, qn_vec, g_vec, fe_vec = lax.fori_loop(
            0, seq // tk, build, (jnp.zeros((1, 1), jnp.float32), zrow, zrow, zrow))
        st_scr[0:1, :] = jnp.sqrt(qn_vec * kmax) + g_vec
        st_scr[1:2, :] = fe_vec

    def back_blocks(blk):
        bound = jnp.sum(jnp.where(lane1 == blk, st_scr[0:1, :], 0.0), axis=1, keepdims=True)
        needed = (lane1 < blk) & jnp.logical_not(st_scr[1:2, :] >= bound + SKIP_MARGIN * LOG2E)
        return blk - jnp.min(jnp.where(needed, lane1, blk))

    blks = [i * FOX_PAIR + u for u in range(FOX_PAIR)]
    n_back = functools.reduce(jnp.maximum, [back_blocks(b) for b in blks])
    qas = [qa_scr[pl.ds(pl.multiple_of(b * tq, tq), tq), :] for b in blks]

    def attend(carry, t, mode):
        m, l, acc = carry
        kbs, parts = [], []
        for u in range(FOX_PAIR):
            kb = jnp.maximum(blks[u] - t, 0)
            k0 = pl.multiple_of(kb * tk, tk)
            s = lax.dot_general(ka_scr[pl.ds(k0, tk), :], qas[u], (((1,), (1,)), ((), ())),
                                preferred_element_type=jnp.float32)
            if mode == "diagonal":
                key_idx = lax.broadcasted_iota(jnp.int32, (tk, tq), 0)
                qry_idx = lax.broadcasted_iota(jnp.int32, (tk, tq), 1)
                s = jnp.where(key_idx <= qry_idx, s, NEG)
            elif mode == "edge" and u < FOX_PAIR - 1:
                s = jnp.where(t <= blks[u], s, NEG)
            kbs.append(kb)
            parts.append(s)
        s = jnp.concatenate(parts, axis=1)
        m_new = jnp.maximum(m, jnp.max(s, axis=0, keepdims=True))
        alpha = jnp.exp2(m - m_new)
        p = jnp.exp2(s - m_new)
        l = alpha * l + jnp.sum(p, axis=0, keepdims=True)
        p = p.astype(jnp.bfloat16)
        pv = jnp.concatenate(
            [jnp.dot(vt_scr[kbs[u]], p[:, u * tq:(u + 1) * tq], preferred_element_type=jnp.float32)
             for u in range(FOX_PAIR)], axis=1)
        return m_new, l, alpha * acc + pv

    rows = FOX_PAIR * tq
    init = (jnp.full((1, rows), NEG, jnp.float32), jnp.zeros((1, rows), jnp.float32),
            jnp.zeros((FOX_HEAD_DIM, rows), jnp.float32))
    carry = attend(init, 0, "diagonal")
    n_plain = jnp.minimum(n_back, blks[0])
    carry = lax.fori_loop(1, n_plain + 1, lambda t, c: attend(c, t, "plain"), carry)
    _, l, acc = lax.fori_loop(n_plain + 1, n_back + 1, lambda t, c: attend(c, t, "edge"), carry)
    y_ref[...] = ((acc / l).T * _silu(z_ref[...].astype(jnp.float32))).astype(jnp.bfloat16)


def _fox(proj, gcol, batch, seq):
    m = proj.shape[0]
    rows = FOX_PAIR * FOX_TQ
    nq = seq // rows
    assert FOX_TQ == FOX_TK and seq % rows == 0 and seq // FOX_TQ <= LANES
    hb = D_MODEL // FOX_HEAD_DIM

    def head(c):
        return pl.BlockSpec((seq, FOX_HEAD_DIM), lambda b, h, i, c=c: (b, c * hb + h))

    return pl.pallas_call(
        functools.partial(_fox_kernel, seq=seq),
        grid=(batch, FOX_HEADS, nq),
        in_specs=[
            head(C_BQ), head(C_BK), head(C_BV),
            pl.BlockSpec((rows, FOX_HEAD_DIM), lambda b, h, i: (b * nq + i, C_BZ * hb + h)),
            pl.BlockSpec((seq, LANES), lambda b, h, i: (b, 0)),
        ],
        out_specs=pl.BlockSpec((rows, FOX_HEAD_DIM), lambda b, h, i: (b * nq + i, h)),
        out_shape=jax.ShapeDtypeStruct((m, D_MODEL), jnp.bfloat16),
        scratch_shapes=[
            pltpu.VMEM((seq, 2 * FOX_HEAD_DIM), jnp.bfloat16),
            pltpu.VMEM((seq, 2 * FOX_HEAD_DIM), jnp.bfloat16),
            pltpu.VMEM((seq // FOX_TK, FOX_HEAD_DIM, FOX_TK), jnp.bfloat16),
            pltpu.VMEM((8, LANES), jnp.float32),
        ],
        compiler_params=pltpu.CompilerParams(
            dimension_semantics=("arbitrary", "arbitrary", "arbitrary"), vmem_limit_bytes=VMEM_LIMIT),
        name="fox",
    )(proj, proj, proj, proj, gcol)


def _merge_kernel(ya_ref, yb_ref, cu_ref, halo_ref, cz_ref, gates_ref, x_ref, pw_ref, ps_ref,
                  wb_ref, wo_ref, fg_ref, o_ref, *, seq, final):
    tm = MERGE_TM
    gd = POOL_GROUP_DIM
    i = pl.program_id(0)
    t0 = (i * tm) % seq
    pos = t0 + lax.broadcasted_iota(jnp.int32, (tm, 1), 0)
    halo = jnp.where(t0 == 0, 0.0, halo_ref[...].astype(jnp.float32))
    u = cu_ref[...].astype(jnp.float32)
    ext = jnp.concatenate([halo, u], axis=0)

    ys = []
    for g, win in enumerate(POOL_WINDOWS):
        sl = slice(g * gd, (g + 1) * gd)
        acc = ext[:, sl]
        step = 1
        while step < win:
            acc = acc + pltpu.roll(acc, step, 0)
            step *= 2
        cnt = jnp.minimum(pos + 1, win).astype(jnp.float32)
        d = acc[HALO:, :] / cnt - u[:, sl]
        ys.append(jnp.dot(d.astype(jnp.bfloat16), pw_ref[g], preferred_element_type=jnp.float32))
    yc = jnp.concatenate(ys, axis=1) * ps_ref[...]
    yc = yc * _silu(cz_ref[...].astype(jnp.float32))

    merged = jnp.zeros((tm, D_MODEL), jnp.float32)
    for n, y in enumerate((ya_ref[...], yb_ref[...], yc.astype(jnp.bfloat16))):
        yb = jnp.dot(y, wb_ref[n], preferred_element_type=jnp.float32)
        gate = _sigmoid(gates_ref[:, n * D_MODEL:(n + 1) * D_MODEL].astype(jnp.float32))
        merged = merged + gate * yb
    out = x_ref[...] + jnp.dot(merged.astype(jnp.bfloat16), wo_ref[...],
                               preferred_element_type=jnp.float32)
    if final:
        r = lax.rsqrt(jnp.mean(out * out, axis=-1, keepdims=True) + EPS)
        out = (out * r) * fg_ref[...]
    o_ref[...] = out


def _merge(ya, yb, proj, x2, pool_w, pool_scale, w_branch, w_out, final_g, seq, final, layer):
    m = x2.shape[0]
    tm = MERGE_TM
    hpt = tm // HALO

    def rowblk(c, width=1):
        return pl.BlockSpec((tm, width * D_MODEL), lambda i, c=c: (i, c))

    const2 = lambda i: (0, 0)
    layer3 = lambda i: (layer, 0, 0)
    layer4 = lambda i: (layer, 0, 0, 0)
    return pl.pallas_call(
        functools.partial(_merge_kernel, seq=seq, final=final),
        grid=(m // tm,),
        in_specs=[
            rowblk(0), rowblk(0), rowblk(C_CU),
            pl.BlockSpec((HALO, D_MODEL), lambda i: (jnp.maximum(i * hpt - 1, 0), C_CU)),
            rowblk(C_CZ),
            pl.BlockSpec((tm, N_BRANCH * D_MODEL), lambda i: (i, C_GATES // N_BRANCH)),
            rowblk(0),
            pl.BlockSpec((None, POOL_GROUPS, POOL_GROUP_DIM, POOL_GROUP_DIM), layer4),
            pl.BlockSpec((1, D_MODEL), const2),
            pl.BlockSpec((None, N_BRANCH, D_MODEL, D_MODEL), layer4),
            pl.BlockSpec((None, D_MODEL, D_MODEL), layer3),
            pl.BlockSpec((1, D_MODEL), const2),
        ],
        out_specs=pl.BlockSpec((tm, D_MODEL), lambda i: (i, 0)),
        out_shape=jax.ShapeDtypeStruct((m, D_MODEL), jnp.float32),
        compiler_params=pltpu.CompilerParams(
            dimension_semantics=("arbitrary",), vmem_limit_bytes=VMEM_LIMIT),
        name="merge",
    )(ya, yb, proj, proj, proj, proj, x2, pool_w, pool_scale, w_branch, w_out, final_g)


O_AI = 5 * D_MODEL
O_BQ = O_AI + 2 * ML_HEADS
O_BF = O_BQ + 4 * D_MODEL
O_CU = O_BF + FOX_HEADS
O_G = O_CU + 2 * D_MODEL
N_IN = O_G + N_BRANCH * D_MODEL
W_SEGMENTS = ((O_G, N_BRANCH), (0, 5), (O_BQ, 4), (O_CU, 2))
W_PREP_ROWS = 128


def _w_prep_kernel(wt_ref, main_ref, gate_ref):
    dst = 0
    for src, nblk in W_SEGMENTS:
        for n in range(nblk):
            a = src + n * D_MODEL
            blk = wt_ref[a:a + D_MODEL, :].T
            if dst == C_BQ * D_MODEL:
                blk = blk * (FOX_HEAD_DIM ** -0.5 * LOG2E)
            main_ref[:, dst:dst + D_MODEL] = blk.astype(jnp.bfloat16)
            dst += D_MODEL
    n_gate = 2 * ML_HEADS + FOX_HEADS
    gate = jnp.concatenate([wt_ref[O_AI:O_BQ, :], wt_ref[O_BF:O_CU, :],
                            jnp.zeros((LANES - n_gate, W_PREP_ROWS), jnp.float32)], axis=0)
    gate_ref[...] = gate.T.astype(jnp.bfloat16)


def _split_w_in(w_in):
    depth = w_in.shape[0]
    assert w_in.shape[1:] == (D_MODEL, N_IN)
    return pl.pallas_call(
        _w_prep_kernel,
        grid=(depth, D_MODEL // W_PREP_ROWS),
        in_specs=[pl.BlockSpec((None, N_IN, W_PREP_ROWS), lambda l, r: (l, 0, r))],
        out_specs=[
            pl.BlockSpec((None, W_PREP_ROWS, N_MAIN), lambda l, r: (l, r, 0)),
            pl.BlockSpec((None, W_PREP_ROWS, LANES), lambda l, r: (l, r, 0)),
        ],
        out_shape=[
            jax.ShapeDtypeStruct((depth, D_MODEL, N_MAIN), jnp.bfloat16),
            jax.ShapeDtypeStruct((depth, D_MODEL, LANES), jnp.bfloat16),
        ],
        compiler_params=pltpu.CompilerParams(
            dimension_semantics=("arbitrary", "arbitrary"), vmem_limit_bytes=VMEM_LIMIT),
        name="w_prep",
    )(jnp.swapaxes(w_in, 1, 2))


def kernel(x, norm_g, w_in, conv_w, ml_bi, ml_bf, ml_norm_g, fox_bf, pool_w, pool_scale, w_branch,
           w_out, final_g):
    batch, seq, d = x.shape
    depth = norm_g.shape[0]
    assert d == D_MODEL and seq % max(IN_TM, FOX_TQ, MERGE_TM, ML_CHUNK) == 0
    x2 = x.reshape(batch * seq, d)
    w_main, w_gate = _split_w_in(w_in)
    pool_wb = pool_w.astype(jnp.bfloat16)
    w_branchb = w_branch.astype(jnp.bfloat16)
    w_outb = w_out.astype(jnp.bfloat16)
    for l in range(depth):
        gate_bias = jnp.pad(jnp.concatenate([ml_bi[l], ml_bf[l], fox_bf[l]]),
                            (0, LANES - 2 * ML_HEADS - FOX_HEADS)).reshape(1, LANES)
        proj, gate_pre = _in_proj(x2, norm_g[l].reshape(1, d), w_main, w_gate, l)
        gcol, grow = _gates(gate_pre, gate_bias, batch, seq)
        ya = _mlstm(proj, conv_w[l], gcol, grow, ml_norm_g[l].reshape(1, d), batch, seq)
        yb = _fox(proj, gcol, batch, seq)
        x2 = _merge(ya, yb, proj, x2, pool_wb, pool_scale[l].reshape(1, d), w_branchb, w_outb,
                    final_g.reshape(1, d), seq, final=(l == depth - 1), layer=l)
    return x2.reshape(batch, seq, d)
```

```python
import functools

import jax
import jax.numpy as jnp
from jax import lax
from jax.experimental import pallas as pl
from jax.experimental.pallas import tpu as pltpu

D_MODEL = 1024
ML_HEADS = 4
ML_HEAD_DIM = 256
CONV_WIDTH = 4
FOX_HEADS = 8
FOX_HEAD_DIM = 128
POOL_GROUPS = 4
POOL_GROUP_DIM = 256
POOL_WINDOWS = (2, 4, 8, 16)
N_BRANCH = 3
EPS = 1e-6

LANES = 128
HALO = 16
CONV_HALO = 8
NEG = -1e30
SKIP_MARGIN = 32.0

C_GATES = 0
C_AQ, C_AK, C_AV, C_AO, C_AZ, C_BQ, C_BK, C_BV, C_BZ, C_CU, C_CZ = range(N_BRANCH, N_BRANCH + 11)
N_MAIN = 14 * D_MODEL
G_AI, G_AF, G_BF = 0, 4, 8
G_FHI, G_FMID, G_FLO, G_ONE = 16, 24, 32, 127

ML_CHUNK = 256
IN_TM, IN_TN = 2048, 1024
FOX_TQ = 256
FOX_TK = 256
FOX_PAIR = 16
GATES_ROWS = 1024
LOG2E = 1.4426950408889634
MERGE_TM = 512
VMEM_LIMIT = 56 * 1024 * 1024


def _sigmoid(x):
    return 1.0 / (1.0 + jnp.exp(-x))


def _silu(x):
    return x * _sigmoid(x)


def _in_proj_kernel(x_ref, g_ref, w_ref, wg_ref, proj_ref, gate_ref, h_scr):
    @pl.when(pl.program_id(1) == 0)
    def _():
        x = x_ref[...]
        r = lax.rsqrt(jnp.mean(x * x, axis=-1, keepdims=True) + EPS)
        h = ((x * r) * g_ref[...]).astype(jnp.bfloat16)
        h_scr[...] = h
        gate_ref[...] = jnp.dot(h, wg_ref[...], preferred_element_type=jnp.float32)

    proj_ref[...] = jnp.dot(h_scr[...], w_ref[...],
                            preferred_element_type=jnp.float32).astype(jnp.bfloat16)


def _in_proj(x2, norm_g, w_main, w_gate, layer):
    m = x2.shape[0]
    return pl.pallas_call(
        _in_proj_kernel,
        grid=(m // IN_TM, N_MAIN // IN_TN),
        in_specs=[
            pl.BlockSpec((IN_TM, D_MODEL), lambda i, j: (i, 0)),
            pl.BlockSpec((1, D_MODEL), lambda i, j: (0, 0)),
            pl.BlockSpec((None, D_MODEL, IN_TN), lambda i, j: (layer, 0, j)),
            pl.BlockSpec((None, D_MODEL, LANES), lambda i, j: (layer, 0, 0)),
        ],
        out_specs=[
            pl.BlockSpec((IN_TM, IN_TN), lambda i, j: (i, j)),
            pl.BlockSpec((IN_TM, LANES), lambda i, j: (i, 0)),
        ],
        out_shape=[
            jax.ShapeDtypeStruct((m, N_MAIN), jnp.bfloat16),
            jax.ShapeDtypeStruct((m, LANES), jnp.float32),
        ],
        scratch_shapes=[pltpu.VMEM((IN_TM, D_MODEL), jnp.bfloat16)],
        compiler_params=pltpu.CompilerParams(
            dimension_semantics=("arbitrary", "arbitrary"), vmem_limit_bytes=VMEM_LIMIT),
        name="in_proj",
    )(x2, norm_g, w_main, w_gate)


def _gates_kernel(g_ref, bias_ref, col_ref, row_ref, carry_scr):
    @pl.when(pl.program_id(1) == 0)
    def _():
        carry_scr[...] = jnp.zeros_like(carry_scr)

    row = lax.broadcasted_iota(jnp.int32, (ML_CHUNK, LANES), 0)
    lane = lax.broadcasted_iota(jnp.int32, (ML_CHUNK, LANES), 1)
    carry = carry_scr[...]
    for c in range(GATES_ROWS // ML_CHUNK):
        rs = slice(c * ML_CHUNK, (c + 1) * ML_CHUNK)
        pre = g_ref[rs, :] + bias_ref[...]
        ls = jnp.minimum(pre, 0.0) - jnp.log1p(jnp.exp(-jnp.abs(pre)))
        cs = ls
        k = 1
        while k < ML_CHUNK:
            cs = cs + jnp.where(row >= k, pltpu.roll(cs, k, 0), 0.0)
            k *= 2
        run = cs + carry
        carry = run[ML_CHUNK - 1:ML_CHUNK, :]
        out = jnp.where(lane < G_AF, pre, jnp.where(lane < G_BF, cs, run))
        row_ref[0, :, rs] = out.T[0:16, :]
        f2 = run * LOG2E
        hi = f2.astype(jnp.bfloat16).astype(jnp.float32)
        r1 = f2 - hi
        mid = r1.astype(jnp.bfloat16).astype(jnp.float32)
        lo = r1 - mid
        pieces = jnp.where(lane < G_FMID, pltpu.roll(hi, G_FHI - G_BF, 1),
                           jnp.where(lane < G_FLO, pltpu.roll(mid, G_FMID - G_BF, 1),
                                     jnp.where(lane < G_FLO + FOX_HEADS, pltpu.roll(lo, G_FLO - G_BF, 1),
                                               jnp.where(lane == G_ONE, 1.0, 0.0))))
        col_ref[rs, :] = jnp.where(lane < G_FHI, out, pieces)
    carry_scr[...] = carry


def _gates(gate_pre, gate_bias, batch, seq):
    m = gate_pre.shape[0]
    nc = seq // GATES_ROWS
    return pl.pallas_call(
        _gates_kernel,
        grid=(batch, nc),
        in_specs=[
            pl.BlockSpec((GATES_ROWS, LANES), lambda b, c: (b * nc + c, 0)),
            pl.BlockSpec((1, LANES), lambda b, c: (0, 0)),
        ],
        out_specs=[
            pl.BlockSpec((GATES_ROWS, LANES), lambda b, c: (b * nc + c, 0)),
            pl.BlockSpec((1, 16, GATES_ROWS), lambda b, c: (b, 0, c)),
        ],
        out_shape=[
            jax.ShapeDtypeStruct((m, LANES), jnp.float32),
            jax.ShapeDtypeStruct((batch, 16, seq), jnp.float32),
        ],
        scratch_shapes=[pltpu.VMEM((1, LANES), jnp.float32)],
        compiler_params=pltpu.CompilerParams(dimension_semantics=("arbitrary", "arbitrary")),
        name="gates",
    )(gate_pre, gate_bias)


def _mlstm_kernel(aq_ref, ak_ref, av_ref, ao_ref, az_ref, cw_ref, gc_ref, gr_ref, ng_ref,
                  y_ref, qbuf, kbuf, ct_scr, m_scr):
    L = ML_CHUNK
    dh = ML_HEAD_DIM

    @pl.when(pl.program_id(1) == 0)
    def _():
        qbuf[0:CONV_HALO, :] = jnp.zeros((CONV_HALO, D_MODEL), jnp.float32)
        kbuf[0:CONV_HALO, :] = jnp.zeros((CONV_HALO, D_MODEL), jnp.float32)
        ct_scr[...] = jnp.zeros_like(ct_scr)
        m_scr[...] = jnp.zeros_like(m_scr)

    qbuf[CONV_HALO:, :] = aq_ref[...].astype(jnp.float32)
    kbuf[CONV_HALO:, :] = ak_ref[...].astype(jnp.float32)

    def conv(buf, w):
        base = CONV_HALO - (CONV_WIDTH - 1)
        acc = buf[base:base + L, :] * w[0:1, :]
        for kk in range(1, CONV_WIDTH):
            acc = acc + buf[base + kk:base + kk + L, :] * w[kk:kk + 1, :]
        return acc

    cw = cw_ref[...]
    qc = (_silu(conv(qbuf, cw[:, :D_MODEL])) * (dh ** -0.5)).astype(jnp.bfloat16)
    kc = _silu(conv(kbuf, cw[:, D_MODEL:]))
    qbuf[0:CONV_HALO, :] = qbuf[L:L + CONV_HALO, :]
    kbuf[0:CONV_HALO, :] = kbuf[L:L + CONV_HALO, :]

    gcol = gc_ref[...]
    grow = gr_ref[0]
    t_idx = lax.broadcasted_iota(jnp.int32, (L, L), 0)
    s_idx = lax.broadcasted_iota(jnp.int32, (L, L), 1)
    causal = s_idx <= t_idx
    ones_col = (lax.broadcasted_iota(jnp.int32, (L, LANES), 1) == 0).astype(jnp.bfloat16)

    for h in range(ML_HEADS):
        sl = slice(h * dh, (h + 1) * dh)
        q = qc[:, sl]
        kf = kc[:, sl]
        kb = kf.astype(jnp.bfloat16)
        v_aug = jnp.concatenate([av_ref[:, sl], ones_col], axis=1)
        i_col = gcol[:, G_AI + h:G_AI + h + 1]
        b_col = gcol[:, G_AF + h:G_AF + h + 1]
        i_row = grow[G_AI + h:G_AI + h + 1, :]
        b_row = grow[G_AF + h:G_AF + h + 1, :]
        m_prev = m_scr[h:h + 1, 0:1]

        d = jnp.where(causal, b_col + (i_row - b_row), NEG)
        inter = b_col + m_prev
        m_t = jnp.maximum(inter, jnp.max(d, axis=-1, keepdims=True))
        w_inter = jnp.exp(inter - m_t)
        e = jnp.exp(d - m_t)
        s = lax.dot_general(q, kb, (((1,), (1,)), ((), ())), preferred_element_type=jnp.float32)
        p = (e * s).astype(jnp.bfloat16)
        ct = ct_scr[h]
        num_aug = (w_inter * jnp.dot(q, ct.astype(jnp.bfloat16), preferred_element_type=jnp.float32)
                   + jnp.dot(p, v_aug, preferred_element_type=jnp.float32))
        num = num_aug[:, :dh]
        den = num_aug[:, dh:dh + 1]
        hh = num / jnp.maximum(jnp.abs(den), jnp.exp(-m_t))
        hh = hh * lax.rsqrt(jnp.mean(hh * hh, axis=-1, keepdims=True) + EPS)
        y = (hh * ng_ref[:, sl] * _sigmoid(ao_ref[:, sl].astype(jnp.float32))
             * _silu(az_ref[:, sl].astype(jnp.float32)))
        y_ref[:, sl] = y.astype(jnp.bfloat16)

        b_last = b_col[L - 1:L, :]
        dec = b_last - b_col + i_col
        m_new = jnp.maximum(b_last + m_prev, jnp.max(dec, axis=0, keepdims=True))
        w_s = jnp.exp(dec - m_new)
        w_old = jnp.exp(b_last + m_prev - m_new)
        vw = (v_aug.astype(jnp.float32) * w_s).astype(jnp.bfloat16)
        upd = jnp.dot(kf.T.astype(jnp.bfloat16), vw, preferred_element_type=jnp.float32)
        ct_scr[h] = w_old * ct + upd
        m_scr[h:h + 1, :] = jnp.broadcast_to(m_new, (1, LANES))


def _mlstm(proj, conv_w, gcol, grow, ml_norm_g, batch, seq):
    m = proj.shape[0]
    L = ML_CHUNK
    nc = seq // L

    def col(c):
        return pl.BlockSpec((L, D_MODEL), lambda b, i, c=c: (b * nc + i, c))

    return pl.pallas_call(
        _mlstm_kernel,
        grid=(batch, nc),
        in_specs=[
            col(C_AQ), col(C_AK), col(C_AV), col(C_AO), col(C_AZ),
            pl.BlockSpec((CONV_WIDTH, 2 * D_MODEL), lambda b, i: (0, 0)),
            pl.BlockSpec((L, LANES), lambda b, i: (b * nc + i, 0)),
            pl.BlockSpec((1, 16, L), lambda b, i: (b, 0, i)),
            pl.BlockSpec((1, D_MODEL), lambda b, i: (0, 0)),
        ],
        out_specs=pl.BlockSpec((L, D_MODEL), lambda b, i: (b * nc + i, 0)),
        out_shape=jax.ShapeDtypeStruct((m, D_MODEL), jnp.bfloat16),
        scratch_shapes=[
            pltpu.VMEM((CONV_HALO + L, D_MODEL), jnp.float32),
            pltpu.VMEM((CONV_HALO + L, D_MODEL), jnp.float32),
            pltpu.VMEM((ML_HEADS, ML_HEAD_DIM, ML_HEAD_DIM + LANES), jnp.float32),
            pltpu.VMEM((8, LANES), jnp.float32),
        ],
        compiler_params=pltpu.CompilerParams(
            dimension_semantics=("arbitrary", "arbitrary"), vmem_limit_bytes=VMEM_LIMIT),
        name="mlstm",
    )(proj, proj, proj, proj, proj, conv_w, gcol, grow, ml_norm_g)


def _fox_kernel(q_ref, k_ref, v_ref, z_ref, g_ref, y_ref, qa_scr, ka_scr, vt_scr, st_scr, *, seq):
    tq, tk = FOX_TQ, FOX_TK
    h = pl.program_id(1)
    i = pl.program_id(2)
    lane1 = lax.broadcasted_iota(jnp.int32, (1, LANES), 1)

    @pl.when(i == 0)
    def _():
        src = lax.broadcasted_iota(jnp.int32, (LANES, LANES), 0)
        dst = lax.broadcasted_iota(jnp.int32, (LANES, LANES), 1)
        piece = jnp.where(src == G_FHI + h, 0, jnp.where(src == G_FMID + h, 1,
                                                         jnp.where(src == G_FLO + h, 2, -1)))
        is_piece = piece >= 0
        one = src == G_ONE
        pq = ((is_piece & ((dst == piece) | (dst == 6))) | (one & (dst >= 3) & (dst < 6)))
        pk = (one & (dst < 3)).astype(jnp.float32) - (is_piece & (dst == piece + 3)).astype(jnp.float32)
        pq = pq.astype(jnp.float32).astype(jnp.bfloat16)
        pk = pk.astype(jnp.bfloat16)

        def build(c, carry):
            kmax, qn_vec, g_vec, fe_vec = carry
            r0 = pl.multiple_of(c * tk, tk)
            g = g_ref[pl.ds(r0, tk), :].astype(jnp.bfloat16)
            qx = jnp.dot(g, pq, preferred_element_type=jnp.float32)
            kx = jnp.dot(g, pk, preferred_element_type=jnp.float32)
            f = qx[:, 6:7]
            qs = q_ref[pl.ds(r0, tk), :]
            kk = k_ref[pl.ds(r0, tk), :]
            qa_scr[pl.ds(r0, tk), :] = jnp.concatenate([qs, qx.astype(jnp.bfloat16)], axis=1)
            ka_scr[pl.ds(r0, tk), :] = jnp.concatenate([kk, kx.astype(jnp.bfloat16)], axis=1)
            vt_scr[c] = v_ref[pl.ds(r0, tk), :].astype(jnp.float32).T.astype(jnp.bfloat16)

            qf = qs.astype(jnp.float32)
            kf = kk.astype(jnp.float32)
            qn = jnp.sum(qf * qf, axis=1, keepdims=True)
            kn = jnp.sum(kf * kf, axis=1, keepdims=True)
            diag = jnp.sum(qf * kf, axis=1, keepdims=True)
            here = lane1 == c
            kmax = jnp.maximum(kmax, jnp.max(kn, axis=0, keepdims=True))
            qn_vec = jnp.where(here, jnp.max(qn, axis=0, keepdims=True), qn_vec)
            g_vec = jnp.where(here, jnp.max(f - diag, axis=0, keepdims=True), g_vec)
            fe_vec = jnp.where(here, f[tk - 1:tk, :], fe_vec)
            return kmax, qn_vec, g_vec, fe_vec

        zrow = jnp.zeros((1, LANES), jnp.float32)
        kmax, qn_vec, g_vec, fe_vec = lax.fori_loop(
            0, seq // tk, build, (jnp.zeros((1, 1), jnp.float32), zrow, zrow, zrow), unroll=4)
        st_scr[0:1, :] = jnp.sqrt(qn_vec * kmax) + g_vec
        st_scr[1:2, :] = fe_vec

    def back_blocks(blk):
        bound = jnp.sum(jnp.where(lane1 == blk, st_scr[0:1, :], 0.0), axis=1, keepdims=True)
        needed = (lane1 < blk) & jnp.logical_not(st_scr[1:2, :] >= bound + SKIP_MARGIN * LOG2E)
        return blk - jnp.min(jnp.where(needed, lane1, blk))

    blks = [i * FOX_PAIR + u for u in range(FOX_PAIR)]
    n_back = functools.reduce(jnp.maximum, [back_blocks(b) for b in blks])
    qas = [qa_scr[pl.ds(pl.multiple_of(b * tq, tq), tq), :] for b in blks]

    def attend(carry, t, mode):
        m, l, acc = carry
        kbs, parts = [], []
        for u in range(FOX_PAIR):
            kb = jnp.maximum(blks[u] - t, 0)
            k0 = pl.multiple_of(kb * tk, tk)
            s = lax.dot_general(ka_scr[pl.ds(k0, tk), :], qas[u], (((1,), (1,)), ((), ())),
                                preferred_element_type=jnp.float32)
            if mode == "diagonal":
                key_idx = lax.broadcasted_iota(jnp.int32, (tk, tq), 0)
                qry_idx = lax.broadcasted_iota(jnp.int32, (tk, tq), 1)
                s = jnp.where(key_idx <= qry_idx, s, NEG)
            elif mode == "edge" and u < FOX_PAIR - 1:
                s = jnp.where(t <= blks[u], s, NEG)
            kbs.append(kb)
            parts.append(s)
        s = jnp.concatenate(parts, axis=1)
        m_new = jnp.maximum(m, jnp.max(s, axis=0, keepdims=True))
        alpha = jnp.exp2(m - m_new)
        p = jnp.exp2(s - m_new)
        l = alpha * l + jnp.sum(p, axis=0, keepdims=True)
        p = p.astype(jnp.bfloat16)
        pv = jnp.concatenate(
            [jnp.dot(vt_scr[kbs[u]], p[:, u * tq:(u + 1) * tq], preferred_element_type=jnp.float32)
             for u in range(FOX_PAIR)], axis=1)
        return m_new, l, alpha * acc + pv

    rows = FOX_PAIR * tq
    init = (jnp.full((1, rows), NEG, jnp.float32), jnp.zeros((1, rows), jnp.float32),
            jnp.zeros((FOX_HEAD_DIM, rows), jnp.float32))
    carry = attend(init, 0, "diagonal")
    n_plain = jnp.minimum(n_back, blks[0])
    carry = lax.fori_loop(1, n_plain + 1, lambda t, c: attend(c, t, "plain"), carry)
    _, l, acc = lax.fori_loop(n_plain + 1, n_back + 1, lambda t, c: attend(c, t, "edge"), carry)
    y_ref[...] = ((acc / l).T * _silu(z_ref[...].astype(jnp.float32))).astype(jnp.bfloat16)


def _fox(proj, gcol, batch, seq):
    m = proj.shape[0]
    rows = FOX_PAIR * FOX_TQ
    nq = seq // rows
    assert FOX_TQ == FOX_TK and seq % rows == 0 and seq // FOX_TQ <= LANES
    hb = D_MODEL // FOX_HEAD_DIM

    def head(c):
        return pl.BlockSpec((seq, FOX_HEAD_DIM), lambda b, h, i, c=c: (b, c * hb + h))

    return pl.pallas_call(
        functools.partial(_fox_kernel, seq=seq),
        grid=(batch, FOX_HEADS, nq),
        in_specs=[
            head(C_BQ), head(C_BK), head(C_BV),
            pl.BlockSpec((rows, FOX_HEAD_DIM), lambda b, h, i: (b * nq + i, C_BZ * hb + h)),
            pl.BlockSpec((seq, LANES), lambda b, h, i: (b, 0)),
        ],
        out_specs=pl.BlockSpec((rows, FOX_HEAD_DIM), lambda b, h, i: (b * nq + i, h)),
        out_shape=jax.ShapeDtypeStruct((m, D_MODEL), jnp.bfloat16),
        scratch_shapes=[
            pltpu.VMEM((seq, 2 * FOX_HEAD_DIM), jnp.bfloat16),
            pltpu.VMEM((seq, 2 * FOX_HEAD_DIM), jnp.bfloat16),
            pltpu.VMEM((seq // FOX_TK, FOX_HEAD_DIM, FOX_TK), jnp.bfloat16),
            pltpu.VMEM((8, LANES), jnp.float32),
        ],
        compiler_params=pltpu.CompilerParams(
            dimension_semantics=("arbitrary", "arbitrary", "arbitrary"), vmem_limit_bytes=VMEM_LIMIT),
        name="fox",
    )(proj, proj, proj, proj, gcol)


def _merge_kernel(ya_ref, yb_ref, cu_ref, halo_ref, cz_ref, gates_ref, x_ref, pw_ref, ps_ref,
                  wb_ref, wo_ref, fg_ref, o_ref, *, seq, final):
    tm = MERGE_TM
    gd = POOL_GROUP_DIM
    i = pl.program_id(0)
    t0 = (i * tm) % seq
    pos = t0 + lax.broadcasted_iota(jnp.int32, (tm, 1), 0)
    halo = jnp.where(t0 == 0, 0.0, halo_ref[...].astype(jnp.float32))
    u = cu_ref[...].astype(jnp.float32)
    ext = jnp.concatenate([halo, u], axis=0)

    ys = []
    for g, win in enumerate(POOL_WINDOWS):
        sl = slice(g * gd, (g + 1) * gd)
        acc = ext[:, sl]
        step = 1
        while step < win:
            acc = acc + pltpu.roll(acc, step, 0)
            step *= 2
        cnt = jnp.minimum(pos + 1, win).astype(jnp.float32)
        d = acc[HALO:, :] / cnt - u[:, sl]
        ys.append(jnp.dot(d.astype(jnp.bfloat16), pw_ref[g], preferred_element_type=jnp.float32))
    yc = jnp.concatenate(ys, axis=1) * ps_ref[...]
    yc = yc * _silu(cz_ref[...].astype(jnp.float32))

    merged = jnp.zeros((tm, D_MODEL), jnp.float32)
    for n, y in enumerate((ya_ref[...], yb_ref[...], yc.astype(jnp.bfloat16))):
        yb = jnp.dot(y, wb_ref[n], preferred_element_type=jnp.float32)
        gate = _sigmoid(gates_ref[:, n * D_MODEL:(n + 1) * D_MODEL].astype(jnp.float32))
        merged = merged + gate * yb
    out = x_ref[...] + jnp.dot(merged.astype(jnp.bfloat16), wo_ref[...],
                               preferred_element_type=jnp.float32)
    if final:
        r = lax.rsqrt(jnp.mean(out * out, axis=-1, keepdims=True) + EPS)
        out = (out * r) * fg_ref[...]
    o_ref[...] = out


def _merge(ya, yb, proj, x2, pool_w, pool_scale, w_branch, w_out, final_g, seq, final, layer):
    m = x2.shape[0]
    tm = MERGE_TM
    hpt = tm // HALO

    def rowblk(c, width=1):
        return pl.BlockSpec((tm, width * D_MODEL), lambda i, c=c: (i, c))

    const2 = lambda i: (0, 0)
    layer3 = lambda i: (layer, 0, 0)
    layer4 = lambda i: (layer, 0, 0, 0)
    return pl.pallas_call(
        functools.partial(_merge_kernel, seq=seq, final=final),
        grid=(m // tm,),
        in_specs=[
            rowblk(0), rowblk(0), rowblk(C_CU),
            pl.BlockSpec((HALO, D_MODEL), lambda i: (jnp.maximum(i * hpt - 1, 0), C_CU)),
            rowblk(C_CZ),
            pl.BlockSpec((tm, N_BRANCH * D_MODEL), lambda i: (i, C_GATES // N_BRANCH)),
            rowblk(0),
            pl.BlockSpec((None, POOL_GROUPS, POOL_GROUP_DIM, POOL_GROUP_DIM), layer4),
            pl.BlockSpec((1, D_MODEL), const2),
            pl.BlockSpec((None, N_BRANCH, D_MODEL, D_MODEL), layer4),
            pl.BlockSpec((None, D_MODEL, D_MODEL), layer3),
            pl.BlockSpec((1, D_MODEL), const2),
        ],
        out_specs=pl.BlockSpec((tm, D_MODEL), lambda i: (i, 0)),
        out_shape=jax.ShapeDtypeStruct((m, D_MODEL), jnp.float32),
        compiler_params=pltpu.CompilerParams(
            dimension_semantics=("arbitrary",), vmem_limit_bytes=VMEM_LIMIT),
        name="merge",
    )(ya, yb, proj, proj, proj, proj, x2, pool_w, pool_scale, w_branch, w_out, final_g)


O_AI = 5 * D_MODEL
O_BQ = O_AI + 2 * ML_HEADS
O_BF = O_BQ + 4 * D_MODEL
O_CU = O_BF + FOX_HEADS
O_G = O_CU + 2 * D_MODEL
N_IN = O_G + N_BRANCH * D_MODEL
W_SEGMENTS = ((O_G, N_BRANCH), (0, 5), (O_BQ, 4), (O_CU, 2))
W_PREP_ROWS = 128


def _w_prep_kernel(wt_ref, main_ref, gate_ref):
    dst = 0
    for src, nblk in W_SEGMENTS:
        for n in range(nblk):
            a = src + n * D_MODEL
            blk = wt_ref[a:a + D_MODEL, :].T
            if dst == C_BQ * D_MODEL:
                blk = blk * (FOX_HEAD_DIM ** -0.5 * LOG2E)
            main_ref[:, dst:dst + D_MODEL] = blk.astype(jnp.bfloat16)
            dst += D_MODEL
    n_gate = 2 * ML_HEADS + FOX_HEADS
    gate = jnp.concatenate([wt_ref[O_AI:O_BQ, :], wt_ref[O_BF:O_CU, :],
                            jnp.zeros((LANES - n_gate, W_PREP_ROWS), jnp.float32)], axis=0)
    gate_ref[...] = gate.T.astype(jnp.bfloat16)


def _split_w_in(w_in):
    depth = w_in.shape[0]
    assert w_in.shape[1:] == (D_MODEL, N_IN)
    return pl.pallas_call(
        _w_prep_kernel,
        grid=(depth, D_MODEL // W_PREP_ROWS),
        in_specs=[pl.BlockSpec((None, N_IN, W_PREP_ROWS), lambda l, r: (l, 0, r))],
        out_specs=[
            pl.BlockSpec((None, W_PREP_ROWS, N_MAIN), lambda l, r: (l, r, 0)),
            pl.BlockSpec((None, W_PREP_ROWS, LANES), lambda l, r: (l, r, 0)),
        ],
        out_shape=[
            jax.ShapeDtypeStruct((depth, D_MODEL, N_MAIN), jnp.bfloat16),
            jax.ShapeDtypeStruct((depth, D_MODEL, LANES), jnp.bfloat16),
        ],
        compiler_params=pltpu.CompilerParams(
            dimension_semantics=("arbitrary", "arbitrary"), vmem_limit_bytes=VMEM_LIMIT),
        name="w_prep",
    )(jnp.swapaxes(w_in, 1, 2))


def kernel(x, norm_g, w_in, conv_w, ml_bi, ml_bf, ml_norm_g, fox_bf, pool_w, pool_scale, w_branch,
           w_out, final_g):
    batch, seq, d = x.shape
    depth = norm_g.shape[0]
    assert d == D_MODEL and all(seq % t == 0 for t in (IN_TM, MERGE_TM, ML_CHUNK, GATES_ROWS))
    assert GATES_ROWS % ML_CHUNK == 0
    x2 = x.reshape(batch * seq, d)
    w_main, w_gate = _split_w_in(w_in)
    pool_wb = pool_w.astype(jnp.bfloat16)
    w_branchb = w_branch.astype(jnp.bfloat16)
    w_outb = w_out.astype(jnp.bfloat16)
    for l in range(depth):
        gate_bias = jnp.pad(jnp.concatenate([ml_bi[l], ml_bf[l], fox_bf[l]]),
                            (0, LANES - 2 * ML_HEADS - FOX_HEADS)).reshape(1, LANES)
        proj, gate_pre = _in_proj(x2, norm_g[l].reshape(1, d), w_main, w_gate, l)
        gcol, grow = _gates(gate_pre, gate_bias, batch, seq)
        ya = _mlstm(proj, conv_w[l], gcol, grow, ml_norm_g[l].reshape(1, d), batch, seq)
        yb = _fox(proj, gcol, batch, seq)
        x2 = _merge(ya, yb, proj, x2, pool_wb, pool_scale[l].reshape(1, d), w_branchb, w_outb,
                    final_g.reshape(1, d), seq, final=(l == depth - 1), layer=l)
    return x2.reshape(batch, seq, d)
```

```python
import functools

import jax
import jax.numpy as jnp
from jax import lax
from jax.experimental import pallas as pl
from jax.experimental.pallas import tpu as pltpu

D_MODEL = 1024
ML_HEADS = 4
ML_HEAD_DIM = 256
CONV_WIDTH = 4
FOX_HEADS = 8
FOX_HEAD_DIM = 128
POOL_GROUPS = 4
POOL_GROUP_DIM = 256
POOL_WINDOWS = (2, 4, 8, 16)
N_BRANCH = 3
EPS = 1e-6

LANES = 128
HALO = 16
CONV_HALO = 8
NEG = -1e30
SKIP_MARGIN = 32.0

C_GATES = 0
C_AQ, C_AK, C_AV, C_AO, C_AZ, C_BQ, C_BK, C_BV, C_BZ, C_CU, C_CZ = range(N_BRANCH, N_BRANCH + 11)
N_MAIN = 14 * D_MODEL
G_AI, G_AF, G_BF = 0, 4, 8
G_FHI, G_FMID, G_FLO, G_ONE = 16, 24, 32, 127

ML_CHUNK = 256
IN_TM, IN_TN = 2048, 1024
FOX_TQ = 256
FOX_TK = 256
FOX_PAIR = 16
GATES_ROWS = 1024
LOG2E = 1.4426950408889634
MERGE_TM = 512
VMEM_LIMIT = 56 * 1024 * 1024


def _sigmoid(x):
    return 1.0 / (1.0 + jnp.exp(-x))


def _silu(x):
    return x * _sigmoid(x)


def _in_proj_kernel(x_ref, g_ref, w_ref, wg_ref, proj_ref, gate_ref, h_scr):
    @pl.when(pl.program_id(1) == 0)
    def _():
        x = x_ref[...]
        r = lax.rsqrt(jnp.mean(x * x, axis=-1, keepdims=True) + EPS)
        h = ((x * r) * g_ref[...]).astype(jnp.bfloat16)
        h_scr[...] = h
        gate_ref[...] = jnp.dot(h, wg_ref[...], preferred_element_type=jnp.float32)

    proj_ref[...] = jnp.dot(h_scr[...], w_ref[...],
                            preferred_element_type=jnp.float32).astype(jnp.bfloat16)


def _in_proj(x2, norm_g, w_main, w_gate, layer):
    m = x2.shape[0]
    return pl.pallas_call(
        _in_proj_kernel,
        grid=(m // IN_TM, N_MAIN // IN_TN),
        in_specs=[
            pl.BlockSpec((IN_TM, D_MODEL), lambda i, j: (i, 0)),
            pl.BlockSpec((1, D_MODEL), lambda i, j: (0, 0)),
            pl.BlockSpec((None, D_MODEL, IN_TN), lambda i, j: (layer, 0, j)),
            pl.BlockSpec((None, D_MODEL, LANES), lambda i, j: (layer, 0, 0)),
        ],
        out_specs=[
            pl.BlockSpec((IN_TM, IN_TN), lambda i, j: (i, j)),
            pl.BlockSpec((IN_TM, LANES), lambda i, j: (i, 0)),
        ],
        out_shape=[
            jax.ShapeDtypeStruct((m, N_MAIN), jnp.bfloat16),
            jax.ShapeDtypeStruct((m, LANES), jnp.float32),
        ],
        scratch_shapes=[pltpu.VMEM((IN_TM, D_MODEL), jnp.bfloat16)],
        compiler_params=pltpu.CompilerParams(
            dimension_semantics=("arbitrary", "arbitrary"), vmem_limit_bytes=VMEM_LIMIT),
        name="in_proj",
    )(x2, norm_g, w_main, w_gate)


def _gates_kernel(g_ref, bias_ref, col_ref, row_ref, carry_scr):
    @pl.when(pl.program_id(1) == 0)
    def _():
        carry_scr[...] = jnp.zeros_like(carry_scr)

    row = lax.broadcasted_iota(jnp.int32, (ML_CHUNK, LANES), 0)
    lane = lax.broadcasted_iota(jnp.int32, (ML_CHUNK, LANES), 1)
    carry = carry_scr[...]
    for c in range(GATES_ROWS // ML_CHUNK):
        rs = slice(c * ML_CHUNK, (c + 1) * ML_CHUNK)
        pre = g_ref[rs, :] + bias_ref[...]
        ls = jnp.minimum(pre, 0.0) - jnp.log1p(jnp.exp(-jnp.abs(pre)))
        cs = ls
        k = 1
        while k < ML_CHUNK:
            cs = cs + jnp.where(row >= k, pltpu.roll(cs, k, 0), 0.0)
            k *= 2
        run = cs + carry
        carry = run[ML_CHUNK - 1:ML_CHUNK, :]
        out = jnp.where(lane < G_AF, pre, jnp.where(lane < G_BF, cs, run))
        row_ref[0, :, rs] = out.T[0:16, :]
        f2 = run * LOG2E
        hi = f2.astype(jnp.bfloat16).astype(jnp.float32)
        r1 = f2 - hi
        mid = r1.astype(jnp.bfloat16).astype(jnp.float32)
        lo = r1 - mid
        pieces = jnp.where(lane < G_FMID, pltpu.roll(hi, G_FHI - G_BF, 1),
                           jnp.where(lane < G_FLO, pltpu.roll(mid, G_FMID - G_BF, 1),
                                     jnp.where(lane < G_FLO + FOX_HEADS, pltpu.roll(lo, G_FLO - G_BF, 1),
                                               jnp.where(lane == G_ONE, 1.0, 0.0))))
        col_ref[rs, :] = jnp.where(lane < G_FHI, out, pieces)
    carry_scr[...] = carry


def _gates(gate_pre, gate_bias, batch, seq):
    m = gate_pre.shape[0]
    nc = seq // GATES_ROWS
    return pl.pallas_call(
        _gates_kernel,
        grid=(batch, nc),
        in_specs=[
            pl.BlockSpec((GATES_ROWS, LANES), lambda b, c: (b * nc + c, 0)),
            pl.BlockSpec((1, LANES), lambda b, c: (0, 0)),
        ],
        out_specs=[
            pl.BlockSpec((GATES_ROWS, LANES), lambda b, c: (b * nc + c, 0)),
            pl.BlockSpec((1, 16, GATES_ROWS), lambda b, c: (b, 0, c)),
        ],
        out_shape=[
            jax.ShapeDtypeStruct((m, LANES), jnp.float32),
            jax.ShapeDtypeStruct((batch, 16, seq), jnp.float32),
        ],
        scratch_shapes=[pltpu.VMEM((1, LANES), jnp.float32)],
        compiler_params=pltpu.CompilerParams(dimension_semantics=("arbitrary", "arbitrary")),
        name="gates",
    )(gate_pre, gate_bias)


def _mlstm_kernel(aq_ref, ak_ref, av_ref, ao_ref, az_ref, cw_ref, gc_ref, gr_ref, ng_ref,
                  y_ref, qbuf, kbuf, sh_scr, ct_scr, m_scr):
    L = ML_CHUNK
    dh = ML_HEAD_DIM

    n_sh = CONV_WIDTH - 1

    @pl.when(pl.program_id(1) == 0)
    def _():
        qbuf[...] = jnp.zeros_like(qbuf)
        kbuf[...] = jnp.zeros_like(kbuf)
        ct_scr[...] = jnp.zeros_like(ct_scr)
        m_scr[...] = jnp.zeros_like(m_scr)
        r = lax.broadcasted_iota(jnp.int32, (n_sh * L, L), 0)
        c = lax.broadcasted_iota(jnp.int32, (n_sh * L, L), 1)
        hit = functools.reduce(jnp.logical_or,
                               [(r >= j * L) & (r < (j + 1) * L) & (c == r - j * L - (j + 1)) for j in range(n_sh)])
        sh_scr[...] = hit.astype(jnp.float32).astype(jnp.bfloat16)

    row8 = lax.broadcasted_iota(jnp.int32, (CONV_HALO, D_MODEL), 0)

    def conv(x_ref, prev, w):
        xb = x_ref[...]
        x = xb.astype(jnp.float32)
        shifted = jnp.dot(sh_scr[...], xb, preferred_element_type=jnp.float32)
        acc = x * w[n_sh:n_sh + 1, :]
        head = jnp.zeros((CONV_HALO, D_MODEL), jnp.float32)
        tail = prev[...]
        for j in range(1, n_sh + 1):
            wj = w[n_sh - j:n_sh - j + 1, :]
            acc = acc + shifted[(j - 1) * L:j * L, :] * wj
            head = head + jnp.where(row8 < j, pltpu.roll(tail, j, 0), 0.0) * wj
        prev[...] = x[L - CONV_HALO:, :]
        return jnp.concatenate([acc[:CONV_HALO] + head, acc[CONV_HALO:]], axis=0)

    cw = cw_ref[...]
    qc = (_silu(conv(aq_ref, qbuf, cw[:, :D_MODEL])) * (dh ** -0.5)).astype(jnp.bfloat16)
    kc = _silu(conv(ak_ref, kbuf, cw[:, D_MODEL:]))

    gcol = gc_ref[...]
    grow = gr_ref[0]
    t_idx = lax.broadcasted_iota(jnp.int32, (L, L), 0)
    s_idx = lax.broadcasted_iota(jnp.int32, (L, L), 1)
    causal = s_idx <= t_idx
    ones_col = (lax.broadcasted_iota(jnp.int32, (L, LANES), 1) == 0).astype(jnp.bfloat16)

    for h in range(ML_HEADS):
        sl = slice(h * dh, (h + 1) * dh)
        q = qc[:, sl]
        kf = kc[:, sl]
        kb = kf.astype(jnp.bfloat16)
        v_aug = jnp.concatenate([av_ref[:, sl], ones_col], axis=1)
        i_col = gcol[:, G_AI + h:G_AI + h + 1]
        b_col = gcol[:, G_AF + h:G_AF + h + 1]
        i_row = grow[G_AI + h:G_AI + h + 1, :]
        b_row = grow[G_AF + h:G_AF + h + 1, :]
        m_prev = m_scr[h:h + 1, 0:1]

        d = jnp.where(causal, b_col + (i_row - b_row), NEG)
        inter = b_col + m_prev
        m_t = jnp.maximum(inter, jnp.max(d, axis=-1, keepdims=True))
        w_inter = jnp.exp(inter - m_t)
        e = jnp.exp(d - m_t)
        s = lax.dot_general(q, kb, (((1,), (1,)), ((), ())), preferred_element_type=jnp.float32)
        p = (e * s).astype(jnp.bfloat16)
        ct = ct_scr[h]
        num_aug = (w_inter * jnp.dot(q, ct.astype(jnp.bfloat16), preferred_element_type=jnp.float32)
                   + jnp.dot(p, v_aug, preferred_element_type=jnp.float32))
        num = num_aug[:, :dh]
        den = num_aug[:, dh:dh + 1]
        hh = num / jnp.maximum(jnp.abs(den), jnp.exp(-m_t))
        hh = hh * lax.rsqrt(jnp.mean(hh * hh, axis=-1, keepdims=True) + EPS)
        y = (hh * ng_ref[:, sl] * _sigmoid(ao_ref[:, sl].astype(jnp.float32))
             * _silu(az_ref[:, sl].astype(jnp.float32)))
        y_ref[:, sl] = y.astype(jnp.bfloat16)

        b_last = b_col[L - 1:L, :]
        dec = b_last - b_col + i_col
        m_new = jnp.maximum(b_last + m_prev, jnp.max(dec, axis=0, keepdims=True))
        w_s = jnp.exp(dec - m_new)
        w_old = jnp.exp(b_last + m_prev - m_new)
        vw = (v_aug.astype(jnp.float32) * w_s).astype(jnp.bfloat16)
        upd = jnp.dot(kf.T.astype(jnp.bfloat16), vw, preferred_element_type=jnp.float32)
        ct_scr[h] = w_old * ct + upd
        m_scr[h:h + 1, :] = jnp.broadcast_to(m_new, (1, LANES))


def _mlstm(proj, conv_w, gcol, grow, ml_norm_g, batch, seq):
    m = proj.shape[0]
    L = ML_CHUNK
    nc = seq // L

    def col(c):
        return pl.BlockSpec((L, D_MODEL), lambda b, i, c=c: (b * nc + i, c))

    return pl.pallas_call(
        _mlstm_kernel,
        grid=(batch, nc),
        in_specs=[
            col(C_AQ), col(C_AK), col(C_AV), col(C_AO), col(C_AZ),
            pl.BlockSpec((CONV_WIDTH, 2 * D_MODEL), lambda b, i: (0, 0)),
            pl.BlockSpec((L, LANES), lambda b, i: (b * nc + i, 0)),
            pl.BlockSpec((1, 16, L), lambda b, i: (b, 0, i)),
            pl.BlockSpec((1, D_MODEL), lambda b, i: (0, 0)),
        ],
        out_specs=pl.BlockSpec((L, D_MODEL), lambda b, i: (b * nc + i, 0)),
        out_shape=jax.ShapeDtypeStruct((m, D_MODEL), jnp.bfloat16),
        scratch_shapes=[
            pltpu.VMEM((CONV_HALO, D_MODEL), jnp.float32),
            pltpu.VMEM((CONV_HALO, D_MODEL), jnp.float32),
            pltpu.VMEM(((CONV_WIDTH - 1) * L, L), jnp.bfloat16),
            pltpu.VMEM((ML_HEADS, ML_HEAD_DIM, ML_HEAD_DIM + LANES), jnp.float32),
            pltpu.VMEM((8, LANES), jnp.float32),
        ],
        compiler_params=pltpu.CompilerParams(
            dimension_semantics=("arbitrary", "arbitrary"), vmem_limit_bytes=VMEM_LIMIT),
        name="mlstm",
    )(proj, proj, proj, proj, proj, conv_w, gcol, grow, ml_norm_g)


def _fox_kernel(q_ref, k_ref, v_ref, z_ref, g_ref, y_ref, qa_scr, ka_scr, vt_scr, st_scr, *, seq):
    tq, tk = FOX_TQ, FOX_TK
    h = pl.program_id(1)
    i = pl.program_id(2)
    lane1 = lax.broadcasted_iota(jnp.int32, (1, LANES), 1)

    @pl.when(i == 0)
    def _():
        src = lax.broadcasted_iota(jnp.int32, (LANES, LANES), 0)
        dst = lax.broadcasted_iota(jnp.int32, (LANES, LANES), 1)
        piece = jnp.where(src == G_FHI + h, 0, jnp.where(src == G_FMID + h, 1,
                                                         jnp.where(src == G_FLO + h, 2, -1)))
        is_piece = piece >= 0
        one = src == G_ONE
        pq = ((is_piece & ((dst == piece) | (dst == 6))) | (one & (dst >= 3) & (dst < 6)))
        pk = (one & (dst < 3)).astype(jnp.float32) - (is_piece & (dst == piece + 3)).astype(jnp.float32)
        pq = pq.astype(jnp.float32).astype(jnp.bfloat16)
        pk = pk.astype(jnp.bfloat16)

        lane_k = lax.broadcasted_iota(jnp.int32, (tk, 2 * FOX_HEAD_DIM), 1)
        ka_scr[0:tk, :] = jnp.where(lane_k == FOX_HEAD_DIM + 3, NEG, 0.0).astype(jnp.bfloat16)
        vt_scr[0] = jnp.zeros((FOX_HEAD_DIM, tk), jnp.bfloat16)

        def build(c, carry):
            kmax, qn_vec, g_vec, fe_vec = carry
            r0 = pl.multiple_of(c * tk, tk)
            g = g_ref[pl.ds(r0, tk), :].astype(jnp.bfloat16)
            qx = jnp.dot(g, pq, preferred_element_type=jnp.float32)
            kx = jnp.dot(g, pk, preferred_element_type=jnp.float32)
            f = qx[:, 6:7]
            qs = q_ref[pl.ds(r0, tk), :]
            kk = k_ref[pl.ds(r0, tk), :]
            qa_scr[pl.ds(r0, tk), :] = jnp.concatenate([qs, qx.astype(jnp.bfloat16)], axis=1)
            ka_scr[pl.ds(r0 + tk, tk), :] = jnp.concatenate([kk, kx.astype(jnp.bfloat16)], axis=1)
            vt_scr[c + 1] = v_ref[pl.ds(r0, tk), :].astype(jnp.float32).T.astype(jnp.bfloat16)

            qf = qs.astype(jnp.float32)
            kf = kk.astype(jnp.float32)
            qn = jnp.sum(qf * qf, axis=1, keepdims=True)
            kn = jnp.sum(kf * kf, axis=1, keepdims=True)
            diag = jnp.sum(qf * kf, axis=1, keepdims=True)
            here = lane1 == c
            kmax = jnp.maximum(kmax, jnp.max(kn, axis=0, keepdims=True))
            qn_vec = jnp.where(here, jnp.max(qn, axis=0, keepdims=True), qn_vec)
            g_vec = jnp.where(here, jnp.max(f - diag, axis=0, keepdims=True), g_vec)
            fe_vec = jnp.where(here, f[tk - 1:tk, :], fe_vec)
            return kmax, qn_vec, g_vec, fe_vec

        zrow = jnp.zeros((1, LANES), jnp.float32)
        kmax, qn_vec, g_vec, fe_vec = lax.fori_loop(
            0, seq // tk, build, (jnp.zeros((1, 1), jnp.float32), zrow, zrow, zrow), unroll=4)
        st_scr[0:1, :] = jnp.sqrt(qn_vec * kmax) + g_vec
        st_scr[1:2, :] = fe_vec

    def back_blocks(blk):
        bound = jnp.sum(jnp.where(lane1 == blk, st_scr[0:1, :], 0.0), axis=1, keepdims=True)
        needed = (lane1 < blk) & jnp.logical_not(st_scr[1:2, :] >= bound + SKIP_MARGIN * LOG2E)
        return blk - jnp.min(jnp.where(needed, lane1, blk))

    blks = [i * FOX_PAIR + u for u in range(FOX_PAIR)]
    n_back = functools.reduce(jnp.maximum, [back_blocks(b) for b in blks])
    qas = [qa_scr[pl.ds(pl.multiple_of(b * tq, tq), tq), :] for b in blks]

    def attend(carry, t, diagonal):
        m, l, acc = carry
        kbs, parts = [], []
        for u in range(FOX_PAIR):
            kb = jnp.maximum(blks[u] - t + 1, 0)
            k0 = pl.multiple_of(kb * tk, tk)
            s = lax.dot_general(ka_scr[pl.ds(k0, tk), :], qas[u], (((1,), (1,)), ((), ())),
                                preferred_element_type=jnp.float32)
            if diagonal:
                key_idx = lax.broadcasted_iota(jnp.int32, (tk, tq), 0)
                qry_idx = lax.broadcasted_iota(jnp.int32, (tk, tq), 1)
                s = jnp.where(key_idx <= qry_idx, s, NEG)
            kbs.append(kb)
            parts.append(s)
        s = jnp.concatenate(parts, axis=1)
        m_new = jnp.maximum(m, jnp.max(s, axis=0, keepdims=True))
        alpha = jnp.exp2(m - m_new)
        p = jnp.exp2(s - m_new)
        l = alpha * l + jnp.sum(p, axis=0, keepdims=True)
        p = p.astype(jnp.bfloat16)
        pv = jnp.concatenate(
            [jnp.dot(vt_scr[kbs[u]], p[:, u * tq:(u + 1) * tq], preferred_element_type=jnp.float32)
             for u in range(FOX_PAIR)], axis=1)
        return m_new, l, alpha * acc + pv

    rows = FOX_PAIR * tq
    init = (jnp.full((1, rows), NEG, jnp.float32), jnp.zeros((1, rows), jnp.float32),
            jnp.zeros((FOX_HEAD_DIM, rows), jnp.float32))
    carry = attend(init, 0, True)
    _, l, acc = lax.fori_loop(1, n_back + 1, lambda t, c: attend(c, t, False), carry)
    y_ref[...] = ((acc / l).T * _silu(z_ref[...].astype(jnp.float32))).astype(jnp.bfloat16)


def _fox(proj, gcol, batch, seq):
    m = proj.shape[0]
    rows = FOX_PAIR * FOX_TQ
    nq = seq // rows
    assert FOX_TQ == FOX_TK and seq % rows == 0 and seq // FOX_TQ <= LANES
    hb = D_MODEL // FOX_HEAD_DIM

    def head(c):
        return pl.BlockSpec((seq, FOX_HEAD_DIM), lambda b, h, i, c=c: (b, c * hb + h))

    return pl.pallas_call(
        functools.partial(_fox_kernel, seq=seq),
        grid=(batch, FOX_HEADS, nq),
        in_specs=[
            head(C_BQ), head(C_BK), head(C_BV),
            pl.BlockSpec((rows, FOX_HEAD_DIM), lambda b, h, i: (b * nq + i, C_BZ * hb + h)),
            pl.BlockSpec((seq, LANES), lambda b, h, i: (b, 0)),
        ],
        out_specs=pl.BlockSpec((rows, FOX_HEAD_DIM), lambda b, h, i: (b * nq + i, h)),
        out_shape=jax.ShapeDtypeStruct((m, D_MODEL), jnp.bfloat16),
        scratch_shapes=[
            pltpu.VMEM((seq, 2 * FOX_HEAD_DIM), jnp.bfloat16),
            pltpu.VMEM((seq + FOX_TK, 2 * FOX_HEAD_DIM), jnp.bfloat16),
            pltpu.VMEM((seq // FOX_TK + 1, FOX_HEAD_DIM, FOX_TK), jnp.bfloat16),
            pltpu.VMEM((8, LANES), jnp.float32),
        ],
        compiler_params=pltpu.CompilerParams(
            dimension_semantics=("arbitrary", "arbitrary", "arbitrary"), vmem_limit_bytes=VMEM_LIMIT),
        name="fox",
    )(proj, proj, proj, proj, gcol)


def _merge_kernel(ya_ref, yb_ref, cu_ref, halo_ref, cz_ref, gates_ref, x_ref, pw_ref, ps_ref,
                  wb_ref, wo_ref, fg_ref, o_ref, *, seq, final):
    tm = MERGE_TM
    gd = POOL_GROUP_DIM
    i = pl.program_id(0)
    t0 = (i * tm) % seq
    pos = t0 + lax.broadcasted_iota(jnp.int32, (tm, 1), 0)
    halo = jnp.where(t0 == 0, 0.0, halo_ref[...].astype(jnp.float32))
    u = cu_ref[...].astype(jnp.float32)
    ext = jnp.concatenate([halo, u], axis=0)

    ys = []
    for g, win in enumerate(POOL_WINDOWS):
        sl = slice(g * gd, (g + 1) * gd)
        acc = ext[:, sl]
        step = 1
        while step < win:
            acc = acc + pltpu.roll(acc, step, 0)
            step *= 2
        cnt = jnp.minimum(pos + 1, win).astype(jnp.float32)
        d = acc[HALO:, :] / cnt - u[:, sl]
        ys.append(jnp.dot(d.astype(jnp.bfloat16), pw_ref[g], preferred_element_type=jnp.float32))
    yc = jnp.concatenate(ys, axis=1) * ps_ref[...]
    yc = yc * _silu(cz_ref[...].astype(jnp.float32))

    merged = jnp.zeros((tm, D_MODEL), jnp.float32)
    for n, y in enumerate((ya_ref[...], yb_ref[...], yc.astype(jnp.bfloat16))):
        yb = jnp.dot(y, wb_ref[n], preferred_element_type=jnp.float32)
        gate = _sigmoid(gates_ref[:, n * D_MODEL:(n + 1) * D_MODEL].astype(jnp.float32))
        merged = merged + gate * yb
    out = x_ref[...] + jnp.dot(merged.astype(jnp.bfloat16), wo_ref[...],
                               preferred_element_type=jnp.float32)
    if final:
        r = lax.rsqrt(jnp.mean(out * out, axis=-1, keepdims=True) + EPS)
        out = (out * r) * fg_ref[...]
    o_ref[...] = out


def _merge(ya, yb, proj, x2, pool_w, pool_scale, w_branch, w_out, final_g, seq, final, layer):
    m = x2.shape[0]
    tm = MERGE_TM
    hpt = tm // HALO

    def rowblk(c, width=1):
        return pl.BlockSpec((tm, width * D_MODEL), lambda i, c=c: (i, c))

    const2 = lambda i: (0, 0)
    layer3 = lambda i: (layer, 0, 0)
    layer4 = lambda i: (layer, 0, 0, 0)
    return pl.pallas_call(
        functools.partial(_merge_kernel, seq=seq, final=final),
        grid=(m // tm,),
        in_specs=[
            rowblk(0), rowblk(0), rowblk(C_CU),
            pl.BlockSpec((HALO, D_MODEL), lambda i: (jnp.maximum(i * hpt - 1, 0), C_CU)),
            rowblk(C_CZ),
            pl.BlockSpec((tm, N_BRANCH * D_MODEL), lambda i: (i, C_GATES // N_BRANCH)),
            rowblk(0),
            pl.BlockSpec((None, POOL_GROUPS, POOL_GROUP_DIM, POOL_GROUP_DIM), layer4),
            pl.BlockSpec((1, D_MODEL), const2),
            pl.BlockSpec((None, N_BRANCH, D_MODEL, D_MODEL), layer4),
            pl.BlockSpec((None, D_MODEL, D_MODEL), layer3),
            pl.BlockSpec((1, D_MODEL), const2),
        ],
        out_specs=pl.BlockSpec((tm, D_MODEL), lambda i: (i, 0)),
        out_shape=jax.ShapeDtypeStruct((m, D_MODEL), jnp.float32),
        compiler_params=pltpu.CompilerParams(
            dimension_semantics=("arbitrary",), vmem_limit_bytes=VMEM_LIMIT),
        name="merge",
    )(ya, yb, proj, proj, proj, proj, x2, pool_w, pool_scale, w_branch, w_out, final_g)


O_AI = 5 * D_MODEL
O_BQ = O_AI + 2 * ML_HEADS
O_BF = O_BQ + 4 * D_MODEL
O_CU = O_BF + FOX_HEADS
O_G = O_CU + 2 * D_MODEL
N_IN = O_G + N_BRANCH * D_MODEL
W_SEGMENTS = ((O_G, N_BRANCH), (0, 5), (O_BQ, 4), (O_CU, 2))
W_PREP_ROWS = 128


def _w_prep_kernel(wt_ref, main_ref, gate_ref):
    dst = 0
    for src, nblk in W_SEGMENTS:
        for n in range(nblk):
            a = src + n * D_MODEL
            blk = wt_ref[a:a + D_MODEL, :].T
            if dst == C_BQ * D_MODEL:
                blk = blk * (FOX_HEAD_DIM ** -0.5 * LOG2E)
            main_ref[:, dst:dst + D_MODEL] = blk.astype(jnp.bfloat16)
            dst += D_MODEL
    n_gate = 2 * ML_HEADS + FOX_HEADS
    gate = jnp.concatenate([wt_ref[O_AI:O_BQ, :], wt_ref[O_BF:O_CU, :],
                            jnp.zeros((LANES - n_gate, W_PREP_ROWS), jnp.float32)], axis=0)
    gate_ref[...] = gate.T.astype(jnp.bfloat16)


def _split_w_in(w_in):
    depth = w_in.shape[0]
    assert w_in.shape[1:] == (D_MODEL, N_IN)
    return pl.pallas_call(
        _w_prep_kernel,
        grid=(depth, D_MODEL // W_PREP_ROWS),
        in_specs=[pl.BlockSpec((None, N_IN, W_PREP_ROWS), lambda l, r: (l, 0, r))],
        out_specs=[
            pl.BlockSpec((None, W_PREP_ROWS, N_MAIN), lambda l, r: (l, r, 0)),
            pl.BlockSpec((None, W_PREP_ROWS, LANES), lambda l, r: (l, r, 0)),
        ],
        out_shape=[
            jax.ShapeDtypeStruct((depth, D_MODEL, N_MAIN), jnp.bfloat16),
            jax.ShapeDtypeStruct((depth, D_MODEL, LANES), jnp.bfloat16),
        ],
        compiler_params=pltpu.CompilerParams(
            dimension_semantics=("arbitrary", "arbitrary"), vmem_limit_bytes=VMEM_LIMIT),
        name="w_prep",
    )(jnp.swapaxes(w_in, 1, 2))


def kernel(x, norm_g, w_in, conv_w, ml_bi, ml_bf, ml_norm_g, fox_bf, pool_w, pool_scale, w_branch,
           w_out, final_g):
    batch, seq, d = x.shape
    depth = norm_g.shape[0]
    assert d == D_MODEL and all(seq % t == 0 for t in (IN_TM, MERGE_TM, ML_CHUNK, GATES_ROWS))
    assert GATES_ROWS % ML_CHUNK == 0
    x2 = x.reshape(batch * seq, d)
    w_main, w_gate = _split_w_in(w_in)
    pool_wb = pool_w.astype(jnp.bfloat16)
    w_branchb = w_branch.astype(jnp.bfloat16)
    w_outb = w_out.astype(jnp.bfloat16)
    for l in range(depth):
        gate_bias = jnp.pad(jnp.concatenate([ml_bi[l], ml_bf[l], fox_bf[l]]),
                            (0, LANES - 2 * ML_HEADS - FOX_HEADS)).reshape(1, LANES)
        proj, gate_pre = _in_proj(x2, norm_g[l].reshape(1, d), w_main, w_gate, l)
        gcol, grow = _gates(gate_pre, gate_bias, batch, seq)
        ya = _mlstm(proj, conv_w[l], gcol, grow, ml_norm_g[l].reshape(1, d), batch, seq)
        yb = _fox(proj, gcol, batch, seq)
        x2 = _merge(ya, yb, proj, x2, pool_wb, pool_scale[l].reshape(1, d), w_branchb, w_outb,
                    final_g.reshape(1, d), seq, final=(l == depth - 1), layer=l)
    return x2.reshape(batch, seq, d)
```

```python
import functools

import jax
import jax.numpy as jnp
from jax import lax
from jax.experimental import pallas as pl
from jax.experimental.pallas import tpu as pltpu

D_MODEL = 1024
ML_HEADS = 4
ML_HEAD_DIM = 256
CONV_WIDTH = 4
FOX_HEADS = 8
FOX_HEAD_DIM = 128
POOL_GROUPS = 4
POOL_GROUP_DIM = 256
POOL_WINDOWS = (2, 4, 8, 16)
N_BRANCH = 3
EPS = 1e-6

LANES = 128
HALO = 16
CONV_HALO = 8
NEG = -1e30
FIXED_REF_SLACK = 100.0
SKIP_MARGIN = 32.0

C_GATES = 0
C_AQ, C_AK, C_AV, C_AO, C_AZ, C_BQ, C_BK, C_BV, C_BZ, C_CU, C_CZ = range(N_BRANCH, N_BRANCH + 11)
N_MAIN = 14 * D_MODEL
G_AI, G_AF, G_BF = 0, 4, 8
G_FHI, G_FMID, G_FLO, G_ONE = 16, 24, 32, 127

ML_CHUNK = 256
IN_TM, IN_TN = 2048, 1024
FOX_TQ = 256
FOX_TK = 256
FOX_PAIR = 16
GATES_ROWS = 1024
LOG2E = 1.4426950408889634
MERGE_TM = 512
VMEM_LIMIT = 56 * 1024 * 1024


def _sigmoid(x):
    return 1.0 / (1.0 + jnp.exp(-x))


def _silu(x):
    return x * _sigmoid(x)


def _in_proj_kernel(x_ref, g_ref, w_ref, wg_ref, proj_ref, gate_ref, h_scr):
    @pl.when(pl.program_id(1) == 0)
    def _():
        x = x_ref[...]
        r = lax.rsqrt(jnp.mean(x * x, axis=-1, keepdims=True) + EPS)
        h = ((x * r) * g_ref[...]).astype(jnp.bfloat16)
        h_scr[...] = h
        gate_ref[...] = jnp.dot(h, wg_ref[...], preferred_element_type=jnp.float32)

    proj_ref[...] = jnp.dot(h_scr[...], w_ref[...],
                            preferred_element_type=jnp.float32).astype(jnp.bfloat16)


def _in_proj(x2, norm_g, w_main, w_gate, layer):
    m = x2.shape[0]
    return pl.pallas_call(
        _in_proj_kernel,
        grid=(m // IN_TM, N_MAIN // IN_TN),
        in_specs=[
            pl.BlockSpec((IN_TM, D_MODEL), lambda i, j: (i, 0)),
            pl.BlockSpec((1, D_MODEL), lambda i, j: (0, 0)),
            pl.BlockSpec((None, D_MODEL, IN_TN), lambda i, j: (layer, 0, j)),
            pl.BlockSpec((None, D_MODEL, LANES), lambda i, j: (layer, 0, 0)),
        ],
        out_specs=[
            pl.BlockSpec((IN_TM, IN_TN), lambda i, j: (i, j)),
            pl.BlockSpec((IN_TM, LANES), lambda i, j: (i, 0)),
        ],
        out_shape=[
            jax.ShapeDtypeStruct((m, N_MAIN), jnp.bfloat16),
            jax.ShapeDtypeStruct((m, LANES), jnp.float32),
        ],
        scratch_shapes=[pltpu.VMEM((IN_TM, D_MODEL), jnp.bfloat16)],
        compiler_params=pltpu.CompilerParams(
            dimension_semantics=("arbitrary", "arbitrary"), vmem_limit_bytes=VMEM_LIMIT),
        name="in_proj",
    )(x2, norm_g, w_main, w_gate)


def _gates_kernel(g_ref, bias_ref, col_ref, row_ref, carry_scr):
    @pl.when(pl.program_id(1) == 0)
    def _():
        carry_scr[...] = jnp.zeros_like(carry_scr)

    row = lax.broadcasted_iota(jnp.int32, (ML_CHUNK, LANES), 0)
    lane = lax.broadcasted_iota(jnp.int32, (ML_CHUNK, LANES), 1)
    carry = carry_scr[...]
    for c in range(GATES_ROWS // ML_CHUNK):
        rs = slice(c * ML_CHUNK, (c + 1) * ML_CHUNK)
        pre = g_ref[rs, :] + bias_ref[...]
        ls = jnp.minimum(pre, 0.0) - jnp.log1p(jnp.exp(-jnp.abs(pre)))
        cs = ls
        k = 1
        while k < ML_CHUNK:
            cs = cs + jnp.where(row >= k, pltpu.roll(cs, k, 0), 0.0)
            k *= 2
        run = cs + carry
        carry = run[ML_CHUNK - 1:ML_CHUNK, :]
        out = jnp.where(lane < G_AF, pre, jnp.where(lane < G_BF, cs, run))
        row_ref[0, :, rs] = out.T[0:16, :]
        f2 = run * LOG2E
        hi = f2.astype(jnp.bfloat16).astype(jnp.float32)
        r1 = f2 - hi
        mid = r1.astype(jnp.bfloat16).astype(jnp.float32)
        lo = r1 - mid
        pieces = jnp.where(lane < G_FMID, pltpu.roll(hi, G_FHI - G_BF, 1),
                           jnp.where(lane < G_FLO, pltpu.roll(mid, G_FMID - G_BF, 1),
                                     jnp.where(lane < G_FLO + FOX_HEADS, pltpu.roll(lo, G_FLO - G_BF, 1),
                                               jnp.where(lane == G_ONE, 1.0, 0.0))))
        col_ref[rs, :] = jnp.where(lane < G_FHI, out, pieces)
    carry_scr[...] = carry


def _gates(gate_pre, gate_bias, batch, seq):
    m = gate_pre.shape[0]
    nc = seq // GATES_ROWS
    return pl.pallas_call(
        _gates_kernel,
        grid=(batch, nc),
        in_specs=[
            pl.BlockSpec((GATES_ROWS, LANES), lambda b, c: (b * nc + c, 0)),
            pl.BlockSpec((1, LANES), lambda b, c: (0, 0)),
        ],
        out_specs=[
            pl.BlockSpec((GATES_ROWS, LANES), lambda b, c: (b * nc + c, 0)),
            pl.BlockSpec((1, 16, GATES_ROWS), lambda b, c: (b, 0, c)),
        ],
        out_shape=[
            jax.ShapeDtypeStruct((m, LANES), jnp.float32),
            jax.ShapeDtypeStruct((batch, 16, seq), jnp.float32),
        ],
        scratch_shapes=[pltpu.VMEM((1, LANES), jnp.float32)],
        compiler_params=pltpu.CompilerParams(dimension_semantics=("arbitrary", "arbitrary")),
        name="gates",
    )(gate_pre, gate_bias)


def _mlstm_kernel(aq_ref, ak_ref, av_ref, ao_ref, az_ref, cw_ref, gc_ref, gr_ref, ng_ref,
                  y_ref, qbuf, kbuf, sh_scr, ct_scr, m_scr):
    L = ML_CHUNK
    dh = ML_HEAD_DIM

    n_sh = CONV_WIDTH - 1

    @pl.when(pl.program_id(1) == 0)
    def _():
        qbuf[...] = jnp.zeros_like(qbuf)
        kbuf[...] = jnp.zeros_like(kbuf)
        ct_scr[...] = jnp.zeros_like(ct_scr)
        m_scr[...] = jnp.zeros_like(m_scr)
        r = lax.broadcasted_iota(jnp.int32, (n_sh * L, L), 0)
        c = lax.broadcasted_iota(jnp.int32, (n_sh * L, L), 1)
        hit = functools.reduce(jnp.logical_or,
                               [(r >= j * L) & (r < (j + 1) * L) & (c == r - j * L - (j + 1)) for j in range(n_sh)])
        sh_scr[...] = hit.astype(jnp.float32).astype(jnp.bfloat16)

    row8 = lax.broadcasted_iota(jnp.int32, (CONV_HALO, D_MODEL), 0)

    def conv(x_ref, prev, w):
        xb = x_ref[...]
        x = xb.astype(jnp.float32)
        shifted = jnp.dot(sh_scr[...], xb, preferred_element_type=jnp.float32)
        acc = x * w[n_sh:n_sh + 1, :]
        head = jnp.zeros((CONV_HALO, D_MODEL), jnp.float32)
        tail = prev[...]
        for j in range(1, n_sh + 1):
            wj = w[n_sh - j:n_sh - j + 1, :]
            acc = acc + shifted[(j - 1) * L:j * L, :] * wj
            head = head + jnp.where(row8 < j, pltpu.roll(tail, j, 0), 0.0) * wj
        prev[...] = x[L - CONV_HALO:, :]
        return jnp.concatenate([acc[:CONV_HALO] + head, acc[CONV_HALO:]], axis=0)

    cw = cw_ref[...]
    qc = (_silu(conv(aq_ref, qbuf, cw[:, :D_MODEL])) * (dh ** -0.5)).astype(jnp.bfloat16)
    kc = _silu(conv(ak_ref, kbuf, cw[:, D_MODEL:]))

    gcol = gc_ref[...]
    grow = gr_ref[0]
    t_idx = lax.broadcasted_iota(jnp.int32, (L, L), 0)
    s_idx = lax.broadcasted_iota(jnp.int32, (L, L), 1)
    causal = s_idx <= t_idx
    ones_col = (lax.broadcasted_iota(jnp.int32, (L, LANES), 1) == 0).astype(jnp.bfloat16)

    for h in range(ML_HEADS):
        sl = slice(h * dh, (h + 1) * dh)
        q = qc[:, sl]
        kf = kc[:, sl]
        kb = kf.astype(jnp.bfloat16)
        v_aug = jnp.concatenate([av_ref[:, sl], ones_col], axis=1)
        i_col = gcol[:, G_AI + h:G_AI + h + 1]
        b_col = gcol[:, G_AF + h:G_AF + h + 1]
        i_row = grow[G_AI + h:G_AI + h + 1, :]
        b_row = grow[G_AF + h:G_AF + h + 1, :]
        m_prev = m_scr[h:h + 1, 0:1]

        d = jnp.where(causal, b_col + (i_row - b_row), NEG)
        inter = b_col + m_prev
        m_t = jnp.maximum(inter, jnp.max(d, axis=-1, keepdims=True))
        w_inter = jnp.exp(inter - m_t)
        e = jnp.exp(d - m_t)
        s = lax.dot_general(q, kb, (((1,), (1,)), ((), ())), preferred_element_type=jnp.float32)
        p = (e * s).astype(jnp.bfloat16)
        ct = ct_scr[h]
        num_aug = (w_inter * jnp.dot(q, ct.astype(jnp.bfloat16), preferred_element_type=jnp.float32)
                   + jnp.dot(p, v_aug, preferred_element_type=jnp.float32))
        num = num_aug[:, :dh]
        den = num_aug[:, dh:dh + 1]
        hh = num / jnp.maximum(jnp.abs(den), jnp.exp(-m_t))
        hh = hh * lax.rsqrt(jnp.mean(hh * hh, axis=-1, keepdims=True) + EPS)
        y = (hh * ng_ref[:, sl] * _sigmoid(ao_ref[:, sl].astype(jnp.float32))
             * _silu(az_ref[:, sl].astype(jnp.float32)))
        y_ref[:, sl] = y.astype(jnp.bfloat16)

        b_last = b_col[L - 1:L, :]
        dec = b_last - b_col + i_col
        m_new = jnp.maximum(b_last + m_prev, jnp.max(dec, axis=0, keepdims=True))
        w_s = jnp.exp(dec - m_new)
        w_old = jnp.exp(b_last + m_prev - m_new)
        upd = jnp.dot((kf * w_s).T.astype(jnp.bfloat16), v_aug, preferred_element_type=jnp.float32)
        ct_scr[h] = w_old * ct + upd
        m_scr[h:h + 1, :] = jnp.broadcast_to(m_new, (1, LANES))


def _mlstm(proj, conv_w, gcol, grow, ml_norm_g, batch, seq):
    m = proj.shape[0]
    L = ML_CHUNK
    nc = seq // L

    def col(c):
        return pl.BlockSpec((L, D_MODEL), lambda b, i, c=c: (b * nc + i, c))

    return pl.pallas_call(
        _mlstm_kernel,
        grid=(batch, nc),
        in_specs=[
            col(C_AQ), col(C_AK), col(C_AV), col(C_AO), col(C_AZ),
            pl.BlockSpec((CONV_WIDTH, 2 * D_MODEL), lambda b, i: (0, 0)),
            pl.BlockSpec((L, LANES), lambda b, i: (b * nc + i, 0)),
            pl.BlockSpec((1, 16, L), lambda b, i: (b, 0, i)),
            pl.BlockSpec((1, D_MODEL), lambda b, i: (0, 0)),
        ],
        out_specs=pl.BlockSpec((L, D_MODEL), lambda b, i: (b * nc + i, 0)),
        out_shape=jax.ShapeDtypeStruct((m, D_MODEL), jnp.bfloat16),
        scratch_shapes=[
            pltpu.VMEM((CONV_HALO, D_MODEL), jnp.float32),
            pltpu.VMEM((CONV_HALO, D_MODEL), jnp.float32),
            pltpu.VMEM(((CONV_WIDTH - 1) * L, L), jnp.bfloat16),
            pltpu.VMEM((ML_HEADS, ML_HEAD_DIM, ML_HEAD_DIM + LANES), jnp.float32),
            pltpu.VMEM((8, LANES), jnp.float32),
        ],
        compiler_params=pltpu.CompilerParams(
            dimension_semantics=("arbitrary", "arbitrary"), vmem_limit_bytes=VMEM_LIMIT),
        name="mlstm",
    )(proj, proj, proj, proj, proj, conv_w, gcol, grow, ml_norm_g)


def _fox_kernel(q_ref, k_ref, v_ref, z_ref, g_ref, y_ref, qa_scr, ka_scr, vt_scr, st_scr, *, seq):
    tq, tk = FOX_TQ, FOX_TK
    h = pl.program_id(1)
    i = pl.program_id(2)
    lane1 = lax.broadcasted_iota(jnp.int32, (1, LANES), 1)

    @pl.when(i == 0)
    def _():
        src = lax.broadcasted_iota(jnp.int32, (LANES, LANES), 0)
        dst = lax.broadcasted_iota(jnp.int32, (LANES, LANES), 1)
        piece = jnp.where(src == G_FHI + h, 0, jnp.where(src == G_FMID + h, 1,
                                                         jnp.where(src == G_FLO + h, 2, -1)))
        is_piece = piece >= 0
        one = src == G_ONE
        pq = ((is_piece & ((dst == piece) | (dst == 6))) | (one & (dst >= 3) & (dst < 6)))
        pk = ((one & ((dst < 3) | (dst == 7))).astype(jnp.float32)
              - (is_piece & (dst == piece + 3)).astype(jnp.float32))
        pq = pq.astype(jnp.float32).astype(jnp.bfloat16)
        pk = pk.astype(jnp.bfloat16)

        lane_k = lax.broadcasted_iota(jnp.int32, (tk, 2 * FOX_HEAD_DIM), 1)
        ka_scr[0:tk, :] = jnp.where(lane_k == FOX_HEAD_DIM + 3, NEG, 0.0).astype(jnp.bfloat16)
        vt_scr[0] = jnp.zeros((FOX_HEAD_DIM, tk), jnp.bfloat16)

        lane_q = lax.broadcasted_iota(jnp.int32, (tk, LANES), 1)

        def build(c, carry):
            kmax, qn_vec, g_vec, fe_vec, sl_vec = carry
            r0 = pl.multiple_of(c * tk, tk)
            g = g_ref[pl.ds(r0, tk), :].astype(jnp.bfloat16)
            qx = jnp.dot(g, pq, preferred_element_type=jnp.float32)
            kx = jnp.dot(g, pk, preferred_element_type=jnp.float32)
            f = qx[:, 6:7]
            qs = q_ref[pl.ds(r0, tk), :]
            kk = k_ref[pl.ds(r0, tk), :]

            qf = qs.astype(jnp.float32)
            kf = kk.astype(jnp.float32)
            qn = jnp.sum(qf * qf, axis=1, keepdims=True)
            kn = jnp.sum(kf * kf, axis=1, keepdims=True)
            diag = jnp.sum(qf * kf, axis=1, keepdims=True)
            here = lane1 == c
            kmax = jnp.maximum(kmax, jnp.max(kn, axis=0, keepdims=True))
            qmax = jnp.max(qn, axis=0, keepdims=True)
            qn_vec = jnp.where(here, qmax, qn_vec)
            g_vec = jnp.where(here, jnp.max(f - diag, axis=0, keepdims=True), g_vec)
            fe_vec = jnp.where(here, f[tk - 1:tk, :], fe_vec)
            ref = (-jnp.sqrt(qmax * kmax)).astype(jnp.bfloat16).astype(jnp.float32)
            qx = jnp.where(lane_q == 7, ref, qx)
            sl_vec = jnp.where(here, -ref - jnp.min(diag, axis=0, keepdims=True), sl_vec)

            qa_scr[pl.ds(r0, tk), :] = jnp.concatenate([qs, qx.astype(jnp.bfloat16)], axis=1)
            ka_scr[pl.ds(r0 + tk, tk), :] = jnp.concatenate([kk, kx.astype(jnp.bfloat16)], axis=1)
            vt_scr[c + 1] = v_ref[pl.ds(r0, tk), :].astype(jnp.float32).T.astype(jnp.bfloat16)
            return kmax, qn_vec, g_vec, fe_vec, sl_vec

        zrow = jnp.zeros((1, LANES), jnp.float32)
        kmax, qn_vec, g_vec, fe_vec, sl_vec = lax.fori_loop(
            0, seq // tk, build, (jnp.zeros((1, 1), jnp.float32), zrow, zrow, zrow, zrow), unroll=4)
        st_scr[2:3, :] = sl_vec
        st_scr[0:1, :] = jnp.sqrt(qn_vec * kmax) + g_vec
        st_scr[1:2, :] = fe_vec

    def back_blocks(blk):
        bound = jnp.sum(jnp.where(lane1 == blk, st_scr[0:1, :], 0.0), axis=1, keepdims=True)
        needed = (lane1 < blk) & jnp.logical_not(st_scr[1:2, :] >= bound + SKIP_MARGIN * LOG2E)
        return blk - jnp.min(jnp.where(needed, lane1, blk))

    blks = [i * FOX_PAIR + u for u in range(FOX_PAIR)]
    n_back = functools.reduce(jnp.maximum, [back_blocks(b) for b in blks])
    qas = [qa_scr[pl.ds(pl.multiple_of(b * tq, tq), tq), :] for b in blks]

    def scores(t, diagonal):
        kbs, parts = [], []
        for u in range(FOX_PAIR):
            kb = jnp.maximum(blks[u] - t + 1, 0)
            k0 = pl.multiple_of(kb * tk, tk)
            s = lax.dot_general(ka_scr[pl.ds(k0, tk), :], qas[u], (((1,), (1,)), ((), ())),
                                preferred_element_type=jnp.float32)
            if diagonal:
                key_idx = lax.broadcasted_iota(jnp.int32, (tk, tq), 0)
                qry_idx = lax.broadcasted_iota(jnp.int32, (tk, tq), 1)
                s = jnp.where(key_idx <= qry_idx, s, NEG)
            kbs.append(kb)
            parts.append(s)
        return jnp.concatenate(parts, axis=1), kbs

    def weighted_values(p, kbs):
        p = p.astype(jnp.bfloat16)
        return jnp.concatenate(
            [jnp.dot(vt_scr[kbs[u]], p[:, u * tq:(u + 1) * tq], preferred_element_type=jnp.float32)
             for u in range(FOX_PAIR)], axis=1)

    def attend_online(carry, t, diagonal):
        m, l, acc = carry
        s, kbs = scores(t, diagonal)
        m_new = jnp.maximum(m, jnp.max(s, axis=0, keepdims=True))
        alpha = jnp.exp2(m - m_new)
        p = jnp.exp2(s - m_new)
        l = alpha * l + jnp.sum(p, axis=0, keepdims=True)
        return m_new, l, alpha * acc + weighted_values(p, kbs)

    def attend_fixed(carry, t, diagonal):
        l, acc = carry
        s, kbs = scores(t, diagonal)
        p = jnp.exp2(s)
        return l + jnp.sum(p, axis=0, keepdims=True), acc + weighted_values(p, kbs)

    rows = FOX_PAIR * tq
    zl = jnp.zeros((1, rows), jnp.float32)
    zacc = jnp.zeros((FOX_HEAD_DIM, rows), jnp.float32)

    def fixed_path():
        carry = attend_fixed((zl, zacc), 0, True)
        return lax.fori_loop(1, n_back + 1, lambda t, c: attend_fixed(c, t, False), carry)

    def online_path():
        carry = attend_online((jnp.full((1, rows), NEG, jnp.float32), zl, zacc), 0, True)
        _, l, acc = lax.fori_loop(1, n_back + 1, lambda t, c: attend_online(c, t, False), carry)
        return l, acc

    mine = (lane1 >= blks[0]) & (lane1 < blks[0] + FOX_PAIR)
    worst = jnp.max(jnp.where(mine, st_scr[2:3, :], 0.0))
    l, acc = lax.cond(worst <= FIXED_REF_SLACK, fixed_path, online_path)
    y_ref[...] = ((acc / l).T * _silu(z_ref[...].astype(jnp.float32))).astype(jnp.bfloat16)


def _fox(proj, gcol, batch, seq):
    m = proj.shape[0]
    rows = FOX_PAIR * FOX_TQ
    nq = seq // rows
    assert FOX_TQ == FOX_TK and seq % rows == 0 and seq // FOX_TQ <= LANES
    hb = D_MODEL // FOX_HEAD_DIM

    def head(c):
        return pl.BlockSpec((seq, FOX_HEAD_DIM), lambda b, h, i, c=c: (b, c * hb + h))

    return pl.pallas_call(
        functools.partial(_fox_kernel, seq=seq),
        grid=(batch, FOX_HEADS, nq),
        in_specs=[
            head(C_BQ), head(C_BK), head(C_BV),
            pl.BlockSpec((rows, FOX_HEAD_DIM), lambda b, h, i: (b * nq + i, C_BZ * hb + h)),
            pl.BlockSpec((seq, LANES), lambda b, h, i: (b, 0)),
        ],
        out_specs=pl.BlockSpec((rows, FOX_HEAD_DIM), lambda b, h, i: (b * nq + i, h)),
        out_shape=jax.ShapeDtypeStruct((m, D_MODEL), jnp.bfloat16),
        scratch_shapes=[
            pltpu.VMEM((seq, 2 * FOX_HEAD_DIM), jnp.bfloat16),
            pltpu.VMEM((seq + FOX_TK, 2 * FOX_HEAD_DIM), jnp.bfloat16),
            pltpu.VMEM((seq // FOX_TK + 1, FOX_HEAD_DIM, FOX_TK), jnp.bfloat16),
            pltpu.VMEM((8, LANES), jnp.float32),
        ],
        compiler_params=pltpu.CompilerParams(
            dimension_semantics=("arbitrary", "arbitrary", "arbitrary"), vmem_limit_bytes=VMEM_LIMIT),
        name="fox",
    )(proj, proj, proj, proj, gcol)


def _merge_kernel(ya_ref, yb_ref, cu_ref, halo_ref, cz_ref, gates_ref, x_ref, pw_ref, ps_ref,
                  wb_ref, wo_ref, fg_ref, o_ref, *, seq, final):
    tm = MERGE_TM
    gd = POOL_GROUP_DIM
    i = pl.program_id(0)
    t0 = (i * tm) % seq
    pos = t0 + lax.broadcasted_iota(jnp.int32, (tm, 1), 0)
    halo = jnp.where(t0 == 0, 0.0, halo_ref[...].astype(jnp.float32))
    u = cu_ref[...].astype(jnp.float32)
    ext = jnp.concatenate([halo, u], axis=0)

    ys = []
    for g, win in enumerate(POOL_WINDOWS):
        sl = slice(g * gd, (g + 1) * gd)
        acc = ext[:, sl]
        step = 1
        while step < win:
            acc = acc + pltpu.roll(acc, step, 0)
            step *= 2
        cnt = jnp.minimum(pos + 1, win).astype(jnp.float32)
        d = acc[HALO:, :] / cnt - u[:, sl]
        ys.append(jnp.dot(d.astype(jnp.bfloat16), pw_ref[g], preferred_element_type=jnp.float32))
    yc = jnp.concatenate(ys, axis=1) * ps_ref[...]
    yc = yc * _silu(cz_ref[...].astype(jnp.float32))

    merged = jnp.zeros((tm, D_MODEL), jnp.float32)
    for n, y in enumerate((ya_ref[...], yb_ref[...], yc.astype(jnp.bfloat16))):
        yb = jnp.dot(y, wb_ref[n], preferred_element_type=jnp.float32)
        gate = _sigmoid(gates_ref[:, n * D_MODEL:(n + 1) * D_MODEL].astype(jnp.float32))
        merged = merged + gate * yb
    out = x_ref[...] + jnp.dot(merged.astype(jnp.bfloat16), wo_ref[...],
                               preferred_element_type=jnp.float32)
    if final:
        r = lax.rsqrt(jnp.mean(out * out, axis=-1, keepdims=True) + EPS)
        out = (out * r) * fg_ref[...]
    o_ref[...] = out


def _merge(ya, yb, proj, x2, pool_w, pool_scale, w_branch, w_out, final_g, seq, final, layer):
    m = x2.shape[0]
    tm = MERGE_TM
    hpt = tm // HALO

    def rowblk(c, width=1):
        return pl.BlockSpec((tm, width * D_MODEL), lambda i, c=c: (i, c))

    const2 = lambda i: (0, 0)
    layer3 = lambda i: (layer, 0, 0)
    layer4 = lambda i: (layer, 0, 0, 0)
    return pl.pallas_call(
        functools.partial(_merge_kernel, seq=seq, final=final),
        grid=(m // tm,),
        in_specs=[
            rowblk(0), rowblk(0), rowblk(C_CU),
            pl.BlockSpec((HALO, D_MODEL), lambda i: (jnp.maximum(i * hpt - 1, 0), C_CU)),
            rowblk(C_CZ),
            pl.BlockSpec((tm, N_BRANCH * D_MODEL), lambda i: (i, C_GATES // N_BRANCH)),
            rowblk(0),
            pl.BlockSpec((None, POOL_GROUPS, POOL_GROUP_DIM, POOL_GROUP_DIM), layer4),
            pl.BlockSpec((1, D_MODEL), const2),
            pl.BlockSpec((None, N_BRANCH, D_MODEL, D_MODEL), layer4),
            pl.BlockSpec((None, D_MODEL, D_MODEL), layer3),
            pl.BlockSpec((1, D_MODEL), const2),
        ],
        out_specs=pl.BlockSpec((tm, D_MODEL), lambda i: (i, 0)),
        out_shape=jax.ShapeDtypeStruct((m, D_MODEL), jnp.float32),
        compiler_params=pltpu.CompilerParams(
            dimension_semantics=("arbitrary",), vmem_limit_bytes=VMEM_LIMIT),
        name="merge",
    )(ya, yb, proj, proj, proj, proj, x2, pool_w, pool_scale, w_branch, w_out, final_g)


O_AI = 5 * D_MODEL
O_BQ = O_AI + 2 * ML_HEADS
O_BF = O_BQ + 4 * D_MODEL
O_CU = O_BF + FOX_HEADS
O_G = O_CU + 2 * D_MODEL
N_IN = O_G + N_BRANCH * D_MODEL
W_SEGMENTS = ((O_G, N_BRANCH), (0, 5), (O_BQ, 4), (O_CU, 2))
W_PREP_ROWS = 128


def _w_prep_kernel(wt_ref, main_ref, gate_ref):
    dst = 0
    for src, nblk in W_SEGMENTS:
        for n in range(nblk):
            a = src + n * D_MODEL
            blk = wt_ref[a:a + D_MODEL, :].T
            if dst == C_BQ * D_MODEL:
                blk = blk * (FOX_HEAD_DIM ** -0.5 * LOG2E)
            main_ref[:, dst:dst + D_MODEL] = blk.astype(jnp.bfloat16)
            dst += D_MODEL
    n_gate = 2 * ML_HEADS + FOX_HEADS
    gate = jnp.concatenate([wt_ref[O_AI:O_BQ, :], wt_ref[O_BF:O_CU, :],
                            jnp.zeros((LANES - n_gate, W_PREP_ROWS), jnp.float32)], axis=0)
    gate_ref[...] = gate.T.astype(jnp.bfloat16)


def _split_w_in(w_in):
    depth = w_in.shape[0]
    assert w_in.shape[1:] == (D_MODEL, N_IN)
    return pl.pallas_call(
        _w_prep_kernel,
        grid=(depth, D_MODEL // W_PREP_ROWS),
        in_specs=[pl.BlockSpec((None, N_IN, W_PREP_ROWS), lambda l, r: (l, 0, r))],
        out_specs=[
            pl.BlockSpec((None, W_PREP_ROWS, N_MAIN), lambda l, r: (l, r, 0)),
            pl.BlockSpec((None, W_PREP_ROWS, LANES), lambda l, r: (l, r, 0)),
        ],
        out_shape=[
            jax.ShapeDtypeStruct((depth, D_MODEL, N_MAIN), jnp.bfloat16),
            jax.ShapeDtypeStruct((depth, D_MODEL, LANES), jnp.bfloat16),
        ],
        compiler_params=pltpu.CompilerParams(
            dimension_semantics=("arbitrary", "arbitrary"), vmem_limit_bytes=VMEM_LIMIT),
        name="w_prep",
    )(jnp.swapaxes(w_in, 1, 2))


def kernel(x, norm_g, w_in, conv_w, ml_bi, ml_bf, ml_norm_g, fox_bf, pool_w, pool_scale, w_branch,
           w_out, final_g):
    batch, seq, d = x.shape
    depth = norm_g.shape[0]
    assert d == D_MODEL and all(seq % t == 0 for t in (IN_TM, MERGE_TM, ML_CHUNK, GATES_ROWS))
    assert GATES_ROWS % ML_CHUNK == 0
    x2 = x.reshape(batch * seq, d)
    w_main, w_gate = _split_w_in(w_in)
    pool_wb = pool_w.astype(jnp.bfloat16)
    w_branchb = w_branch.astype(jnp.bfloat16)
    w_outb = w_out.astype(jnp.bfloat16)
    for l in range(depth):
        gate_bias = jnp.pad(jnp.concatenate([ml_bi[l], ml_bf[l], fox_bf[l]]),
                            (0, LANES - 2 * ML_HEADS - FOX_HEADS)).reshape(1, LANES)
        proj, gate_pre = _in_proj(x2, norm_g[l].reshape(1, d), w_main, w_gate, l)
        gcol, grow = _gates(gate_pre, gate_bias, batch, seq)
        ya = _mlstm(proj, conv_w[l], gcol, grow, ml_norm_g[l].reshape(1, d), batch, seq)
        yb = _fox(proj, gcol, batch, seq)
        x2 = _merge(ya, yb, proj, x2, pool_wb, pool_scale[l].reshape(1, d), w_branchb, w_outb,
                    final_g.reshape(1, d), seq, final=(l == depth - 1), layer=l)
    return x2.reshape(batch, seq, d)
```

```python
import functools

import jax
import jax.numpy as jnp
from jax import lax
from jax.experimental import pallas as pl
from jax.experimental.pallas import tpu as pltpu

D_MODEL = 1024
ML_HEADS = 4
ML_HEAD_DIM = 256
CONV_WIDTH = 4
FOX_HEADS = 8
FOX_HEAD_DIM = 128
POOL_GROUPS = 4
POOL_GROUP_DIM = 256
POOL_WINDOWS = (2, 4, 8, 16)
N_BRANCH = 3
EPS = 1e-6

LANES = 128
HALO = 16
CONV_HALO = 8
NEG = -1e30
FIXED_REF_SLACK = 100.0
SKIP_MARGIN = 32.0

C_GATES = 0
C_AQ, C_AK, C_AV, C_AO, C_AZ, C_BQ, C_BK, C_BV, C_BZ, C_CU, C_CZ = range(N_BRANCH, N_BRANCH + 11)
N_MAIN = 14 * D_MODEL
G_AI, G_AF, G_BF = 0, 4, 8
G_FHI, G_FMID, G_FLO, G_ONE = 16, 24, 32, 127

ML_CHUNK = 256
IN_TM, IN_TN = 2048, 1024
FOX_TQ = 256
FOX_TK = 256
FOX_PAIR = 16
GATES_ROWS = 1024
LOG2E = 1.4426950408889634
MERGE_TM = 512
VMEM_LIMIT = 56 * 1024 * 1024


def _sigmoid(x):
    return 1.0 / (1.0 + jnp.exp(-x))


def _silu(x):
    return x * _sigmoid(x)


def _in_proj_kernel(x_ref, g_ref, w_ref, wg_ref, proj_ref, gate_ref, h_scr):
    @pl.when(pl.program_id(1) == 0)
    def _():
        x = x_ref[...]
        r = lax.rsqrt(jnp.mean(x * x, axis=-1, keepdims=True) + EPS)
        h = ((x * r) * g_ref[...]).astype(jnp.bfloat16)
        h_scr[...] = h
        gate_ref[...] = lax.dot_general(h, wg_ref[...], (((1,), (1,)), ((), ())),
                                        preferred_element_type=jnp.float32)

    proj_ref[...] = lax.dot_general(h_scr[...], w_ref[0].astype(jnp.bfloat16), (((1,), (1,)), ((), ())),
                                    preferred_element_type=jnp.float32).astype(jnp.bfloat16)


def _in_proj(x2, norm_g, wt, w_gate, layer):
    m = x2.shape[0]
    return pl.pallas_call(
        _in_proj_kernel,
        grid=(m // IN_TM, N_MAIN // IN_TN),
        in_specs=[
            pl.BlockSpec((IN_TM, D_MODEL), lambda i, j: (i, 0)),
            pl.BlockSpec((1, D_MODEL), lambda i, j: (0, 0)),
            pl.BlockSpec((pl.Element(1), pl.Element(IN_TN), pl.Element(D_MODEL)),
                         lambda i, j: (layer, pl.multiple_of(_w_row(j), 8), 0)),
            pl.BlockSpec((None, LANES, D_MODEL), lambda i, j: (layer, 0, 0)),
        ],
        out_specs=[
            pl.BlockSpec((IN_TM, IN_TN), lambda i, j: (i, j)),
            pl.BlockSpec((IN_TM, LANES), lambda i, j: (i, 0)),
        ],
        out_shape=[
            jax.ShapeDtypeStruct((m, N_MAIN), jnp.bfloat16),
            jax.ShapeDtypeStruct((m, LANES), jnp.float32),
        ],
        scratch_shapes=[pltpu.VMEM((IN_TM, D_MODEL), jnp.bfloat16)],
        compiler_params=pltpu.CompilerParams(
            dimension_semantics=("arbitrary", "arbitrary"), vmem_limit_bytes=VMEM_LIMIT),
        name="in_proj",
    )(x2, norm_g, wt, w_gate)


def _gates_kernel(g_ref, bias_ref, col_ref, row_ref, carry_scr):
    @pl.when(pl.program_id(1) == 0)
    def _():
        carry_scr[...] = jnp.zeros_like(carry_scr)

    row = lax.broadcasted_iota(jnp.int32, (ML_CHUNK, LANES), 0)
    lane = lax.broadcasted_iota(jnp.int32, (ML_CHUNK, LANES), 1)
    carry = carry_scr[...]
    for c in range(GATES_ROWS // ML_CHUNK):
        rs = slice(c * ML_CHUNK, (c + 1) * ML_CHUNK)
        pre = g_ref[rs, :] + bias_ref[...]
        ls = jnp.minimum(pre, 0.0) - jnp.log1p(jnp.exp(-jnp.abs(pre)))
        cs = ls
        k = 1
        while k < ML_CHUNK:
            cs = cs + jnp.where(row >= k, pltpu.roll(cs, k, 0), 0.0)
            k *= 2
        run = cs + carry
        carry = run[ML_CHUNK - 1:ML_CHUNK, :]
        out = jnp.where(lane < G_AF, pre, jnp.where(lane < G_BF, cs, run))
        row_ref[0, :, rs] = out.T[0:16, :]
        f2 = run * LOG2E
        hi = f2.astype(jnp.bfloat16).astype(jnp.float32)
        r1 = f2 - hi
        mid = r1.astype(jnp.bfloat16).astype(jnp.float32)
        lo = r1 - mid
        pieces = jnp.where(lane < G_FMID, pltpu.roll(hi, G_FHI - G_BF, 1),
                           jnp.where(lane < G_FLO, pltpu.roll(mid, G_FMID - G_BF, 1),
                                     jnp.where(lane < G_FLO + FOX_HEADS, pltpu.roll(lo, G_FLO - G_BF, 1),
                                               jnp.where(lane == G_ONE, 1.0, 0.0))))
        col_ref[rs, :] = jnp.where(lane < G_FHI, out, pieces)
    carry_scr[...] = carry


def _gates(gate_pre, gate_bias, batch, seq):
    m = gate_pre.shape[0]
    nc = seq // GATES_ROWS
    return pl.pallas_call(
        _gates_kernel,
        grid=(batch, nc),
        in_specs=[
            pl.BlockSpec((GATES_ROWS, LANES), lambda b, c: (b * nc + c, 0)),
            pl.BlockSpec((1, LANES), lambda b, c: (0, 0)),
        ],
        out_specs=[
            pl.BlockSpec((GATES_ROWS, LANES), lambda b, c: (b * nc + c, 0)),
            pl.BlockSpec((1, 16, GATES_ROWS), lambda b, c: (b, 0, c)),
        ],
        out_shape=[
            jax.ShapeDtypeStruct((m, LANES), jnp.float32),
            jax.ShapeDtypeStruct((batch, 16, seq), jnp.float32),
        ],
        scratch_shapes=[pltpu.VMEM((1, LANES), jnp.float32)],
        compiler_params=pltpu.CompilerParams(dimension_semantics=("arbitrary", "arbitrary")),
        name="gates",
    )(gate_pre, gate_bias)


def _mlstm_kernel(aq_ref, ak_ref, av_ref, ao_ref, az_ref, cw_ref, gc_ref, gr_ref, ng_ref,
                  y_ref, qbuf, kbuf, sh_scr, ct_scr, m_scr):
    L = ML_CHUNK
    dh = ML_HEAD_DIM

    n_sh = CONV_WIDTH - 1

    @pl.when(pl.program_id(1) == 0)
    def _():
        qbuf[...] = jnp.zeros_like(qbuf)
        kbuf[...] = jnp.zeros_like(kbuf)
        ct_scr[...] = jnp.zeros_like(ct_scr)
        m_scr[...] = jnp.zeros_like(m_scr)
        r = lax.broadcasted_iota(jnp.int32, (n_sh * L, L), 0)
        c = lax.broadcasted_iota(jnp.int32, (n_sh * L, L), 1)
        hit = functools.reduce(jnp.logical_or,
                               [(r >= j * L) & (r < (j + 1) * L) & (c == r - j * L - (j + 1)) for j in range(n_sh)])
        sh_scr[...] = hit.astype(jnp.float32).astype(jnp.bfloat16)

    row8 = lax.broadcasted_iota(jnp.int32, (CONV_HALO, D_MODEL), 0)

    def conv(x_ref, prev, w):
        xb = x_ref[...]
        x = xb.astype(jnp.float32)
        shifted = jnp.dot(sh_scr[...], xb, preferred_element_type=jnp.float32)
        acc = x * w[n_sh:n_sh + 1, :]
        head = jnp.zeros((CONV_HALO, D_MODEL), jnp.float32)
        tail = prev[...]
        for j in range(1, n_sh + 1):
            wj = w[n_sh - j:n_sh - j + 1, :]
            acc = acc + shifted[(j - 1) * L:j * L, :] * wj
            head = head + jnp.where(row8 < j, pltpu.roll(tail, j, 0), 0.0) * wj
        prev[...] = x[L - CONV_HALO:, :]
        return jnp.concatenate([acc[:CONV_HALO] + head, acc[CONV_HALO:]], axis=0)

    cw = cw_ref[...]
    qc = (_silu(conv(aq_ref, qbuf, cw[:, :D_MODEL])) * (dh ** -0.5)).astype(jnp.bfloat16)
    kc = _silu(conv(ak_ref, kbuf, cw[:, D_MODEL:]))

    gcol = gc_ref[...]
    grow = gr_ref[0]
    t_idx = lax.broadcasted_iota(jnp.int32, (L, L), 0)
    s_idx = lax.broadcasted_iota(jnp.int32, (L, L), 1)
    causal = s_idx <= t_idx
    ones_col = (lax.broadcasted_iota(jnp.int32, (L, LANES), 1) == 0).astype(jnp.bfloat16)

    for h in range(ML_HEADS):
        sl = slice(h * dh, (h + 1) * dh)
        q = qc[:, sl]
        kf = kc[:, sl]
        kb = kf.astype(jnp.bfloat16)
        v_aug = jnp.concatenate([av_ref[:, sl], ones_col], axis=1)
        i_col = gcol[:, G_AI + h:G_AI + h + 1]
        b_col = gcol[:, G_AF + h:G_AF + h + 1]
        i_row = grow[G_AI + h:G_AI + h + 1, :]
        b_row = grow[G_AF + h:G_AF + h + 1, :]
        m_prev = m_scr[h:h + 1, 0:1]

        d = jnp.where(causal, b_col + (i_row - b_row), NEG)
        inter = b_col + m_prev
        m_t = jnp.maximum(inter, jnp.max(d, axis=-1, keepdims=True))
        w_inter = jnp.exp(inter - m_t)
        e = jnp.exp(d - m_t)
        s = lax.dot_general(q, kb, (((1,), (1,)), ((), ())), preferred_element_type=jnp.float32)
        p = (e * s).astype(jnp.bfloat16)
        ct = ct_scr[h]
        num_aug = (w_inter * jnp.dot(q, ct.astype(jnp.bfloat16), preferred_element_type=jnp.float32)
                   + jnp.dot(p, v_aug, preferred_element_type=jnp.float32))
        num = num_aug[:, :dh]
        den = num_aug[:, dh:dh + 1]
        hh = num / jnp.maximum(jnp.abs(den), jnp.exp(-m_t))
        hh = hh * lax.rsqrt(jnp.mean(hh * hh, axis=-1, keepdims=True) + EPS)
        y = (hh * ng_ref[:, sl] * _sigmoid(ao_ref[:, sl].astype(jnp.float32))
             * _silu(az_ref[:, sl].astype(jnp.float32)))
        y_ref[:, sl] = y.astype(jnp.bfloat16)

        b_last = b_col[L - 1:L, :]
        dec = b_last - b_col + i_col
        m_new = jnp.maximum(b_last + m_prev, jnp.max(dec, axis=0, keepdims=True))
        w_s = jnp.exp(dec - m_new)
        w_old = jnp.exp(b_last + m_prev - m_new)
        upd = jnp.dot((kf * w_s).T.astype(jnp.bfloat16), v_aug, preferred_element_type=jnp.float32)
        ct_scr[h] = w_old * ct + upd
        m_scr[h:h + 1, :] = jnp.broadcast_to(m_new, (1, LANES))


def _mlstm(proj, conv_w, gcol, grow, ml_norm_g, batch, seq):
    m = proj.shape[0]
    L = ML_CHUNK
    nc = seq // L

    def col(c):
        return pl.BlockSpec((L, D_MODEL), lambda b, i, c=c: (b * nc + i, c))

    return pl.pallas_call(
        _mlstm_kernel,
        grid=(batch, nc),
        in_specs=[
            col(C_AQ), col(C_AK), col(C_AV), col(C_AO), col(C_AZ),
            pl.BlockSpec((CONV_WIDTH, 2 * D_MODEL), lambda b, i: (0, 0)),
            pl.BlockSpec((L, LANES), lambda b, i: (b * nc + i, 0)),
            pl.BlockSpec((1, 16, L), lambda b, i: (b, 0, i)),
            pl.BlockSpec((1, D_MODEL), lambda b, i: (0, 0)),
        ],
        out_specs=pl.BlockSpec((L, D_MODEL), lambda b, i: (b * nc + i, 0)),
        out_shape=jax.ShapeDtypeStruct((m, D_MODEL), jnp.bfloat16),
        scratch_shapes=[
            pltpu.VMEM((CONV_HALO, D_MODEL), jnp.float32),
            pltpu.VMEM((CONV_HALO, D_MODEL), jnp.float32),
            pltpu.VMEM(((CONV_WIDTH - 1) * L, L), jnp.bfloat16),
            pltpu.VMEM((ML_HEADS, ML_HEAD_DIM, ML_HEAD_DIM + LANES), jnp.float32),
            pltpu.VMEM((8, LANES), jnp.float32),
        ],
        compiler_params=pltpu.CompilerParams(
            dimension_semantics=("arbitrary", "arbitrary"), vmem_limit_bytes=VMEM_LIMIT),
        name="mlstm",
    )(proj, proj, proj, proj, proj, conv_w, gcol, grow, ml_norm_g)


def _fox_kernel(q_ref, k_ref, v_ref, z_ref, g_ref, y_ref, qa_scr, ka_scr, vt_scr, st_scr, *, seq):
    tq, tk = FOX_TQ, FOX_TK
    h = pl.program_id(1)
    i = pl.program_id(2)
    lane1 = lax.broadcasted_iota(jnp.int32, (1, LANES), 1)

    @pl.when(i == 0)
    def _():
        src = lax.broadcasted_iota(jnp.int32, (LANES, LANES), 0)
        dst = lax.broadcasted_iota(jnp.int32, (LANES, LANES), 1)
        piece = jnp.where(src == G_FHI + h, 0, jnp.where(src == G_FMID + h, 1,
                                                         jnp.where(src == G_FLO + h, 2, -1)))
        is_piece = piece >= 0
        one = src == G_ONE
        pq = ((is_piece & ((dst == piece) | (dst == 6))) | (one & (dst >= 3) & (dst < 6)))
        pk = ((one & ((dst < 3) | (dst == 7))).astype(jnp.float32)
              - (is_piece & (dst == piece + 3)).astype(jnp.float32))
        pq = pq.astype(jnp.float32).astype(jnp.bfloat16)
        pk = pk.astype(jnp.bfloat16)

        lane_k = lax.broadcasted_iota(jnp.int32, (tk, 2 * FOX_HEAD_DIM), 1)
        ka_scr[0:tk, :] = jnp.where(lane_k == FOX_HEAD_DIM + 3, NEG, 0.0).astype(jnp.bfloat16)
        vt_scr[0] = jnp.zeros((FOX_HEAD_DIM, tk), jnp.bfloat16)

        lane_q = lax.broadcasted_iota(jnp.int32, (tk, LANES), 1)

        def build(c, carry):
            kmax, qn_vec, g_vec, fe_vec, sl_vec = carry
            r0 = pl.multiple_of(c * tk, tk)
            g = g_ref[pl.ds(r0, tk), :].astype(jnp.bfloat16)
            qx = jnp.dot(g, pq, preferred_element_type=jnp.float32)
            kx = jnp.dot(g, pk, preferred_element_type=jnp.float32)
            f = qx[:, 6:7]
            qs = (q_ref[pl.ds(r0, tk), :].astype(jnp.float32)
                  * (FOX_HEAD_DIM ** -0.5 * LOG2E)).astype(jnp.bfloat16)
            kk = k_ref[pl.ds(r0, tk), :]

            qf = qs.astype(jnp.float32)
            kf = kk.astype(jnp.float32)
            qn = jnp.sum(qf * qf, axis=1, keepdims=True)
            kn = jnp.sum(kf * kf, axis=1, keepdims=True)
            diag = jnp.sum(qf * kf, axis=1, keepdims=True)
            here = lane1 == c
            kmax = jnp.maximum(kmax, jnp.max(kn, axis=0, keepdims=True))
            qmax = jnp.max(qn, axis=0, keepdims=True)
            qn_vec = jnp.where(here, qmax, qn_vec)
            g_vec = jnp.where(here, jnp.max(f - diag, axis=0, keepdims=True), g_vec)
            fe_vec = jnp.where(here, f[tk - 1:tk, :], fe_vec)
            ref = (-jnp.sqrt(qmax * kmax)).astype(jnp.bfloat16).astype(jnp.float32)
            qx = jnp.where(lane_q == 7, ref, qx)
            sl_vec = jnp.where(here, -ref - jnp.min(diag, axis=0, keepdims=True), sl_vec)

            qa_scr[pl.ds(r0, tk), :] = jnp.concatenate([qs, qx.astype(jnp.bfloat16)], axis=1)
            ka_scr[pl.ds(r0 + tk, tk), :] = jnp.concatenate([kk, kx.astype(jnp.bfloat16)], axis=1)
            vt_scr[c + 1] = v_ref[pl.ds(r0, tk), :].astype(jnp.float32).T.astype(jnp.bfloat16)
            return kmax, qn_vec, g_vec, fe_vec, sl_vec

        zrow = jnp.zeros((1, LANES), jnp.float32)
        kmax, qn_vec, g_vec, fe_vec, sl_vec = lax.fori_loop(
            0, seq // tk, build, (jnp.zeros((1, 1), jnp.float32), zrow, zrow, zrow, zrow), unroll=4)
        st_scr[2:3, :] = sl_vec
        st_scr[0:1, :] = jnp.sqrt(qn_vec * kmax) + g_vec
        st_scr[1:2, :] = fe_vec

    def back_blocks(blk):
        bound = jnp.sum(jnp.where(lane1 == blk, st_scr[0:1, :], 0.0), axis=1, keepdims=True)
        needed = (lane1 < blk) & jnp.logical_not(st_scr[1:2, :] >= bound + SKIP_MARGIN * LOG2E)
        return blk - jnp.min(jnp.where(needed, lane1, blk))

    blks = [i * FOX_PAIR + u for u in range(FOX_PAIR)]
    n_back = functools.reduce(jnp.maximum, [back_blocks(b) for b in blks])
    qas = [qa_scr[pl.ds(pl.multiple_of(b * tq, tq), tq), :] for b in blks]

    def scores(t, diagonal):
        kbs, parts = [], []
        for u in range(FOX_PAIR):
            kb = jnp.maximum(blks[u] - t + 1, 0)
            k0 = pl.multiple_of(kb * tk, tk)
            s = lax.dot_general(ka_scr[pl.ds(k0, tk), :], qas[u], (((1,), (1,)), ((), ())),
                                preferred_element_type=jnp.float32)
            if diagonal:
                key_idx = lax.broadcasted_iota(jnp.int32, (tk, tq), 0)
                qry_idx = lax.broadcasted_iota(jnp.int32, (tk, tq), 1)
                s = jnp.where(key_idx <= qry_idx, s, NEG)
            kbs.append(kb)
            parts.append(s)
        return jnp.concatenate(parts, axis=1), kbs

    def weighted_values(p, kbs):
        p = p.astype(jnp.bfloat16)
        return jnp.concatenate(
            [jnp.dot(vt_scr[kbs[u]], p[:, u * tq:(u + 1) * tq], preferred_element_type=jnp.float32)
             for u in range(FOX_PAIR)], axis=1)

    def attend_online(carry, t, diagonal):
        m, l, acc = carry
        s, kbs = scores(t, diagonal)
        m_new = jnp.maximum(m, jnp.max(s, axis=0, keepdims=True))
        alpha = jnp.exp2(m - m_new)
        p = jnp.exp2(s - m_new)
        l = alpha * l + jnp.sum(p, axis=0, keepdims=True)
        return m_new, l, alpha * acc + weighted_values(p, kbs)

    def attend_fixed(carry, t, diagonal):
        l, acc = carry
        s, kbs = scores(t, diagonal)
        p = jnp.exp2(s)
        return l + jnp.sum(p, axis=0, keepdims=True), acc + weighted_values(p, kbs)

    rows = FOX_PAIR * tq
    zl = jnp.zeros((1, rows), jnp.float32)
    zacc = jnp.zeros((FOX_HEAD_DIM, rows), jnp.float32)

    def fixed_path():
        carry = attend_fixed((zl, zacc), 0, True)
        return lax.fori_loop(1, n_back + 1, lambda t, c: attend_fixed(c, t, False), carry)

    def online_path():
        carry = attend_online((jnp.full((1, rows), NEG, jnp.float32), zl, zacc), 0, True)
        _, l, acc = lax.fori_loop(1, n_back + 1, lambda t, c: attend_online(c, t, False), carry)
        return l, acc

    mine = (lane1 >= blks[0]) & (lane1 < blks[0] + FOX_PAIR)
    worst = jnp.max(jnp.where(mine, st_scr[2:3, :], 0.0))
    l, acc = lax.cond(worst <= FIXED_REF_SLACK, fixed_path, online_path)
    y_ref[...] = ((acc / l).T * _silu(z_ref[...].astype(jnp.float32))).astype(jnp.bfloat16)


def _fox(proj, gcol, batch, seq):
    m = proj.shape[0]
    rows = FOX_PAIR * FOX_TQ
    nq = seq // rows
    assert FOX_TQ == FOX_TK and seq % rows == 0 and seq // FOX_TQ <= LANES
    hb = D_MODEL // FOX_HEAD_DIM

    def head(c):
        return pl.BlockSpec((seq, FOX_HEAD_DIM), lambda b, h, i, c=c: (b, c * hb + h))

    return pl.pallas_call(
        functools.partial(_fox_kernel, seq=seq),
        grid=(batch, FOX_HEADS, nq),
        in_specs=[
            head(C_BQ), head(C_BK), head(C_BV),
            pl.BlockSpec((rows, FOX_HEAD_DIM), lambda b, h, i: (b * nq + i, C_BZ * hb + h)),
            pl.BlockSpec((seq, LANES), lambda b, h, i: (b, 0)),
        ],
        out_specs=pl.BlockSpec((rows, FOX_HEAD_DIM), lambda b, h, i: (b * nq + i, h)),
        out_shape=jax.ShapeDtypeStruct((m, D_MODEL), jnp.bfloat16),
        scratch_shapes=[
            pltpu.VMEM((seq, 2 * FOX_HEAD_DIM), jnp.bfloat16),
            pltpu.VMEM((seq + FOX_TK, 2 * FOX_HEAD_DIM), jnp.bfloat16),
            pltpu.VMEM((seq // FOX_TK + 1, FOX_HEAD_DIM, FOX_TK), jnp.bfloat16),
            pltpu.VMEM((8, LANES), jnp.float32),
        ],
        compiler_params=pltpu.CompilerParams(
            dimension_semantics=("arbitrary", "arbitrary", "arbitrary"), vmem_limit_bytes=VMEM_LIMIT),
        name="fox",
    )(proj, proj, proj, proj, gcol)


def _merge_kernel(ya_ref, yb_ref, cu_ref, halo_ref, cz_ref, gates_ref, x_ref, pw_ref, ps_ref,
                  wb_ref, wo_ref, fg_ref, o_ref, *, seq, final):
    tm = MERGE_TM
    gd = POOL_GROUP_DIM
    i = pl.program_id(0)
    t0 = (i * tm) % seq
    pos = t0 + lax.broadcasted_iota(jnp.int32, (tm, 1), 0)
    halo = jnp.where(t0 == 0, 0.0, halo_ref[...].astype(jnp.float32))
    u = cu_ref[...].astype(jnp.float32)
    ext = jnp.concatenate([halo, u], axis=0)

    ys = []
    for g, win in enumerate(POOL_WINDOWS):
        sl = slice(g * gd, (g + 1) * gd)
        acc = ext[:, sl]
        step = 1
        while step < win:
            acc = acc + pltpu.roll(acc, step, 0)
            step *= 2
        cnt = jnp.minimum(pos + 1, win).astype(jnp.float32)
        d = acc[HALO:, :] / cnt - u[:, sl]
        ys.append(jnp.dot(d.astype(jnp.bfloat16), pw_ref[g], preferred_element_type=jnp.float32))
    yc = jnp.concatenate(ys, axis=1) * ps_ref[...]
    yc = yc * _silu(cz_ref[...].astype(jnp.float32))

    merged = jnp.zeros((tm, D_MODEL), jnp.float32)
    for n, y in enumerate((ya_ref[...], yb_ref[...], yc.astype(jnp.bfloat16))):
        yb = jnp.dot(y, wb_ref[n], preferred_element_type=jnp.float32)
        gate = _sigmoid(gates_ref[:, n * D_MODEL:(n + 1) * D_MODEL].astype(jnp.float32))
        merged = merged + gate * yb
    out = x_ref[...] + jnp.dot(merged.astype(jnp.bfloat16), wo_ref[...],
                               preferred_element_type=jnp.float32)
    if final:
        r = lax.rsqrt(jnp.mean(out * out, axis=-1, keepdims=True) + EPS)
        out = (out * r) * fg_ref[...]
    o_ref[...] = out


def _merge(ya, yb, proj, x2, pool_w, pool_scale, w_branch, w_out, final_g, seq, final, layer):
    m = x2.shape[0]
    tm = MERGE_TM
    hpt = tm // HALO

    def rowblk(c, width=1):
        return pl.BlockSpec((tm, width * D_MODEL), lambda i, c=c: (i, c))

    const2 = lambda i: (0, 0)
    layer3 = lambda i: (layer, 0, 0)
    layer4 = lambda i: (layer, 0, 0, 0)
    return pl.pallas_call(
        functools.partial(_merge_kernel, seq=seq, final=final),
        grid=(m // tm,),
        in_specs=[
            rowblk(0), rowblk(0), rowblk(C_CU),
            pl.BlockSpec((HALO, D_MODEL), lambda i: (jnp.maximum(i * hpt - 1, 0), C_CU)),
            rowblk(C_CZ),
            pl.BlockSpec((tm, N_BRANCH * D_MODEL), lambda i: (i, C_GATES // N_BRANCH)),
            rowblk(0),
            pl.BlockSpec((None, POOL_GROUPS, POOL_GROUP_DIM, POOL_GROUP_DIM), layer4),
            pl.BlockSpec((1, D_MODEL), const2),
            pl.BlockSpec((None, N_BRANCH, D_MODEL, D_MODEL), layer4),
            pl.BlockSpec((None, D_MODEL, D_MODEL), layer3),
            pl.BlockSpec((1, D_MODEL), const2),
        ],
        out_specs=pl.BlockSpec((tm, D_MODEL), lambda i: (i, 0)),
        out_shape=jax.ShapeDtypeStruct((m, D_MODEL), jnp.float32),
        compiler_params=pltpu.CompilerParams(
            dimension_semantics=("arbitrary",), vmem_limit_bytes=VMEM_LIMIT),
        name="merge",
    )(ya, yb, proj, proj, proj, proj, x2, pool_w, pool_scale, w_branch, w_out, final_g)


O_AI = 5 * D_MODEL
O_BQ = O_AI + 2 * ML_HEADS
O_BF = O_BQ + 4 * D_MODEL
O_CU = O_BF + FOX_HEADS
O_G = O_CU + 2 * D_MODEL
N_IN = O_G + N_BRANCH * D_MODEL
assert all(o % 8 == 0 for o in (O_BQ, O_CU, O_G))


def _w_row(j):
    return jnp.where(j < C_AQ, O_G + j * D_MODEL,
                     jnp.where(j < C_BQ, (j - C_AQ) * D_MODEL,
                               jnp.where(j < C_CU, O_BQ + (j - C_BQ) * D_MODEL,
                                         O_CU + (j - C_CU) * D_MODEL)))


def _split_w_in(w_in):
    assert w_in.shape[1:] == (D_MODEL, N_IN)
    wt = jnp.swapaxes(w_in, 1, 2)
    w_gate = jnp.concatenate([wt[:, O_AI:O_BQ], wt[:, O_BF:O_CU]], axis=1)
    w_gate = jnp.pad(w_gate, ((0, 0), (0, LANES - w_gate.shape[1]), (0, 0)))
    return wt, w_gate.astype(jnp.bfloat16)


def kernel(x, norm_g, w_in, conv_w, ml_bi, ml_bf, ml_norm_g, fox_bf, pool_w, pool_scale, w_branch,
           w_out, final_g):
    batch, seq, d = x.shape
    depth = norm_g.shape[0]
    assert d == D_MODEL and all(seq % t == 0 for t in (IN_TM, MERGE_TM, ML_CHUNK, GATES_ROWS))
    assert GATES_ROWS % ML_CHUNK == 0
    x2 = x.reshape(batch * seq, d)
    assert IN_TN == D_MODEL
    wt, w_gate = _split_w_in(w_in)
    pool_wb = pool_w.astype(jnp.bfloat16)
    w_branchb = w_branch.astype(jnp.bfloat16)
    w_outb = w_out.astype(jnp.bfloat16)
    for l in range(depth):
        gate_bias = jnp.pad(jnp.concatenate([ml_bi[l], ml_bf[l], fox_bf[l]]),
                            (0, LANES - 2 * ML_HEADS - FOX_HEADS)).reshape(1, LANES)
        proj, gate_pre = _in_proj(x2, norm_g[l].reshape(1, d), wt, w_gate, l)
        gcol, grow = _gates(gate_pre, gate_bias, batch, seq)
        ya = _mlstm(proj, conv_w[l], gcol, grow, ml_norm_g[l].reshape(1, d), batch, seq)
        yb = _fox(proj, gcol, batch, seq)
        x2 = _merge(ya, yb, proj, x2, pool_wb, pool_scale[l].reshape(1, d), w_branchb, w_outb,
                    final_g.reshape(1, d), seq, final=(l == depth - 1), layer=l)
    return x2.reshape(batch, seq, d)
```

```python
import functools

import jax
import jax.numpy as jnp
from jax import lax
from jax.experimental import pallas as pl
from jax.experimental.pallas import tpu as pltpu

D_MODEL = 1024
ML_HEADS = 4
ML_HEAD_DIM = 256
CONV_WIDTH = 4
FOX_HEADS = 8
FOX_HEAD_DIM = 128
POOL_GROUPS = 4
POOL_GROUP_DIM = 256
POOL_WINDOWS = (2, 4, 8, 16)
N_BRANCH = 3
EPS = 1e-6

LANES = 128
HALO = 16
CONV_HALO = 8
NEG = -1e30
FIXED_REF_SLACK = 100.0
SKIP_MARGIN = 32.0

C_GATES = 0
C_AQ, C_AK, C_AV, C_AO, C_AZ, C_BQ, C_BK, C_BV, C_BZ, C_CU, C_CZ = range(N_BRANCH, N_BRANCH + 11)
N_MAIN = 14 * D_MODEL
G_AI, G_AF, G_BF = 0, 4, 8
G_FHI, G_FMID, G_FLO, G_ONE = 16, 24, 32, 127

ML_CHUNK = 256
IN_TM, IN_TN = 2048, 1024
FOX_TQ = 256
FOX_TK = 256
FOX_PAIR = 16
GATES_ROWS = 1024
LOG2E = 1.4426950408889634
MERGE_TM = 512
VMEM_LIMIT = 56 * 1024 * 1024


def _sigmoid(x):
    return 1.0 / (1.0 + jnp.exp(-x))


def _silu(x):
    return x * _sigmoid(x)


def _in_proj_kernel(x_ref, g_ref, w_ref, wga_ref, wgb_ref, proj_ref, gate_ref, h_scr):
    @pl.when(pl.program_id(1) == 0)
    def _():
        x = x_ref[...]
        r = lax.rsqrt(jnp.mean(x * x, axis=-1, keepdims=True) + EPS)
        h = ((x * r) * g_ref[...]).astype(jnp.bfloat16)
        h_scr[...] = h
        n_gate = 2 * ML_HEADS + FOX_HEADS
        wg = jnp.concatenate([wga_ref[0], wgb_ref[0], jnp.zeros((LANES - n_gate, D_MODEL), jnp.float32)],
                             axis=0).astype(jnp.bfloat16)
        gate_ref[...] = lax.dot_general(h, wg, (((1,), (1,)), ((), ())), preferred_element_type=jnp.float32)

    proj_ref[...] = lax.dot_general(h_scr[...], w_ref[0].astype(jnp.bfloat16), (((1,), (1,)), ((), ())),
                                    preferred_element_type=jnp.float32).astype(jnp.bfloat16)


def _in_proj(x2, norm_g, wt, layer):
    m = x2.shape[0]
    assert 2 * ML_HEADS == 8 and FOX_HEADS == 8

    def gate_rows(row):
        return pl.BlockSpec((pl.Element(1), pl.Element(8), pl.Element(D_MODEL)), lambda i, j: (layer, row, 0))

    return pl.pallas_call(
        _in_proj_kernel,
        grid=(m // IN_TM, N_MAIN // IN_TN),
        in_specs=[
            pl.BlockSpec((IN_TM, D_MODEL), lambda i, j: (i, 0)),
            pl.BlockSpec((1, D_MODEL), lambda i, j: (0, 0)),
            pl.BlockSpec((pl.Element(1), pl.Element(IN_TN), pl.Element(D_MODEL)),
                         lambda i, j: (layer, pl.multiple_of(_w_row(j), 8), 0)),
            gate_rows(O_AI), gate_rows(O_BF),
        ],
        out_specs=[
            pl.BlockSpec((IN_TM, IN_TN), lambda i, j: (i, j)),
            pl.BlockSpec((IN_TM, LANES), lambda i, j: (i, 0)),
        ],
        out_shape=[
            jax.ShapeDtypeStruct((m, N_MAIN), jnp.bfloat16),
            jax.ShapeDtypeStruct((m, LANES), jnp.float32),
        ],
        scratch_shapes=[pltpu.VMEM((IN_TM, D_MODEL), jnp.bfloat16)],
        compiler_params=pltpu.CompilerParams(
            dimension_semantics=("arbitrary", "arbitrary"), vmem_limit_bytes=VMEM_LIMIT),
        name="in_proj",
    )(x2, norm_g, wt, wt, wt)


def _gates_kernel(g_ref, bias_ref, col_ref, row_ref, carry_scr):
    @pl.when(pl.program_id(1) == 0)
    def _():
        carry_scr[...] = jnp.zeros_like(carry_scr)

    row = lax.broadcasted_iota(jnp.int32, (ML_CHUNK, LANES), 0)
    lane = lax.broadcasted_iota(jnp.int32, (ML_CHUNK, LANES), 1)
    carry = carry_scr[...]
    for c in range(GATES_ROWS // ML_CHUNK):
        rs = slice(c * ML_CHUNK, (c + 1) * ML_CHUNK)
        pre = g_ref[rs, :] + bias_ref[...]
        ls = jnp.minimum(pre, 0.0) - jnp.log1p(jnp.exp(-jnp.abs(pre)))
        cs = ls
        k = 1
        while k < ML_CHUNK:
            cs = cs + jnp.where(row >= k, pltpu.roll(cs, k, 0), 0.0)
            k *= 2
        run = cs + carry
        carry = run[ML_CHUNK - 1:ML_CHUNK, :]
        out = jnp.where(lane < G_AF, pre, jnp.where(lane < G_BF, cs, run))
        row_ref[0, :, rs] = out.T[0:16, :]
        f2 = run * LOG2E
        hi = f2.astype(jnp.bfloat16).astype(jnp.float32)
        r1 = f2 - hi
        mid = r1.astype(jnp.bfloat16).astype(jnp.float32)
        lo = r1 - mid
        pieces = jnp.where(lane < G_FMID, pltpu.roll(hi, G_FHI - G_BF, 1),
                           jnp.where(lane < G_FLO, pltpu.roll(mid, G_FMID - G_BF, 1),
                                     jnp.where(lane < G_FLO + FOX_HEADS, pltpu.roll(lo, G_FLO - G_BF, 1),
                                               jnp.where(lane == G_ONE, 1.0, 0.0))))
        col_ref[rs, :] = jnp.where(lane < G_FHI, out, pieces)
    carry_scr[...] = carry


def _gates(gate_pre, gate_bias, batch, seq):
    m = gate_pre.shape[0]
    nc = seq // GATES_ROWS
    return pl.pallas_call(
        _gates_kernel,
        grid=(batch, nc),
        in_specs=[
            pl.BlockSpec((GATES_ROWS, LANES), lambda b, c: (b * nc + c, 0)),
            pl.BlockSpec((1, LANES), lambda b, c: (0, 0)),
        ],
        out_specs=[
            pl.BlockSpec((GATES_ROWS, LANES), lambda b, c: (b * nc + c, 0)),
            pl.BlockSpec((1, 16, GATES_ROWS), lambda b, c: (b, 0, c)),
        ],
        out_shape=[
            jax.ShapeDtypeStruct((m, LANES), jnp.float32),
            jax.ShapeDtypeStruct((batch, 16, seq), jnp.float32),
        ],
        scratch_shapes=[pltpu.VMEM((1, LANES), jnp.float32)],
        compiler_params=pltpu.CompilerParams(dimension_semantics=("arbitrary", "arbitrary")),
        name="gates",
    )(gate_pre, gate_bias)


def _mlstm_kernel(aq_ref, ak_ref, av_ref, ao_ref, az_ref, cw_ref, gc_ref, gr_ref, ng_ref,
                  y_ref, qbuf, kbuf, sh_scr, ct_scr, m_scr):
    L = ML_CHUNK
    dh = ML_HEAD_DIM

    n_sh = CONV_WIDTH - 1

    @pl.when(pl.program_id(1) == 0)
    def _():
        qbuf[...] = jnp.zeros_like(qbuf)
        kbuf[...] = jnp.zeros_like(kbuf)
        ct_scr[...] = jnp.zeros_like(ct_scr)
        m_scr[...] = jnp.zeros_like(m_scr)
        r = lax.broadcasted_iota(jnp.int32, (n_sh * L, L), 0)
        c = lax.broadcasted_iota(jnp.int32, (n_sh * L, L), 1)
        hit = functools.reduce(jnp.logical_or,
                               [(r >= j * L) & (r < (j + 1) * L) & (c == r - j * L - (j + 1)) for j in range(n_sh)])
        sh_scr[...] = hit.astype(jnp.float32).astype(jnp.bfloat16)

    row8 = lax.broadcasted_iota(jnp.int32, (CONV_HALO, D_MODEL), 0)

    def conv(x_ref, prev, w):
        xb = x_ref[...]
        x = xb.astype(jnp.float32)
        shifted = jnp.dot(sh_scr[...], xb, preferred_element_type=jnp.float32)
        acc = x * w[n_sh:n_sh + 1, :]
        head = jnp.zeros((CONV_HALO, D_MODEL), jnp.float32)
        tail = prev[...]
        for j in range(1, n_sh + 1):
            wj = w[n_sh - j:n_sh - j + 1, :]
            acc = acc + shifted[(j - 1) * L:j * L, :] * wj
            head = head + jnp.where(row8 < j, pltpu.roll(tail, j, 0), 0.0) * wj
        prev[...] = x[L - CONV_HALO:, :]
        return jnp.concatenate([acc[:CONV_HALO] + head, acc[CONV_HALO:]], axis=0)

    cw = cw_ref[...]
    qc = (_silu(conv(aq_ref, qbuf, cw[:, :D_MODEL])) * (dh ** -0.5)).astype(jnp.bfloat16)
    kc = _silu(conv(ak_ref, kbuf, cw[:, D_MODEL:]))

    gcol = gc_ref[...]
    grow = gr_ref[0]
    t_idx = lax.broadcasted_iota(jnp.int32, (L, L), 0)
    s_idx = lax.broadcasted_iota(jnp.int32, (L, L), 1)
    causal = s_idx <= t_idx
    ones_col = (lax.broadcasted_iota(jnp.int32, (L, LANES), 1) == 0).astype(jnp.bfloat16)

    for h in range(ML_HEADS):
        sl = slice(h * dh, (h + 1) * dh)
        q = qc[:, sl]
        kf = kc[:, sl]
        kb = kf.astype(jnp.bfloat16)
        v_aug = jnp.concatenate([av_ref[:, sl], ones_col], axis=1)
        i_col = gcol[:, G_AI + h:G_AI + h + 1]
        b_col = gcol[:, G_AF + h:G_AF + h + 1]
        i_row = grow[G_AI + h:G_AI + h + 1, :]
        b_row = grow[G_AF + h:G_AF + h + 1, :]
        m_prev = m_scr[h:h + 1, 0:1]

        d = jnp.where(causal, b_col + (i_row - b_row), NEG)
        inter = b_col + m_prev
        m_t = jnp.maximum(inter, jnp.max(d, axis=-1, keepdims=True))
        w_inter = jnp.exp(inter - m_t)
        e = jnp.exp(d - m_t)
        s = lax.dot_general(q, kb, (((1,), (1,)), ((), ())), preferred_element_type=jnp.float32)
        p = (e * s).astype(jnp.bfloat16)
        ct = ct_scr[h]
        num_aug = (w_inter * jnp.dot(q, ct.astype(jnp.bfloat16), preferred_element_type=jnp.float32)
                   + jnp.dot(p, v_aug, preferred_element_type=jnp.float32))
        num = num_aug[:, :dh]
        den = num_aug[:, dh:dh + 1]
        hh = num / jnp.maximum(jnp.abs(den), jnp.exp(-m_t))
        hh = hh * lax.rsqrt(jnp.mean(hh * hh, axis=-1, keepdims=True) + EPS)
        y = (hh * ng_ref[:, sl] * _sigmoid(ao_ref[:, sl].astype(jnp.float32))
             * _silu(az_ref[:, sl].astype(jnp.float32)))
        y_ref[:, sl] = y.astype(jnp.bfloat16)

        b_last = b_col[L - 1:L, :]
        dec = b_last - b_col + i_col
        m_new = jnp.maximum(b_last + m_prev, jnp.max(dec, axis=0, keepdims=True))
        w_s = jnp.exp(dec - m_new)
        w_old = jnp.exp(b_last + m_prev - m_new)
        upd = jnp.dot((kf * w_s).T.astype(jnp.bfloat16), v_aug, preferred_element_type=jnp.float32)
        ct_scr[h] = w_old * ct + upd
        m_scr[h:h + 1, :] = jnp.broadcast_to(m_new, (1, LANES))


def _mlstm(proj, conv_w, gcol, grow, ml_norm_g, batch, seq):
    m = proj.shape[0]
    L = ML_CHUNK
    nc = seq // L

    def col(c):
        return pl.BlockSpec((L, D_MODEL), lambda b, i, c=c: (b * nc + i, c))

    return pl.pallas_call(
        _mlstm_kernel,
        grid=(batch, nc),
        in_specs=[
            col(C_AQ), col(C_AK), col(C_AV), col(C_AO), col(C_AZ),
            pl.BlockSpec((CONV_WIDTH, 2 * D_MODEL), lambda b, i: (0, 0)),
            pl.BlockSpec((L, LANES), lambda b, i: (b * nc + i, 0)),
            pl.BlockSpec((1, 16, L), lambda b, i: (b, 0, i)),
            pl.BlockSpec((1, D_MODEL), lambda b, i: (0, 0)),
        ],
        out_specs=pl.BlockSpec((L, D_MODEL), lambda b, i: (b * nc + i, 0)),
        out_shape=jax.ShapeDtypeStruct((m, D_MODEL), jnp.bfloat16),
        scratch_shapes=[
            pltpu.VMEM((CONV_HALO, D_MODEL), jnp.float32),
            pltpu.VMEM((CONV_HALO, D_MODEL), jnp.float32),
            pltpu.VMEM(((CONV_WIDTH - 1) * L, L), jnp.bfloat16),
            pltpu.VMEM((ML_HEADS, ML_HEAD_DIM, ML_HEAD_DIM + LANES), jnp.float32),
            pltpu.VMEM((8, LANES), jnp.float32),
        ],
        compiler_params=pltpu.CompilerParams(
            dimension_semantics=("arbitrary", "arbitrary"), vmem_limit_bytes=VMEM_LIMIT),
        name="mlstm",
    )(proj, proj, proj, proj, proj, conv_w, gcol, grow, ml_norm_g)


def _fox_kernel(q_ref, k_ref, v_ref, z_ref, g_ref, y_ref, qa_scr, ka_scr, vt_scr, st_scr, *, seq):
    tq, tk = FOX_TQ, FOX_TK
    h = pl.program_id(1)
    i = pl.program_id(2)
    lane1 = lax.broadcasted_iota(jnp.int32, (1, LANES), 1)

    @pl.when(i == 0)
    def _():
        src = lax.broadcasted_iota(jnp.int32, (LANES, LANES), 0)
        dst = lax.broadcasted_iota(jnp.int32, (LANES, LANES), 1)
        piece = jnp.where(src == G_FHI + h, 0, jnp.where(src == G_FMID + h, 1,
                                                         jnp.where(src == G_FLO + h, 2, -1)))
        is_piece = piece >= 0
        one = src == G_ONE
        pq = ((is_piece & ((dst == piece) | (dst == 6))) | (one & (dst >= 3) & (dst < 6)))
        pk = ((one & ((dst < 3) | (dst == 7))).astype(jnp.float32)
              - (is_piece & (dst == piece + 3)).astype(jnp.float32))
        pq = pq.astype(jnp.float32).astype(jnp.bfloat16)
        pk = pk.astype(jnp.bfloat16)

        lane_k = lax.broadcasted_iota(jnp.int32, (tk, 2 * FOX_HEAD_DIM), 1)
        ka_scr[0:tk, :] = jnp.where(lane_k == FOX_HEAD_DIM + 3, NEG, 0.0).astype(jnp.bfloat16)
        vt_scr[0] = jnp.zeros((FOX_HEAD_DIM, tk), jnp.bfloat16)

        lane_q = lax.broadcasted_iota(jnp.int32, (tk, LANES), 1)

        def build(c, carry):
            kmax, qn_vec, g_vec, fe_vec, sl_vec = carry
            r0 = pl.multiple_of(c * tk, tk)
            g = g_ref[pl.ds(r0, tk), :].astype(jnp.bfloat16)
            qx = jnp.dot(g, pq, preferred_element_type=jnp.float32)
            kx = jnp.dot(g, pk, preferred_element_type=jnp.float32)
            f = qx[:, 6:7]
            qs = (q_ref[pl.ds(r0, tk), :].astype(jnp.float32)
                  * (FOX_HEAD_DIM ** -0.5 * LOG2E)).astype(jnp.bfloat16)
            kk = k_ref[pl.ds(r0, tk), :]

            qf = qs.astype(jnp.float32)
            kf = kk.astype(jnp.float32)
            qn = jnp.sum(qf * qf, axis=1, keepdims=True)
            kn = jnp.sum(kf * kf, axis=1, keepdims=True)
            diag = jnp.sum(qf * kf, axis=1, keepdims=True)
            here = lane1 == c
            kmax = jnp.maximum(kmax, jnp.max(kn, axis=0, keepdims=True))
            qmax = jnp.max(qn, axis=0, keepdims=True)
            qn_vec = jnp.where(here, qmax, qn_vec)
            g_vec = jnp.where(here, jnp.max(f - diag, axis=0, keepdims=True), g_vec)
            fe_vec = jnp.where(here, f[tk - 1:tk, :], fe_vec)
            ref = (-jnp.sqrt(qmax * kmax)).astype(jnp.bfloat16).astype(jnp.float32)
            qx = jnp.where(lane_q == 7, ref, qx)
            sl_vec = jnp.where(here, -ref - jnp.min(diag, axis=0, keepdims=True), sl_vec)

            qa_scr[pl.ds(r0, tk), :] = jnp.concatenate([qs, qx.astype(jnp.bfloat16)], axis=1)
            ka_scr[pl.ds(r0 + tk, tk), :] = jnp.concatenate([kk, kx.astype(jnp.bfloat16)], axis=1)
            vt_scr[c + 1] = v_ref[pl.ds(r0, tk), :].astype(jnp.float32).T.astype(jnp.bfloat16)
            return kmax, qn_vec, g_vec, fe_vec, sl_vec

        zrow = jnp.zeros((1, LANES), jnp.float32)
        kmax, qn_vec, g_vec, fe_vec, sl_vec = lax.fori_loop(
            0, seq // tk, build, (jnp.zeros((1, 1), jnp.float32), zrow, zrow, zrow, zrow), unroll=4)
        st_scr[2:3, :] = sl_vec
        st_scr[0:1, :] = jnp.sqrt(qn_vec * kmax) + g_vec
        st_scr[1:2, :] = fe_vec

    def back_blocks(blk):
        bound = jnp.sum(jnp.where(lane1 == blk, st_scr[0:1, :], 0.0), axis=1, keepdims=True)
        needed = (lane1 < blk) & jnp.logical_not(st_scr[1:2, :] >= bound + SKIP_MARGIN * LOG2E)
        return blk - jnp.min(jnp.where(needed, lane1, blk))

    blks = [i * FOX_PAIR + u for u in range(FOX_PAIR)]
    n_back = functools.reduce(jnp.maximum, [back_blocks(b) for b in blks])
    qas = [qa_scr[pl.ds(pl.multiple_of(b * tq, tq), tq), :] for b in blks]

    def scores(t, diagonal):
        kbs, parts = [], []
        for u in range(FOX_PAIR):
            kb = jnp.maximum(blks[u] - t + 1, 0)
            k0 = pl.multiple_of(kb * tk, tk)
            s = lax.dot_general(ka_scr[pl.ds(k0, tk), :], qas[u], (((1,), (1,)), ((), ())),
                                preferred_element_type=jnp.float32)
            if diagonal:
                key_idx = lax.broadcasted_iota(jnp.int32, (tk, tq), 0)
                qry_idx = lax.broadcasted_iota(jnp.int32, (tk, tq), 1)
                s = jnp.where(key_idx <= qry_idx, s, NEG)
            kbs.append(kb)
            parts.append(s)
        return jnp.concatenate(parts, axis=1), kbs

    def weighted_values(p, kbs):
        p = p.astype(jnp.bfloat16)
        return jnp.concatenate(
            [jnp.dot(vt_scr[kbs[u]], p[:, u * tq:(u + 1) * tq], preferred_element_type=jnp.float32)
             for u in range(FOX_PAIR)], axis=1)

    def attend_online(carry, t, diagonal):
        m, l, acc = carry
        s, kbs = scores(t, diagonal)
        m_new = jnp.maximum(m, jnp.max(s, axis=0, keepdims=True))
        alpha = jnp.exp2(m - m_new)
        p = jnp.exp2(s - m_new)
        l = alpha * l + jnp.sum(p, axis=0, keepdims=True)
        return m_new, l, alpha * acc + weighted_values(p, kbs)

    def attend_fixed(carry, t, diagonal):
        l, acc = carry
        s, kbs = scores(t, diagonal)
        p = jnp.exp2(s)
        return l + jnp.sum(p, axis=0, keepdims=True), acc + weighted_values(p, kbs)

    rows = FOX_PAIR * tq
    zl = jnp.zeros((1, rows), jnp.float32)
    zacc = jnp.zeros((FOX_HEAD_DIM, rows), jnp.float32)

    def fixed_path():
        carry = attend_fixed((zl, zacc), 0, True)
        return lax.fori_loop(1, n_back + 1, lambda t, c: attend_fixed(c, t, False), carry)

    def online_path():
        carry = attend_online((jnp.full((1, rows), NEG, jnp.float32), zl, zacc), 0, True)
        _, l, acc = lax.fori_loop(1, n_back + 1, lambda t, c: attend_online(c, t, False), carry)
        return l, acc

    mine = (lane1 >= blks[0]) & (lane1 < blks[0] + FOX_PAIR)
    worst = jnp.max(jnp.where(mine, st_scr[2:3, :], 0.0))
    l, acc = lax.cond(worst <= FIXED_REF_SLACK, fixed_path, online_path)
    y_ref[...] = ((acc / l).T * _silu(z_ref[...].astype(jnp.float32))).astype(jnp.bfloat16)


def _fox(proj, gcol, batch, seq):
    m = proj.shape[0]
    rows = FOX_PAIR * FOX_TQ
    nq = seq // rows
    assert FOX_TQ == FOX_TK and seq % rows == 0 and seq // FOX_TQ <= LANES
    hb = D_MODEL // FOX_HEAD_DIM

    def head(c):
        return pl.BlockSpec((seq, FOX_HEAD_DIM), lambda b, h, i, c=c: (b, c * hb + h))

    return pl.pallas_call(
        functools.partial(_fox_kernel, seq=seq),
        grid=(batch, FOX_HEADS, nq),
        in_specs=[
            head(C_BQ), head(C_BK), head(C_BV),
            pl.BlockSpec((rows, FOX_HEAD_DIM), lambda b, h, i: (b * nq + i, C_BZ * hb + h)),
            pl.BlockSpec((seq, LANES), lambda b, h, i: (b, 0)),
        ],
        out_specs=pl.BlockSpec((rows, FOX_HEAD_DIM), lambda b, h, i: (b * nq + i, h)),
        out_shape=jax.ShapeDtypeStruct((m, D_MODEL), jnp.bfloat16),
        scratch_shapes=[
            pltpu.VMEM((seq, 2 * FOX_HEAD_DIM), jnp.bfloat16),
            pltpu.VMEM((seq + FOX_TK, 2 * FOX_HEAD_DIM), jnp.bfloat16),
            pltpu.VMEM((seq // FOX_TK + 1, FOX_HEAD_DIM, FOX_TK), jnp.bfloat16),
            pltpu.VMEM((8, LANES), jnp.float32),
        ],
        compiler_params=pltpu.CompilerParams(
            dimension_semantics=("arbitrary", "arbitrary", "arbitrary"), vmem_limit_bytes=VMEM_LIMIT),
        name="fox",
    )(proj, proj, proj, proj, gcol)


def _merge_kernel(ya_ref, yb_ref, cu_ref, halo_ref, cz_ref, gates_ref, x_ref, pw_ref, ps_ref,
                  wb_ref, wo_ref, fg_ref, o_ref, *, seq, final):
    tm = MERGE_TM
    gd = POOL_GROUP_DIM
    i = pl.program_id(0)
    t0 = (i * tm) % seq
    pos = t0 + lax.broadcasted_iota(jnp.int32, (tm, 1), 0)
    halo = jnp.where(t0 == 0, 0.0, halo_ref[...].astype(jnp.float32))
    u = cu_ref[...].astype(jnp.float32)
    ext = jnp.concatenate([halo, u], axis=0)

    ys = []
    for g, win in enumerate(POOL_WINDOWS):
        sl = slice(g * gd, (g + 1) * gd)
        acc = ext[:, sl]
        step = 1
        while step < win:
            acc = acc + pltpu.roll(acc, step, 0)
            step *= 2
        cnt = jnp.minimum(pos + 1, win).astype(jnp.float32)
        d = acc[HALO:, :] / cnt - u[:, sl]
        ys.append(jnp.dot(d.astype(jnp.bfloat16), pw_ref[g], preferred_element_type=jnp.float32))
    yc = jnp.concatenate(ys, axis=1) * ps_ref[...]
    yc = yc * _silu(cz_ref[...].astype(jnp.float32))

    merged = jnp.zeros((tm, D_MODEL), jnp.float32)
    for n, y in enumerate((ya_ref[...], yb_ref[...], yc.astype(jnp.bfloat16))):
        yb = jnp.dot(y, wb_ref[n], preferred_element_type=jnp.float32)
        gate = _sigmoid(gates_ref[:, n * D_MODEL:(n + 1) * D_MODEL].astype(jnp.float32))
        merged = merged + gate * yb
    out = x_ref[...] + jnp.dot(merged.astype(jnp.bfloat16), wo_ref[...],
                               preferred_element_type=jnp.float32)
    if final:
        r = lax.rsqrt(jnp.mean(out * out, axis=-1, keepdims=True) + EPS)
        out = (out * r) * fg_ref[...]
    o_ref[...] = out


def _merge(ya, yb, proj, x2, pool_w, pool_scale, w_branch, w_out, final_g, seq, final, layer):
    m = x2.shape[0]
    tm = MERGE_TM
    hpt = tm // HALO

    def rowblk(c, width=1):
        return pl.BlockSpec((tm, width * D_MODEL), lambda i, c=c: (i, c))

    const2 = lambda i: (0, 0)
    layer3 = lambda i: (layer, 0, 0)
    layer4 = lambda i: (layer, 0, 0, 0)
    return pl.pallas_call(
        functools.partial(_merge_kernel, seq=seq, final=final),
        grid=(m // tm,),
        in_specs=[
            rowblk(0), rowblk(0), rowblk(C_CU),
            pl.BlockSpec((HALO, D_MODEL), lambda i: (jnp.maximum(i * hpt - 1, 0), C_CU)),
            rowblk(C_CZ),
            pl.BlockSpec((tm, N_BRANCH * D_MODEL), lambda i: (i, C_GATES // N_BRANCH)),
            rowblk(0),
            pl.BlockSpec((None, POOL_GROUPS, POOL_GROUP_DIM, POOL_GROUP_DIM), layer4),
            pl.BlockSpec((1, D_MODEL), const2),
            pl.BlockSpec((None, N_BRANCH, D_MODEL, D_MODEL), layer4),
            pl.BlockSpec((None, D_MODEL, D_MODEL), layer3),
            pl.BlockSpec((1, D_MODEL), const2),
        ],
        out_specs=pl.BlockSpec((tm, D_MODEL), lambda i: (i, 0)),
        out_shape=jax.ShapeDtypeStruct((m, D_MODEL), jnp.float32),
        compiler_params=pltpu.CompilerParams(
            dimension_semantics=("arbitrary",), vmem_limit_bytes=VMEM_LIMIT),
        name="merge",
    )(ya, yb, proj, proj, proj, proj, x2, pool_w, pool_scale, w_branch, w_out, final_g)


O_AI = 5 * D_MODEL
O_BQ = O_AI + 2 * ML_HEADS
O_BF = O_BQ + 4 * D_MODEL
O_CU = O_BF + FOX_HEADS
O_G = O_CU + 2 * D_MODEL
N_IN = O_G + N_BRANCH * D_MODEL
assert all(o % 8 == 0 for o in (O_BQ, O_CU, O_G))


def _w_row(j):
    return jnp.where(j < C_AQ, O_G + j * D_MODEL,
                     jnp.where(j < C_BQ, (j - C_AQ) * D_MODEL,
                               jnp.where(j < C_CU, O_BQ + (j - C_BQ) * D_MODEL,
                                         O_CU + (j - C_CU) * D_MODEL)))


def kernel(x, norm_g, w_in, conv_w, ml_bi, ml_bf, ml_norm_g, fox_bf, pool_w, pool_scale, w_branch,
           w_out, final_g):
    batch, seq, d = x.shape
    depth = norm_g.shape[0]
    assert d == D_MODEL and all(seq % t == 0 for t in (IN_TM, MERGE_TM, ML_CHUNK, GATES_ROWS))
    assert GATES_ROWS % ML_CHUNK == 0
    x2 = x.reshape(batch * seq, d)
    assert IN_TN == D_MODEL and w_in.shape[1:] == (D_MODEL, N_IN)
    wt = jnp.swapaxes(w_in, 1, 2)
    pool_wb = pool_w.astype(jnp.bfloat16)
    w_branchb = w_branch.astype(jnp.bfloat16)
    w_outb = w_out.astype(jnp.bfloat16)
    for l in range(depth):
        gate_bias = jnp.pad(jnp.concatenate([ml_bi[l], ml_bf[l], fox_bf[l]]),
                            (0, LANES - 2 * ML_HEADS - FOX_HEADS)).reshape(1, LANES)
        proj, gate_pre = _in_proj(x2, norm_g[l].reshape(1, d), wt, l)
        gcol, grow = _gates(gate_pre, gate_bias, batch, seq)
        ya = _mlstm(proj, conv_w[l], gcol, grow, ml_norm_g[l].reshape(1, d), batch, seq)
        yb = _fox(proj, gcol, batch, seq)
        x2 = _merge(ya, yb, proj, x2, pool_wb, pool_scale[l].reshape(1, d), w_branchb, w_outb,
                    final_g.reshape(1, d), seq, final=(l == depth - 1), layer=l)
    return x2.reshape(batch, seq, d)
```

```python
import functools

import jax
import jax.numpy as jnp
from jax import lax
from jax.experimental import pallas as pl
from jax.experimental.pallas import tpu as pltpu

D_MODEL = 1024
ML_HEADS = 4
ML_HEAD_DIM = 256
CONV_WIDTH = 4
FOX_HEADS = 8
FOX_HEAD_DIM = 128
POOL_GROUPS = 4
POOL_GROUP_DIM = 256
POOL_WINDOWS = (2, 4, 8, 16)
N_BRANCH = 3
EPS = 1e-6

LANES = 128
HALO = 16
CONV_HALO = 8
NEG = -1e30
FIXED_REF_SLACK = 100.0
SKIP_MARGIN = 32.0

C_GATES = 0
C_AQ, C_AK, C_AV, C_AO, C_AZ, C_BQ, C_BK, C_BV, C_BZ, C_CU, C_CZ = range(N_BRANCH, N_BRANCH + 11)
N_MAIN = 14 * D_MODEL
G_AI, G_AF, G_BF = 0, 4, 8
G_FHI, G_FMID, G_FLO, G_ONE = 16, 24, 32, 127

ML_CHUNK = 256
IN_TM, IN_TN = 2048, 1024
FOX_TQ = 256
FOX_TK = 256
FOX_PAIR = 16
GATES_ROWS = 1024
LOG2E = 1.4426950408889634
MERGE_TM = 512
VMEM_LIMIT = 56 * 1024 * 1024


def _sigmoid(x):
    return 1.0 / (1.0 + jnp.exp(-x))


def _silu(x):
    return x * _sigmoid(x)


def _in_proj_kernel(x_ref, g_ref, w_ref, wga_ref, wgb_ref, proj_ref, gate_ref, h_scr):
    @pl.when(pl.program_id(1) == 0)
    def _():
        x = x_ref[...]
        r = lax.rsqrt(jnp.mean(x * x, axis=-1, keepdims=True) + EPS)
        h = ((x * r) * g_ref[...]).astype(jnp.bfloat16)
        h_scr[...] = h
        n_gate = 2 * ML_HEADS + FOX_HEADS
        wg = jnp.concatenate([wga_ref[0], wgb_ref[0], jnp.zeros((LANES - n_gate, D_MODEL), jnp.float32)],
                             axis=0).astype(jnp.bfloat16)
        gate_ref[...] = lax.dot_general(h, wg, (((1,), (1,)), ((), ())), preferred_element_type=jnp.float32)

    proj_ref[...] = lax.dot_general(h_scr[...], w_ref[0].astype(jnp.bfloat16), (((1,), (1,)), ((), ())),
                                    preferred_element_type=jnp.float32).astype(jnp.bfloat16)


def _in_proj(x2, norm_g, wt, layer):
    m = x2.shape[0]
    assert 2 * ML_HEADS == 8 and FOX_HEADS == 8

    def gate_rows(row):
        return pl.BlockSpec((pl.Element(1), pl.Element(8), pl.Element(D_MODEL)), lambda i, j: (layer, row, 0))

    return pl.pallas_call(
        _in_proj_kernel,
        grid=(m // IN_TM, N_MAIN // IN_TN),
        in_specs=[
            pl.BlockSpec((IN_TM, D_MODEL), lambda i, j: (i, 0)),
            pl.BlockSpec((1, D_MODEL), lambda i, j: (0, 0)),
            pl.BlockSpec((pl.Element(1), pl.Element(IN_TN), pl.Element(D_MODEL)),
                         lambda i, j: (layer, pl.multiple_of(_w_row(j), 8), 0)),
            gate_rows(O_AI), gate_rows(O_BF),
        ],
        out_specs=[
            pl.BlockSpec((IN_TM, IN_TN), lambda i, j: (i, j)),
            pl.BlockSpec((IN_TM, LANES), lambda i, j: (i, 0)),
        ],
        out_shape=[
            jax.ShapeDtypeStruct((m, N_MAIN), jnp.bfloat16),
            jax.ShapeDtypeStruct((m, LANES), jnp.float32),
        ],
        scratch_shapes=[pltpu.VMEM((IN_TM, D_MODEL), jnp.bfloat16)],
        compiler_params=pltpu.CompilerParams(
            dimension_semantics=("arbitrary", "arbitrary"), vmem_limit_bytes=VMEM_LIMIT),
        name="in_proj",
    )(x2, norm_g, wt, wt, wt)


def _gates_kernel(g_ref, bias_ref, col_ref, row_ref, carry_scr):
    @pl.when(pl.program_id(1) == 0)
    def _():
        carry_scr[...] = jnp.zeros_like(carry_scr)

    row = lax.broadcasted_iota(jnp.int32, (ML_CHUNK, LANES), 0)
    lane = lax.broadcasted_iota(jnp.int32, (ML_CHUNK, LANES), 1)
    carry = carry_scr[...]
    for c in range(GATES_ROWS // ML_CHUNK):
        rs = slice(c * ML_CHUNK, (c + 1) * ML_CHUNK)
        pre = g_ref[rs, :] + bias_ref[...]
        ls = jnp.minimum(pre, 0.0) - jnp.log1p(jnp.exp(-jnp.abs(pre)))
        cs = ls
        k = 1
        while k < ML_CHUNK:
            cs = cs + jnp.where(row >= k, pltpu.roll(cs, k, 0), 0.0)
            k *= 2
        run = cs + carry
        carry = run[ML_CHUNK - 1:ML_CHUNK, :]
        out = jnp.where(lane < G_AF, pre, jnp.where(lane < G_BF, cs, run))
        row_ref[0, :, rs] = out.T[0:16, :]
        f2 = run * LOG2E
        hi = f2.astype(jnp.bfloat16).astype(jnp.float32)
        r1 = f2 - hi
        mid = r1.astype(jnp.bfloat16).astype(jnp.float32)
        lo = r1 - mid
        pieces = jnp.where(lane < G_FMID, pltpu.roll(hi, G_FHI - G_BF, 1),
                           jnp.where(lane < G_FLO, pltpu.roll(mid, G_FMID - G_BF, 1),
                                     jnp.where(lane < G_FLO + FOX_HEADS, pltpu.roll(lo, G_FLO - G_BF, 1),
                                               jnp.where(lane == G_ONE, 1.0, 0.0))))
        col_ref[rs, :] = jnp.where(lane < G_FHI, out, pieces)
    carry_scr[...] = carry


def _gates(gate_pre, gate_bias, batch, seq):
    m = gate_pre.shape[0]
    nc = seq // GATES_ROWS
    return pl.pallas_call(
        _gates_kernel,
        grid=(batch, nc),
        in_specs=[
            pl.BlockSpec((GATES_ROWS, LANES), lambda b, c: (b * nc + c, 0)),
            pl.BlockSpec((1, LANES), lambda b, c: (0, 0)),
        ],
        out_specs=[
            pl.BlockSpec((GATES_ROWS, LANES), lambda b, c: (b * nc + c, 0)),
            pl.BlockSpec((1, 16, GATES_ROWS), lambda b, c: (b, 0, c)),
        ],
        out_shape=[
            jax.ShapeDtypeStruct((m, LANES), jnp.float32),
            jax.ShapeDtypeStruct((batch, 16, seq), jnp.float32),
        ],
        scratch_shapes=[pltpu.VMEM((1, LANES), jnp.float32)],
        compiler_params=pltpu.CompilerParams(dimension_semantics=("arbitrary", "arbitrary")),
        name="gates",
    )(gate_pre, gate_bias)


def _mlstm_kernel(aq_ref, ak_ref, av_ref, ao_ref, az_ref, cw_ref, gc_ref, gr_ref, ng_ref,
                  y_ref, qbuf, kbuf, sh_scr, ct_scr, m_scr):
    L = ML_CHUNK
    dh = ML_HEAD_DIM

    n_sh = CONV_WIDTH - 1

    @pl.when(pl.program_id(1) == 0)
    def _():
        qbuf[...] = jnp.zeros_like(qbuf)
        kbuf[...] = jnp.zeros_like(kbuf)
        ct_scr[...] = jnp.zeros_like(ct_scr)
        m_scr[...] = jnp.zeros_like(m_scr)
        r = lax.broadcasted_iota(jnp.int32, (n_sh * L, L), 0)
        c = lax.broadcasted_iota(jnp.int32, (n_sh * L, L), 1)
        hit = functools.reduce(jnp.logical_or,
                               [(r >= j * L) & (r < (j + 1) * L) & (c == r - j * L - (j + 1)) for j in range(n_sh)])
        sh_scr[...] = hit.astype(jnp.float32).astype(jnp.bfloat16)

    row8 = lax.broadcasted_iota(jnp.int32, (CONV_HALO, D_MODEL), 0)

    def conv(x_ref, prev, w):
        xb = x_ref[...]
        x = xb.astype(jnp.float32)
        shifted = jnp.dot(sh_scr[...], xb, preferred_element_type=jnp.float32)
        acc = x * w[n_sh:n_sh + 1, :]
        head = jnp.zeros((CONV_HALO, D_MODEL), jnp.float32)
        tail = prev[...]
        for j in range(1, n_sh + 1):
            wj = w[n_sh - j:n_sh - j + 1, :]
            acc = acc + shifted[(j - 1) * L:j * L, :] * wj
            head = head + jnp.where(row8 < j, pltpu.roll(tail, j, 0), 0.0) * wj
        prev[...] = x[L - CONV_HALO:, :]
        return jnp.concatenate([acc[:CONV_HALO] + head, acc[CONV_HALO:]], axis=0)

    cw = cw_ref[...]
    qc = (_silu(conv(aq_ref, qbuf, cw[:, :D_MODEL])) * (dh ** -0.5)).astype(jnp.bfloat16)
    kc = _silu(conv(ak_ref, kbuf, cw[:, D_MODEL:]))

    gcol = gc_ref[...]
    grow = gr_ref[0]
    t_idx = lax.broadcasted_iota(jnp.int32, (L, L), 0)
    s_idx = lax.broadcasted_iota(jnp.int32, (L, L), 1)
    causal = s_idx <= t_idx
    ones_col = (lax.broadcasted_iota(jnp.int32, (L, LANES), 1) == 0).astype(jnp.bfloat16)

    for h in range(ML_HEADS):
        sl = slice(h * dh, (h + 1) * dh)
        q = qc[:, sl]
        kf = kc[:, sl]
        kb = kf.astype(jnp.bfloat16)
        v_aug = jnp.concatenate([av_ref[:, sl], ones_col], axis=1)
        i_col = gcol[:, G_AI + h:G_AI + h + 1]
        b_col = gcol[:, G_AF + h:G_AF + h + 1]
        i_row = grow[G_AI + h:G_AI + h + 1, :]
        b_row = grow[G_AF + h:G_AF + h + 1, :]
        m_prev = m_scr[h:h + 1, 0:1]

        d = jnp.where(causal, b_col + (i_row - b_row), NEG)
        inter = b_col + m_prev
        m_t = jnp.maximum(inter, jnp.max(d, axis=-1, keepdims=True))
        w_inter = jnp.exp(inter - m_t)
        e = jnp.exp(d - m_t)
        s = lax.dot_general(q, kb, (((1,), (1,)), ((), ())), preferred_element_type=jnp.float32)
        p = (e * s).astype(jnp.bfloat16)
        ct = ct_scr[h]
        num_aug = (w_inter * jnp.dot(q, ct.astype(jnp.bfloat16), preferred_element_type=jnp.float32)
                   + jnp.dot(p, v_aug, preferred_element_type=jnp.float32))
        num = num_aug[:, :dh]
        den = num_aug[:, dh:dh + 1]
        hh = num / jnp.maximum(jnp.abs(den), jnp.exp(-m_t))
        hh = hh * lax.rsqrt(jnp.mean(hh * hh, axis=-1, keepdims=True) + EPS)
        y = (hh * ng_ref[:, sl] * _sigmoid(ao_ref[:, sl].astype(jnp.float32))
             * _silu(az_ref[:, sl].astype(jnp.float32)))
        y_ref[:, sl] = y.astype(jnp.bfloat16)

        b_last = b_col[L - 1:L, :]
        dec = b_last - b_col + i_col
        m_new = jnp.maximum(b_last + m_prev, jnp.max(dec, axis=0, keepdims=True))
        w_s = jnp.exp(dec - m_new)
        w_old = jnp.exp(b_last + m_prev - m_new)
        upd = jnp.dot((kf * w_s).T.astype(jnp.bfloat16), v_aug, preferred_element_type=jnp.float32)
        ct_scr[h] = w_old * ct + upd
        m_scr[h:h + 1, :] = jnp.broadcast_to(m_new, (1, LANES))


def _mlstm(proj, conv_w, gcol, grow, ml_norm_g, batch, seq):
    m = proj.shape[0]
    L = ML_CHUNK
    nc = seq // L

    def col(c):
        return pl.BlockSpec((L, D_MODEL), lambda b, i, c=c: (b * nc + i, c))

    return pl.pallas_call(
        _mlstm_kernel,
        grid=(batch, nc),
        in_specs=[
            col(C_AQ), col(C_AK), col(C_AV), col(C_AO), col(C_AZ),
            pl.BlockSpec((CONV_WIDTH, 2 * D_MODEL), lambda b, i: (0, 0)),
            pl.BlockSpec((L, LANES), lambda b, i: (b * nc + i, 0)),
            pl.BlockSpec((1, 16, L), lambda b, i: (b, 0, i)),
            pl.BlockSpec((1, D_MODEL), lambda b, i: (0, 0)),
        ],
        out_specs=pl.BlockSpec((L, D_MODEL), lambda b, i: (b * nc + i, 0)),
        out_shape=jax.ShapeDtypeStruct((m, D_MODEL), jnp.bfloat16),
        scratch_shapes=[
            pltpu.VMEM((CONV_HALO, D_MODEL), jnp.float32),
            pltpu.VMEM((CONV_HALO, D_MODEL), jnp.float32),
            pltpu.VMEM(((CONV_WIDTH - 1) * L, L), jnp.bfloat16),
            pltpu.VMEM((ML_HEADS, ML_HEAD_DIM, ML_HEAD_DIM + LANES), jnp.float32),
            pltpu.VMEM((8, LANES), jnp.float32),
        ],
        compiler_params=pltpu.CompilerParams(
            dimension_semantics=("arbitrary", "arbitrary"), vmem_limit_bytes=VMEM_LIMIT),
        name="mlstm",
    )(proj, proj, proj, proj, proj, conv_w, gcol, grow, ml_norm_g)


def _fox_kernel(q_ref, k_ref, v_ref, z_ref, g_ref, y_ref, qa_scr, ka_scr, vt_scr, st_scr, *, seq):
    tq, tk = FOX_TQ, FOX_TK
    h = pl.program_id(1)
    i = pl.program_id(2)
    lane1 = lax.broadcasted_iota(jnp.int32, (1, LANES), 1)

    @pl.when(i == 0)
    def _():
        src = lax.broadcasted_iota(jnp.int32, (LANES, LANES), 0)
        dst = lax.broadcasted_iota(jnp.int32, (LANES, LANES), 1)
        piece = jnp.where(src == G_FHI + h, 0, jnp.where(src == G_FMID + h, 1,
                                                         jnp.where(src == G_FLO + h, 2, -1)))
        is_piece = piece >= 0
        one = src == G_ONE
        pq = ((is_piece & ((dst == piece) | (dst == 6))) | (one & (dst >= 3) & (dst < 6)))
        pk = ((one & ((dst < 3) | (dst == 7))).astype(jnp.float32)
              - (is_piece & (dst == piece + 3)).astype(jnp.float32))
        pq = pq.astype(jnp.float32).astype(jnp.bfloat16)
        pk = pk.astype(jnp.bfloat16)

        lane_k = lax.broadcasted_iota(jnp.int32, (tk, 2 * FOX_HEAD_DIM), 1)
        ka_scr[0:tk, :] = jnp.where(lane_k == FOX_HEAD_DIM + 3, NEG, 0.0).astype(jnp.bfloat16)
        vt_scr[0] = jnp.zeros((FOX_HEAD_DIM, tk), jnp.bfloat16)

        lane_q = lax.broadcasted_iota(jnp.int32, (tk, LANES), 1)

        def build(c, carry):
            kmax, qn_vec, g_vec, fe_vec, sl_vec = carry
            r0 = pl.multiple_of(c * tk, tk)
            g = g_ref[pl.ds(r0, tk), :].astype(jnp.bfloat16)
            qx = jnp.dot(g, pq, preferred_element_type=jnp.float32)
            kx = jnp.dot(g, pk, preferred_element_type=jnp.float32)
            f = qx[:, 6:7]
            qs = (q_ref[pl.ds(r0, tk), :].astype(jnp.float32)
                  * (FOX_HEAD_DIM ** -0.5 * LOG2E)).astype(jnp.bfloat16)
            kk = k_ref[pl.ds(r0, tk), :]

            qf = qs.astype(jnp.float32)
            kf = kk.astype(jnp.float32)
            qn = jnp.sum(qf * qf, axis=1, keepdims=True)
            kn = jnp.sum(kf * kf, axis=1, keepdims=True)
            diag = jnp.sum(qf * kf, axis=1, keepdims=True)
            here = lane1 == c
            kmax = jnp.maximum(kmax, jnp.max(kn, axis=0, keepdims=True))
            qmax = jnp.max(qn, axis=0, keepdims=True)
            qn_vec = jnp.where(here, qmax, qn_vec)
            g_vec = jnp.where(here, jnp.max(f - diag, axis=0, keepdims=True), g_vec)
            fe_vec = jnp.where(here, f[tk - 1:tk, :], fe_vec)
            ref = (-jnp.sqrt(qmax * kmax)).astype(jnp.bfloat16).astype(jnp.float32)
            qx = jnp.where(lane_q == 7, ref, qx)
            sl_vec = jnp.where(here, -ref - jnp.min(diag, axis=0, keepdims=True), sl_vec)

            qa_scr[pl.ds(r0, tk), :] = jnp.concatenate([qs, qx.astype(jnp.bfloat16)], axis=1)
            ka_scr[pl.ds(r0 + tk, tk), :] = jnp.concatenate([kk, kx.astype(jnp.bfloat16)], axis=1)
            vt_scr[c + 1] = v_ref[pl.ds(r0, tk), :].astype(jnp.float32).T.astype(jnp.bfloat16)
            return kmax, qn_vec, g_vec, fe_vec, sl_vec

        zrow = jnp.zeros((1, LANES), jnp.float32)
        kmax, qn_vec, g_vec, fe_vec, sl_vec = lax.fori_loop(
            0, seq // tk, build, (jnp.zeros((1, 1), jnp.float32), zrow, zrow, zrow, zrow), unroll=4)
        st_scr[2:3, :] = sl_vec
        st_scr[0:1, :] = jnp.sqrt(qn_vec * kmax) + g_vec
        st_scr[1:2, :] = fe_vec

    blks = [i * FOX_PAIR + u for u in range(FOX_PAIR)]
    blk_u = blks[0] + lax.broadcasted_iota(jnp.int32, (FOX_PAIR, LANES), 0)
    lane_u = lax.broadcasted_iota(jnp.int32, (FOX_PAIR, LANES), 1)
    bound = jnp.sum(jnp.where(lane_u == blk_u, st_scr[0:1, :], 0.0), axis=1, keepdims=True)
    needed = (lane_u < blk_u) & jnp.logical_not(st_scr[1:2, :] >= bound + SKIP_MARGIN * LOG2E)
    n_back = jnp.max(blk_u - jnp.min(jnp.where(needed, lane_u, blk_u), axis=1, keepdims=True))
    qas = [qa_scr[pl.ds(pl.multiple_of(b * tq, tq), tq), :] for b in blks]

    def scores(t, diagonal):
        kbs, parts = [], []
        for u in range(FOX_PAIR):
            kb = jnp.maximum(blks[u] - t + 1, 0)
            k0 = pl.multiple_of(kb * tk, tk)
            s = lax.dot_general(ka_scr[pl.ds(k0, tk), :], qas[u], (((1,), (1,)), ((), ())),
                                preferred_element_type=jnp.float32)
            if diagonal:
                key_idx = lax.broadcasted_iota(jnp.int32, (tk, tq), 0)
                qry_idx = lax.broadcasted_iota(jnp.int32, (tk, tq), 1)
                s = jnp.where(key_idx <= qry_idx, s, NEG)
            kbs.append(kb)
            parts.append(s)
        return jnp.concatenate(parts, axis=1), kbs

    def weighted_values(p, kbs):
        p = p.astype(jnp.bfloat16)
        return jnp.concatenate(
            [jnp.dot(vt_scr[kbs[u]], p[:, u * tq:(u + 1) * tq], preferred_element_type=jnp.float32)
             for u in range(FOX_PAIR)], axis=1)

    def attend_online(carry, t, diagonal):
        m, l, acc = carry
        s, kbs = scores(t, diagonal)
        m_new = jnp.maximum(m, jnp.max(s, axis=0, keepdims=True))
        alpha = jnp.exp2(m - m_new)
        p = jnp.exp2(s - m_new)
        l = alpha * l + jnp.sum(p, axis=0, keepdims=True)
        return m_new, l, alpha * acc + weighted_values(p, kbs)

    def attend_fixed(carry, t, diagonal):
        l, acc = carry
        s, kbs = scores(t, diagonal)
        p = jnp.exp2(s)
        return l + jnp.sum(p, axis=0, keepdims=True), acc + weighted_values(p, kbs)

    rows = FOX_PAIR * tq
    zl = jnp.zeros((1, rows), jnp.float32)
    zacc = jnp.zeros((FOX_HEAD_DIM, rows), jnp.float32)

    def fixed_path():
        carry = attend_fixed((zl, zacc), 0, True)
        return lax.fori_loop(1, n_back + 1, lambda t, c: attend_fixed(c, t, False), carry)

    def online_path():
        carry = attend_online((jnp.full((1, rows), NEG, jnp.float32), zl, zacc), 0, True)
        _, l, acc = lax.fori_loop(1, n_back + 1, lambda t, c: attend_online(c, t, False), carry)
        return l, acc

    mine = (lane1 >= blks[0]) & (lane1 < blks[0] + FOX_PAIR)
    worst = jnp.max(jnp.where(mine, st_scr[2:3, :], 0.0))
    l, acc = lax.cond(worst <= FIXED_REF_SLACK, fixed_path, online_path)
    y_ref[...] = ((acc / l).T * _silu(z_ref[...].astype(jnp.float32))).astype(jnp.bfloat16)


def _fox(proj, gcol, batch, seq):
    m = proj.shape[0]
    rows = FOX_PAIR * FOX_TQ
    nq = seq // rows
    assert FOX_TQ == FOX_TK and seq % rows == 0 and seq // FOX_TQ <= LANES
    hb = D_MODEL // FOX_HEAD_DIM

    def head(c):
        return pl.BlockSpec((seq, FOX_HEAD_DIM), lambda b, h, i, c=c: (b, c * hb + h))

    return pl.pallas_call(
        functools.partial(_fox_kernel, seq=seq),
        grid=(batch, FOX_HEADS, nq),
        in_specs=[
            head(C_BQ), head(C_BK), head(C_BV),
            pl.BlockSpec((rows, FOX_HEAD_DIM), lambda b, h, i: (b * nq + i, C_BZ * hb + h)),
            pl.BlockSpec((seq, LANES), lambda b, h, i: (b, 0)),
        ],
        out_specs=pl.BlockSpec((rows, FOX_HEAD_DIM), lambda b, h, i: (b * nq + i, h)),
        out_shape=jax.ShapeDtypeStruct((m, D_MODEL), jnp.bfloat16),
        scratch_shapes=[
            pltpu.VMEM((seq, 2 * FOX_HEAD_DIM), jnp.bfloat16),
            pltpu.VMEM((seq + FOX_TK, 2 * FOX_HEAD_DIM), jnp.bfloat16),
            pltpu.VMEM((seq // FOX_TK + 1, FOX_HEAD_DIM, FOX_TK), jnp.bfloat16),
            pltpu.VMEM((8, LANES), jnp.float32),
        ],
        compiler_params=pltpu.CompilerParams(
            dimension_semantics=("arbitrary", "arbitrary", "arbitrary"), vmem_limit_bytes=VMEM_LIMIT),
        name="fox",
    )(proj, proj, proj, proj, gcol)


def _merge_kernel(ya_ref, yb_ref, cu_ref, halo_ref, cz_ref, gates_ref, x_ref, pw_ref, ps_ref,
                  wb_ref, wo_ref, fg_ref, o_ref, *, seq, final):
    tm = MERGE_TM
    gd = POOL_GROUP_DIM
    i = pl.program_id(0)
    t0 = (i * tm) % seq
    pos = t0 + lax.broadcasted_iota(jnp.int32, (tm, 1), 0)
    halo = jnp.where(t0 == 0, 0.0, halo_ref[...].astype(jnp.float32))
    u = cu_ref[...].astype(jnp.float32)
    ext = jnp.concatenate([halo, u], axis=0)

    ys = []
    for g, win in enumerate(POOL_WINDOWS):
        sl = slice(g * gd, (g + 1) * gd)
        acc = ext[:, sl]
        step = 1
        while step < win:
            acc = acc + pltpu.roll(acc, step, 0)
            step *= 2
        cnt = jnp.minimum(pos + 1, win).astype(jnp.float32)
        d = acc[HALO:, :] / cnt - u[:, sl]
        ys.append(jnp.dot(d.astype(jnp.bfloat16), pw_ref[g], preferred_element_type=jnp.float32))
    yc = jnp.concatenate(ys, axis=1) * ps_ref[...]
    yc = yc * _silu(cz_ref[...].astype(jnp.float32))

    merged = jnp.zeros((tm, D_MODEL), jnp.float32)
    for n, y in enumerate((ya_ref[...], yb_ref[...], yc.astype(jnp.bfloat16))):
        yb = jnp.dot(y, wb_ref[n], preferred_element_type=jnp.float32)
        gate = _sigmoid(gates_ref[:, n * D_MODEL:(n + 1) * D_MODEL].astype(jnp.float32))
        merged = merged + gate * yb
    out = x_ref[...] + jnp.dot(merged.astype(jnp.bfloat16), wo_ref[...],
                               preferred_element_type=jnp.float32)
    if final:
        r = lax.rsqrt(jnp.mean(out * out, axis=-1, keepdims=True) + EPS)
        out = (out * r) * fg_ref[...]
    o_ref[...] = out


def _merge(ya, yb, proj, x2, pool_w, pool_scale, w_branch, w_out, final_g, seq, final, layer):
    m = x2.shape[0]
    tm = MERGE_TM
    hpt = tm // HALO

    def rowblk(c, width=1):
        return pl.BlockSpec((tm, width * D_MODEL), lambda i, c=c: (i, c))

    const2 = lambda i: (0, 0)
    layer3 = lambda i: (layer, 0, 0)
    layer4 = lambda i: (layer, 0, 0, 0)
    return pl.pallas_call(
        functools.partial(_merge_kernel, seq=seq, final=final),
        grid=(m // tm,),
        in_specs=[
            rowblk(0), rowblk(0), rowblk(C_CU),
            pl.BlockSpec((HALO, D_MODEL), lambda i: (jnp.maximum(i * hpt - 1, 0), C_CU)),
            rowblk(C_CZ),
            pl.BlockSpec((tm, N_BRANCH * D_MODEL), lambda i: (i, C_GATES // N_BRANCH)),
            rowblk(0),
            pl.BlockSpec((None, POOL_GROUPS, POOL_GROUP_DIM, POOL_GROUP_DIM), layer4),
            pl.BlockSpec((1, D_MODEL), const2),
            pl.BlockSpec((None, N_BRANCH, D_MODEL, D_MODEL), layer4),
            pl.BlockSpec((None, D_MODEL, D_MODEL), layer3),
            pl.BlockSpec((1, D_MODEL), const2),
        ],
        out_specs=pl.BlockSpec((tm, D_MODEL), lambda i: (i, 0)),
        out_shape=jax.ShapeDtypeStruct((m, D_MODEL), jnp.float32),
        compiler_params=pltpu.CompilerParams(
            dimension_semantics=("arbitrary",), vmem_limit_bytes=VMEM_LIMIT),
        name="merge",
    )(ya, yb, proj, proj, proj, proj, x2, pool_w, pool_scale, w_branch, w_out, final_g)


O_AI = 5 * D_MODEL
O_BQ = O_AI + 2 * ML_HEADS
O_BF = O_BQ + 4 * D_MODEL
O_CU = O_BF + FOX_HEADS
O_G = O_CU + 2 * D_MODEL
N_IN = O_G + N_BRANCH * D_MODEL
assert all(o % 8 == 0 for o in (O_BQ, O_CU, O_G))


def _w_row(j):
    return jnp.where(j < C_AQ, O_G + j * D_MODEL,
                     jnp.where(j < C_BQ, (j - C_AQ) * D_MODEL,
                               jnp.where(j < C_CU, O_BQ + (j - C_BQ) * D_MODEL,
                                         O_CU + (j - C_CU) * D_MODEL)))


def kernel(x, norm_g, w_in, conv_w, ml_bi, ml_bf, ml_norm_g, fox_bf, pool_w, pool_scale, w_branch,
           w_out, final_g):
    batch, seq, d = x.shape
    depth = norm_g.shape[0]
    assert d == D_MODEL and all(seq % t == 0 for t in (IN_TM, MERGE_TM, ML_CHUNK, GATES_ROWS))
    assert GATES_ROWS % ML_CHUNK == 0
    x2 = x.reshape(batch * seq, d)
    assert IN_TN == D_MODEL and w_in.shape[1:] == (D_MODEL, N_IN)
    wt = jnp.swapaxes(w_in, 1, 2)
    pool_wb = pool_w.astype(jnp.bfloat16)
    w_branchb = w_branch.astype(jnp.bfloat16)
    w_outb = w_out.astype(jnp.bfloat16)
    for l in range(depth):
        gate_bias = jnp.pad(jnp.concatenate([ml_bi[l], ml_bf[l], fox_bf[l]]),
                            (0, LANES - 2 * ML_HEADS - FOX_HEADS)).reshape(1, LANES)
        proj, gate_pre = _in_proj(x2, norm_g[l].reshape(1, d), wt, l)
        gcol, grow = _gates(gate_pre, gate_bias, batch, seq)
        ya = _mlstm(proj, conv_w[l], gcol, grow, ml_norm_g[l].reshape(1, d), batch, seq)
        yb = _fox(proj, gcol, batch, seq)
        x2 = _merge(ya, yb, proj, x2, pool_wb, pool_scale[l].reshape(1, d), w_branchb, w_outb,
                    final_g.reshape(1, d), seq, final=(l == depth - 1), layer=l)
    return x2.reshape(batch, seq, d)
```

```python
import functools

import jax
import jax.numpy as jnp
from jax import lax
from jax.experimental import pallas as pl
from jax.experimental.pallas import tpu as pltpu

D_MODEL = 1024
ML_HEADS = 4
ML_HEAD_DIM = 256
CONV_WIDTH = 4
FOX_HEADS = 8
FOX_HEAD_DIM = 128
POOL_GROUPS = 4
POOL_GROUP_DIM = 256
POOL_WINDOWS = (2, 4, 8, 16)
N_BRANCH = 3
EPS = 1e-6

LANES = 128
HALO = 16
CONV_HALO = 8
NEG = -1e30
FIXED_REF_SLACK = 100.0
SKIP_MARGIN = 32.0

C_GATES = 0
C_AQ, C_AK, C_AV, C_AO, C_AZ, C_BQ, C_BK, C_BV, C_BZ, C_CU, C_CZ = range(N_BRANCH, N_BRANCH + 11)
N_MAIN = 14 * D_MODEL
G_AI, G_AF, G_BF = 0, 4, 8
G_FHI, G_FMID, G_FLO, G_ONE = 16, 24, 32, 127

ML_CHUNK = 256
IN_TM, IN_TN = 2048, 1024
FOX_TQ = 256
FOX_TK = 256
FOX_PAIR = 16
GATES_ROWS = 1024
LOG2E = 1.4426950408889634
MERGE_TM = 512
VMEM_LIMIT = 56 * 1024 * 1024


def _sigmoid(x):
    return 1.0 / (1.0 + jnp.exp(-x))


def _silu(x):
    return x * _sigmoid(x)


def _in_proj_kernel(x_ref, g_ref, w_ref, wga_ref, wgb_ref, proj_ref, gate_ref, vt_ref, h_scr):
    @pl.when(pl.program_id(1) == 0)
    def _():
        x = x_ref[...]
        r = lax.rsqrt(jnp.mean(x * x, axis=-1, keepdims=True) + EPS)
        h = ((x * r) * g_ref[...]).astype(jnp.bfloat16)
        h_scr[...] = h
        n_gate = 2 * ML_HEADS + FOX_HEADS
        wg = jnp.concatenate([wga_ref[0], wgb_ref[0], jnp.zeros((LANES - n_gate, D_MODEL), jnp.float32)],
                             axis=0).astype(jnp.bfloat16)
        gate_ref[...] = lax.dot_general(h, wg, (((1,), (1,)), ((), ())), preferred_element_type=jnp.float32)

    is_bv = pl.program_id(1) == C_BV

    @pl.when(jnp.logical_not(is_bv))
    def _():
        proj_ref[...] = lax.dot_general(h_scr[...], w_ref[0].astype(jnp.bfloat16), (((1,), (1,)), ((), ())),
                                        preferred_element_type=jnp.float32).astype(jnp.bfloat16)

    @pl.when(is_bv)
    def _():
        vt = lax.dot_general(w_ref[0].astype(jnp.bfloat16), h_scr[...], (((1,), (1,)), ((), ())),
                             preferred_element_type=jnp.float32).astype(jnp.bfloat16)
        for c in range(IN_TM // FOX_TK):
            vt_ref[c] = vt[:, c * FOX_TK:(c + 1) * FOX_TK]
        proj_ref[...] = jnp.zeros_like(proj_ref)


def _in_proj(x2, norm_g, wt, layer):
    m = x2.shape[0]
    assert 2 * ML_HEADS == 8 and FOX_HEADS == 8

    def gate_rows(row):
        return pl.BlockSpec((pl.Element(1), pl.Element(8), pl.Element(D_MODEL)), lambda i, j: (layer, row, 0))

    return pl.pallas_call(
        _in_proj_kernel,
        grid=(m // IN_TM, N_MAIN // IN_TN),
        in_specs=[
            pl.BlockSpec((IN_TM, D_MODEL), lambda i, j: (i, 0)),
            pl.BlockSpec((1, D_MODEL), lambda i, j: (0, 0)),
            pl.BlockSpec((pl.Element(1), pl.Element(IN_TN), pl.Element(D_MODEL)),
                         lambda i, j: (layer, pl.multiple_of(_w_row(j), 8), 0)),
            gate_rows(O_AI), gate_rows(O_BF),
        ],
        out_specs=[
            pl.BlockSpec((IN_TM, IN_TN), lambda i, j: (i, j)),
            pl.BlockSpec((IN_TM, LANES), lambda i, j: (i, 0)),
            pl.BlockSpec((IN_TM // FOX_TK, D_MODEL, FOX_TK), lambda i, j: (i, 0, 0)),
        ],
        out_shape=[
            jax.ShapeDtypeStruct((m, N_MAIN), jnp.bfloat16),
            jax.ShapeDtypeStruct((m, LANES), jnp.float32),
            jax.ShapeDtypeStruct((m // FOX_TK, D_MODEL, FOX_TK), jnp.bfloat16),
        ],
        scratch_shapes=[pltpu.VMEM((IN_TM, D_MODEL), jnp.bfloat16)],
        compiler_params=pltpu.CompilerParams(
            dimension_semantics=("arbitrary", "arbitrary"), vmem_limit_bytes=VMEM_LIMIT),
        name="in_proj",
    )(x2, norm_g, wt, wt, wt)


def _gates_kernel(g_ref, bias_ref, col_ref, row_ref, carry_scr):
    @pl.when(pl.program_id(1) == 0)
    def _():
        carry_scr[...] = jnp.zeros_like(carry_scr)

    row = lax.broadcasted_iota(jnp.int32, (ML_CHUNK, LANES), 0)
    lane = lax.broadcasted_iota(jnp.int32, (ML_CHUNK, LANES), 1)
    carry = carry_scr[...]
    for c in range(GATES_ROWS // ML_CHUNK):
        rs = slice(c * ML_CHUNK, (c + 1) * ML_CHUNK)
        pre = g_ref[rs, :] + bias_ref[...]
        ls = jnp.minimum(pre, 0.0) - jnp.log1p(jnp.exp(-jnp.abs(pre)))
        cs = ls
        k = 1
        while k < ML_CHUNK:
            cs = cs + jnp.where(row >= k, pltpu.roll(cs, k, 0), 0.0)
            k *= 2
        run = cs + carry
        carry = run[ML_CHUNK - 1:ML_CHUNK, :]
        out = jnp.where(lane < G_AF, pre, jnp.where(lane < G_BF, cs, run))
        row_ref[0, :, rs] = out.T[0:16, :]
        f2 = run * LOG2E
        hi = f2.astype(jnp.bfloat16).astype(jnp.float32)
        r1 = f2 - hi
        mid = r1.astype(jnp.bfloat16).astype(jnp.float32)
        lo = r1 - mid
        pieces = jnp.where(lane < G_FMID, pltpu.roll(hi, G_FHI - G_BF, 1),
                           jnp.where(lane < G_FLO, pltpu.roll(mid, G_FMID - G_BF, 1),
                                     jnp.where(lane < G_FLO + FOX_HEADS, pltpu.roll(lo, G_FLO - G_BF, 1),
                                               jnp.where(lane == G_ONE, 1.0, 0.0))))
        col_ref[rs, :] = jnp.where(lane < G_FHI, out, pieces)
    carry_scr[...] = carry


def _gates(gate_pre, gate_bias, batch, seq):
    m = gate_pre.shape[0]
    nc = seq // GATES_ROWS
    return pl.pallas_call(
        _gates_kernel,
        grid=(batch, nc),
        in_specs=[
            pl.BlockSpec((GATES_ROWS, LANES), lambda b, c: (b * nc + c, 0)),
            pl.BlockSpec((1, LANES), lambda b, c: (0, 0)),
        ],
        out_specs=[
            pl.BlockSpec((GATES_ROWS, LANES), lambda b, c: (b * nc + c, 0)),
            pl.BlockSpec((1, 16, GATES_ROWS), lambda b, c: (b, 0, c)),
        ],
        out_shape=[
            jax.ShapeDtypeStruct((m, LANES), jnp.float32),
            jax.ShapeDtypeStruct((batch, 16, seq), jnp.float32),
        ],
        scratch_shapes=[pltpu.VMEM((1, LANES), jnp.float32)],
        compiler_params=pltpu.CompilerParams(dimension_semantics=("arbitrary", "arbitrary")),
        name="gates",
    )(gate_pre, gate_bias)


def _mlstm_kernel(aq_ref, ak_ref, av_ref, ao_ref, az_ref, cw_ref, gc_ref, gr_ref, ng_ref,
                  y_ref, qbuf, kbuf, sh_scr, ct_scr, m_scr):
    L = ML_CHUNK
    dh = ML_HEAD_DIM

    n_sh = CONV_WIDTH - 1

    @pl.when(pl.program_id(1) == 0)
    def _():
        qbuf[...] = jnp.zeros_like(qbuf)
        kbuf[...] = jnp.zeros_like(kbuf)
        ct_scr[...] = jnp.zeros_like(ct_scr)
        m_scr[...] = jnp.zeros_like(m_scr)
        r = lax.broadcasted_iota(jnp.int32, (n_sh * L, L), 0)
        c = lax.broadcasted_iota(jnp.int32, (n_sh * L, L), 1)
        hit = functools.reduce(jnp.logical_or,
                               [(r >= j * L) & (r < (j + 1) * L) & (c == r - j * L - (j + 1)) for j in range(n_sh)])
        sh_scr[...] = hit.astype(jnp.float32).astype(jnp.bfloat16)

    row8 = lax.broadcasted_iota(jnp.int32, (CONV_HALO, D_MODEL), 0)

    def conv(x_ref, prev, w):
        xb = x_ref[...]
        x = xb.astype(jnp.float32)
        shifted = jnp.dot(sh_scr[...], xb, preferred_element_type=jnp.float32)
        acc = x * w[n_sh:n_sh + 1, :]
        head = jnp.zeros((CONV_HALO, D_MODEL), jnp.float32)
        tail = prev[...]
        for j in range(1, n_sh + 1):
            wj = w[n_sh - j:n_sh - j + 1, :]
            acc = acc + shifted[(j - 1) * L:j * L, :] * wj
            head = head + jnp.where(row8 < j, pltpu.roll(tail, j, 0), 0.0) * wj
        prev[...] = x[L - CONV_HALO:, :]
        return jnp.concatenate([acc[:CONV_HALO] + head, acc[CONV_HALO:]], axis=0)

    cw = cw_ref[...]
    qc = (_silu(conv(aq_ref, qbuf, cw[:, :D_MODEL])) * (dh ** -0.5)).astype(jnp.bfloat16)
    kc = _silu(conv(ak_ref, kbuf, cw[:, D_MODEL:]))

    gcol = gc_ref[...]
    grow = gr_ref[0]
    t_idx = lax.broadcasted_iota(jnp.int32, (L, L), 0)
    s_idx = lax.broadcasted_iota(jnp.int32, (L, L), 1)
    causal = s_idx <= t_idx
    ones_col = (lax.broadcasted_iota(jnp.int32, (L, LANES), 1) == 0).astype(jnp.bfloat16)

    for h in range(ML_HEADS):
        sl = slice(h * dh, (h + 1) * dh)
        q = qc[:, sl]
        kf = kc[:, sl]
        kb = kf.astype(jnp.bfloat16)
        v_aug = jnp.concatenate([av_ref[:, sl], ones_col], axis=1)
        i_col = gcol[:, G_AI + h:G_AI + h + 1]
        b_col = gcol[:, G_AF + h:G_AF + h + 1]
        i_row = grow[G_AI + h:G_AI + h + 1, :]
        b_row = grow[G_AF + h:G_AF + h + 1, :]
        m_prev = m_scr[h:h + 1, 0:1]

        d = jnp.where(causal, b_col + (i_row - b_row), NEG)
        inter = b_col + m_prev
        m_t = jnp.maximum(inter, jnp.max(d, axis=-1, keepdims=True))
        w_inter = jnp.exp(inter - m_t)
        e = jnp.exp(d - m_t)
        s = lax.dot_general(q, kb, (((1,), (1,)), ((), ())), preferred_element_type=jnp.float32)
        p = (e * s).astype(jnp.bfloat16)
        ct = ct_scr[h]
        num_aug = (w_inter * jnp.dot(q, ct.astype(jnp.bfloat16), preferred_element_type=jnp.float32)
                   + jnp.dot(p, v_aug, preferred_element_type=jnp.float32))
        num = num_aug[:, :dh]
        den = num_aug[:, dh:dh + 1]
        hh = num / jnp.maximum(jnp.abs(den), jnp.exp(-m_t))
        hh = hh * lax.rsqrt(jnp.mean(hh * hh, axis=-1, keepdims=True) + EPS)
        y = (hh * ng_ref[:, sl] * _sigmoid(ao_ref[:, sl].astype(jnp.float32))
             * _silu(az_ref[:, sl].astype(jnp.float32)))
        y_ref[:, sl] = y.astype(jnp.bfloat16)

        b_last = b_col[L - 1:L, :]
        dec = b_last - b_col + i_col
        m_new = jnp.maximum(b_last + m_prev, jnp.max(dec, axis=0, keepdims=True))
        w_s = jnp.exp(dec - m_new)
        w_old = jnp.exp(b_last + m_prev - m_new)
        upd = jnp.dot((kf * w_s).T.astype(jnp.bfloat16), v_aug, preferred_element_type=jnp.float32)
        ct_scr[h] = w_old * ct + upd
        m_scr[h:h + 1, :] = jnp.broadcast_to(m_new, (1, LANES))


def _mlstm(proj, conv_w, gcol, grow, ml_norm_g, batch, seq):
    m = proj.shape[0]
    L = ML_CHUNK
    nc = seq // L

    def col(c):
        return pl.BlockSpec((L, D_MODEL), lambda b, i, c=c: (b * nc + i, c))

    return pl.pallas_call(
        _mlstm_kernel,
        grid=(batch, nc),
        in_specs=[
            col(C_AQ), col(C_AK), col(C_AV), col(C_AO), col(C_AZ),
            pl.BlockSpec((CONV_WIDTH, 2 * D_MODEL), lambda b, i: (0, 0)),
            pl.BlockSpec((L, LANES), lambda b, i: (b * nc + i, 0)),
            pl.BlockSpec((1, 16, L), lambda b, i: (b, 0, i)),
            pl.BlockSpec((1, D_MODEL), lambda b, i: (0, 0)),
        ],
        out_specs=pl.BlockSpec((L, D_MODEL), lambda b, i: (b * nc + i, 0)),
        out_shape=jax.ShapeDtypeStruct((m, D_MODEL), jnp.bfloat16),
        scratch_shapes=[
            pltpu.VMEM((CONV_HALO, D_MODEL), jnp.float32),
            pltpu.VMEM((CONV_HALO, D_MODEL), jnp.float32),
            pltpu.VMEM(((CONV_WIDTH - 1) * L, L), jnp.bfloat16),
            pltpu.VMEM((ML_HEADS, ML_HEAD_DIM, ML_HEAD_DIM + LANES), jnp.float32),
            pltpu.VMEM((8, LANES), jnp.float32),
        ],
        compiler_params=pltpu.CompilerParams(
            dimension_semantics=("arbitrary", "arbitrary"), vmem_limit_bytes=VMEM_LIMIT),
        name="mlstm",
    )(proj, proj, proj, proj, proj, conv_w, gcol, grow, ml_norm_g)


def _fox_kernel(q_ref, k_ref, vt_ref, z_ref, g_ref, y_ref, qa_scr, ka_scr, st_scr, *, seq):
    tq, tk = FOX_TQ, FOX_TK
    h = pl.program_id(1)
    i = pl.program_id(2)
    lane1 = lax.broadcasted_iota(jnp.int32, (1, LANES), 1)

    @pl.when(i == 0)
    def _():
        src = lax.broadcasted_iota(jnp.int32, (LANES, LANES), 0)
        dst = lax.broadcasted_iota(jnp.int32, (LANES, LANES), 1)
        piece = jnp.where(src == G_FHI + h, 0, jnp.where(src == G_FMID + h, 1,
                                                         jnp.where(src == G_FLO + h, 2, -1)))
        is_piece = piece >= 0
        one = src == G_ONE
        pq = ((is_piece & ((dst == piece) | (dst == 6))) | (one & (dst >= 3) & (dst < 6)))
        pk = ((one & ((dst < 3) | (dst == 7))).astype(jnp.float32)
              - (is_piece & (dst == piece + 3)).astype(jnp.float32))
        pq = pq.astype(jnp.float32).astype(jnp.bfloat16)
        pk = pk.astype(jnp.bfloat16)

        lane_k = lax.broadcasted_iota(jnp.int32, (tk, 2 * FOX_HEAD_DIM), 1)
        ka_scr[0:tk, :] = jnp.where(lane_k == FOX_HEAD_DIM + 3, NEG, 0.0).astype(jnp.bfloat16)

        lane_q = lax.broadcasted_iota(jnp.int32, (tk, LANES), 1)

        def build(c, carry):
            kmax, qn_vec, g_vec, fe_vec, sl_vec = carry
            r0 = pl.multiple_of(c * tk, tk)
            g = g_ref[pl.ds(r0, tk), :].astype(jnp.bfloat16)
            qx = jnp.dot(g, pq, preferred_element_type=jnp.float32)
            kx = jnp.dot(g, pk, preferred_element_type=jnp.float32)
            f = qx[:, 6:7]
            qs = (q_ref[pl.ds(r0, tk), :].astype(jnp.float32)
                  * (FOX_HEAD_DIM ** -0.5 * LOG2E)).astype(jnp.bfloat16)
            kk = k_ref[pl.ds(r0, tk), :]

            qf = qs.astype(jnp.float32)
            kf = kk.astype(jnp.float32)
            qn = jnp.sum(qf * qf, axis=1, keepdims=True)
            kn = jnp.sum(kf * kf, axis=1, keepdims=True)
            diag = jnp.sum(qf * kf, axis=1, keepdims=True)
            here = lane1 == c
            kmax = jnp.maximum(kmax, jnp.max(kn, axis=0, keepdims=True))
            qmax = jnp.max(qn, axis=0, keepdims=True)
            qn_vec = jnp.where(here, qmax, qn_vec)
            g_vec = jnp.where(here, jnp.max(f - diag, axis=0, keepdims=True), g_vec)
            fe_vec = jnp.where(here, f[tk - 1:tk, :], fe_vec)
            ref = (-jnp.sqrt(qmax * kmax)).astype(jnp.bfloat16).astype(jnp.float32)
            qx = jnp.where(lane_q == 7, ref, qx)
            sl_vec = jnp.where(here, -ref - jnp.min(diag, axis=0, keepdims=True), sl_vec)

            qa_scr[pl.ds(r0, tk), :] = jnp.concatenate([qs, qx.astype(jnp.bfloat16)], axis=1)
            ka_scr[pl.ds(r0 + tk, tk), :] = jnp.concatenate([kk, kx.astype(jnp.bfloat16)], axis=1)
            return kmax, qn_vec, g_vec, fe_vec, sl_vec

        zrow = jnp.zeros((1, LANES), jnp.float32)
        kmax, qn_vec, g_vec, fe_vec, sl_vec = lax.fori_loop(
            0, seq // tk, build, (jnp.zeros((1, 1), jnp.float32), zrow, zrow, zrow, zrow), unroll=4)
        st_scr[2:3, :] = sl_vec
        st_scr[0:1, :] = jnp.sqrt(qn_vec * kmax) + g_vec
        st_scr[1:2, :] = fe_vec

    blks = [i * FOX_PAIR + u for u in range(FOX_PAIR)]
    blk_u = blks[0] + lax.broadcasted_iota(jnp.int32, (FOX_PAIR, LANES), 0)
    lane_u = lax.broadcasted_iota(jnp.int32, (FOX_PAIR, LANES), 1)
    bound = jnp.sum(jnp.where(lane_u == blk_u, st_scr[0:1, :], 0.0), axis=1, keepdims=True)
    needed = (lane_u < blk_u) & jnp.logical_not(st_scr[1:2, :] >= bound + SKIP_MARGIN * LOG2E)
    n_back = jnp.max(blk_u - jnp.min(jnp.where(needed, lane_u, blk_u), axis=1, keepdims=True))
    qas = [qa_scr[pl.ds(pl.multiple_of(b * tq, tq), tq), :] for b in blks]

    def scores(t, diagonal):
        kbs, parts = [], []
        for u in range(FOX_PAIR):
            kb = jnp.maximum(blks[u] - t + 1, 0)
            k0 = pl.multiple_of(kb * tk, tk)
            s = lax.dot_general(ka_scr[pl.ds(k0, tk), :], qas[u], (((1,), (1,)), ((), ())),
                                preferred_element_type=jnp.float32)
            if diagonal:
                key_idx = lax.broadcasted_iota(jnp.int32, (tk, tq), 0)
                qry_idx = lax.broadcasted_iota(jnp.int32, (tk, tq), 1)
                s = jnp.where(key_idx <= qry_idx, s, NEG)
            kbs.append(kb)
            parts.append(s)
        return jnp.concatenate(parts, axis=1), kbs

    def weighted_values(p, kbs):
        p = p.astype(jnp.bfloat16)
        return jnp.concatenate(
            [jnp.dot(vt_ref[jnp.maximum(kbs[u] - 1, 0)], p[:, u * tq:(u + 1) * tq],
                     preferred_element_type=jnp.float32)
             for u in range(FOX_PAIR)], axis=1)

    def attend_online(carry, t, diagonal):
        m, l, acc = carry
        s, kbs = scores(t, diagonal)
        m_new = jnp.maximum(m, jnp.max(s, axis=0, keepdims=True))
        alpha = jnp.exp2(m - m_new)
        p = jnp.exp2(s - m_new)
        l = alpha * l + jnp.sum(p, axis=0, keepdims=True)
        return m_new, l, alpha * acc + weighted_values(p, kbs)

    def attend_fixed(carry, t, diagonal):
        l, acc = carry
        s, kbs = scores(t, diagonal)
        p = jnp.exp2(s)
        return l + jnp.sum(p, axis=0, keepdims=True), acc + weighted_values(p, kbs)

    rows = FOX_PAIR * tq
    zl = jnp.zeros((1, rows), jnp.float32)
    zacc = jnp.zeros((FOX_HEAD_DIM, rows), jnp.float32)

    def fixed_path():
        carry = attend_fixed((zl, zacc), 0, True)
        return lax.fori_loop(1, n_back + 1, lambda t, c: attend_fixed(c, t, False), carry)

    def online_path():
        carry = attend_online((jnp.full((1, rows), NEG, jnp.float32), zl, zacc), 0, True)
        _, l, acc = lax.fori_loop(1, n_back + 1, lambda t, c: attend_online(c, t, False), carry)
        return l, acc

    mine = (lane1 >= blks[0]) & (lane1 < blks[0] + FOX_PAIR)
    worst = jnp.max(jnp.where(mine, st_scr[2:3, :], 0.0))
    l, acc = lax.cond(worst <= FIXED_REF_SLACK, fixed_path, online_path)
    y_ref[...] = ((acc / l).T * _silu(z_ref[...].astype(jnp.float32))).astype(jnp.bfloat16)


def _fox(proj, vt, gcol, batch, seq):
    m = proj.shape[0]
    rows = FOX_PAIR * FOX_TQ
    nq = seq // rows
    assert FOX_TQ == FOX_TK and seq % rows == 0 and seq // FOX_TQ <= LANES
    hb = D_MODEL // FOX_HEAD_DIM

    def head(c):
        return pl.BlockSpec((seq, FOX_HEAD_DIM), lambda b, h, i, c=c: (b, c * hb + h))

    return pl.pallas_call(
        functools.partial(_fox_kernel, seq=seq),
        grid=(batch, FOX_HEADS, nq),
        in_specs=[
            head(C_BQ), head(C_BK),
            pl.BlockSpec((seq // FOX_TK, FOX_HEAD_DIM, FOX_TK), lambda b, h, i: (b, h, 0)),
            pl.BlockSpec((rows, FOX_HEAD_DIM), lambda b, h, i: (b * nq + i, C_BZ * hb + h)),
            pl.BlockSpec((seq, LANES), lambda b, h, i: (b, 0)),
        ],
        out_specs=pl.BlockSpec((rows, FOX_HEAD_DIM), lambda b, h, i: (b * nq + i, h)),
        out_shape=jax.ShapeDtypeStruct((m, D_MODEL), jnp.bfloat16),
        scratch_shapes=[
            pltpu.VMEM((seq, 2 * FOX_HEAD_DIM), jnp.bfloat16),
            pltpu.VMEM((seq + FOX_TK, 2 * FOX_HEAD_DIM), jnp.bfloat16),
            pltpu.VMEM((8, LANES), jnp.float32),
        ],
        compiler_params=pltpu.CompilerParams(
            dimension_semantics=("arbitrary", "arbitrary", "arbitrary"), vmem_limit_bytes=VMEM_LIMIT),
        name="fox",
    )(proj, proj, vt, proj, gcol)


def _merge_kernel(ya_ref, yb_ref, cu_ref, halo_ref, cz_ref, gates_ref, x_ref, pw_ref, ps_ref,
                  wb_ref, wo_ref, fg_ref, o_ref, *, seq, final):
    tm = MERGE_TM
    gd = POOL_GROUP_DIM
    i = pl.program_id(0)
    t0 = (i * tm) % seq
    pos = t0 + lax.broadcasted_iota(jnp.int32, (tm, 1), 0)
    halo = jnp.where(t0 == 0, 0.0, halo_ref[...].astype(jnp.float32))
    u = cu_ref[...].astype(jnp.float32)
    ext = jnp.concatenate([halo, u], axis=0)

    ys = []
    for g, win in enumerate(POOL_WINDOWS):
        sl = slice(g * gd, (g + 1) * gd)
        acc = ext[:, sl]
        step = 1
        while step < win:
            acc = acc + pltpu.roll(acc, step, 0)
            step *= 2
        cnt = jnp.minimum(pos + 1, win).astype(jnp.float32)
        d = acc[HALO:, :] / cnt - u[:, sl]
        ys.append(jnp.dot(d.astype(jnp.bfloat16), pw_ref[g], preferred_element_type=jnp.float32))
    yc = jnp.concatenate(ys, axis=1) * ps_ref[...]
    yc = yc * _silu(cz_ref[...].astype(jnp.float32))

    merged = jnp.zeros((tm, D_MODEL), jnp.float32)
    for n, y in enumerate((ya_ref[...], yb_ref[...], yc.astype(jnp.bfloat16))):
        yb = jnp.dot(y, wb_ref[n], preferred_element_type=jnp.float32)
        gate = _sigmoid(gates_ref[:, n * D_MODEL:(n + 1) * D_MODEL].astype(jnp.float32))
        merged = merged + gate * yb
    out = x_ref[...] + jnp.dot(merged.astype(jnp.bfloat16), wo_ref[...],
                               preferred_element_type=jnp.float32)
    if final:
        r = lax.rsqrt(jnp.mean(out * out, axis=-1, keepdims=True) + EPS)
        out = (out * r) * fg_ref[...]
    o_ref[...] = out


def _merge(ya, yb, proj, x2, pool_w, pool_scale, w_branch, w_out, final_g, seq, final, layer):
    m = x2.shape[0]
    tm = MERGE_TM
    hpt = tm // HALO

    def rowblk(c, width=1):
        return pl.BlockSpec((tm, width * D_MODEL), lambda i, c=c: (i, c))

    const2 = lambda i: (0, 0)
    layer3 = lambda i: (layer, 0, 0)
    layer4 = lambda i: (layer, 0, 0, 0)
    return pl.pallas_call(
        functools.partial(_merge_kernel, seq=seq, final=final),
        grid=(m // tm,),
        in_specs=[
            rowblk(0), rowblk(0), rowblk(C_CU),
            pl.BlockSpec((HALO, D_MODEL), lambda i: (jnp.maximum(i * hpt - 1, 0), C_CU)),
            rowblk(C_CZ),
            pl.BlockSpec((tm, N_BRANCH * D_MODEL), lambda i: (i, C_GATES // N_BRANCH)),
            rowblk(0),
            pl.BlockSpec((None, POOL_GROUPS, POOL_GROUP_DIM, POOL_GROUP_DIM), layer4),
            pl.BlockSpec((1, D_MODEL), const2),
            pl.BlockSpec((None, N_BRANCH, D_MODEL, D_MODEL), layer4),
            pl.BlockSpec((None, D_MODEL, D_MODEL), layer3),
            pl.BlockSpec((1, D_MODEL), const2),
        ],
        out_specs=pl.BlockSpec((tm, D_MODEL), lambda i: (i, 0)),
        out_shape=jax.ShapeDtypeStruct((m, D_MODEL), jnp.float32),
        compiler_params=pltpu.CompilerParams(
            dimension_semantics=("arbitrary",), vmem_limit_bytes=VMEM_LIMIT),
        name="merge",
    )(ya, yb, proj, proj, proj, proj, x2, pool_w, pool_scale, w_branch, w_out, final_g)


O_AI = 5 * D_MODEL
O_BQ = O_AI + 2 * ML_HEADS
O_BF = O_BQ + 4 * D_MODEL
O_CU = O_BF + FOX_HEADS
O_G = O_CU + 2 * D_MODEL
N_IN = O_G + N_BRANCH * D_MODEL
assert all(o % 8 == 0 for o in (O_BQ, O_CU, O_G))


def _w_row(j):
    return jnp.where(j < C_AQ, O_G + j * D_MODEL,
                     jnp.where(j < C_BQ, (j - C_AQ) * D_MODEL,
                               jnp.where(j < C_CU, O_BQ + (j - C_BQ) * D_MODEL,
                                         O_CU + (j - C_CU) * D_MODEL)))


def kernel(x, norm_g, w_in, conv_w, ml_bi, ml_bf, ml_norm_g, fox_bf, pool_w, pool_scale, w_branch,
           w_out, final_g):
    batch, seq, d = x.shape
    depth = norm_g.shape[0]
    assert d == D_MODEL and all(seq % t == 0 for t in (IN_TM, MERGE_TM, ML_CHUNK, GATES_ROWS))
    assert GATES_ROWS % ML_CHUNK == 0
    x2 = x.reshape(batch * seq, d)
    assert IN_TN == D_MODEL and w_in.shape[1:] == (D_MODEL, N_IN)
    wt = jnp.swapaxes(w_in, 1, 2)
    pool_wb = pool_w.astype(jnp.bfloat16)
    w_branchb = w_branch.astype(jnp.bfloat16)
    w_outb = w_out.astype(jnp.bfloat16)
    for l in range(depth):
        gate_bias = jnp.pad(jnp.concatenate([ml_bi[l], ml_bf[l], fox_bf[l]]),
                            (0, LANES - 2 * ML_HEADS - FOX_HEADS)).reshape(1, LANES)
        proj, gate_pre, vt = _in_proj(x2, norm_g[l].reshape(1, d), wt, l)
        gcol, grow = _gates(gate_pre, gate_bias, batch, seq)
        ya = _mlstm(proj, conv_w[l], gcol, grow, ml_norm_g[l].reshape(1, d), batch, seq)
        yb = _fox(proj, vt, gcol, batch, seq)
        x2 = _merge(ya, yb, proj, x2, pool_wb, pool_scale[l].reshape(1, d), w_branchb, w_outb,
                    final_g.reshape(1, d), seq, final=(l == depth - 1), layer=l)
    return x2.reshape(batch, seq, d)
```

```python
import functools

import jax
import jax.numpy as jnp
from jax import lax
from jax.experimental import pallas as pl
from jax.experimental.pallas import tpu as pltpu

D_MODEL = 1024
ML_HEADS = 4
ML_HEAD_DIM = 256
CONV_WIDTH = 4
FOX_HEADS = 8
FOX_HEAD_DIM = 128
POOL_GROUPS = 4
POOL_GROUP_DIM = 256
POOL_WINDOWS = (2, 4, 8, 16)
N_BRANCH = 3
EPS = 1e-6

LANES = 128
HALO = 16
CONV_HALO = 8
NEG = -1e30
FIXED_REF_SLACK = 100.0
SKIP_MARGIN = 32.0

C_GATES = 0
C_AQ, C_AK, C_AV, C_AO, C_AZ, C_BQ, C_BK, C_BV, C_BZ, C_CU, C_CZ = range(N_BRANCH, N_BRANCH + 11)
N_MAIN = 14 * D_MODEL
G_AI, G_AF, G_BF = 0, 4, 8
G_FHI, G_FMID, G_FLO, G_ONE = 16, 24, 32, 127

ML_CHUNK = 256
IN_TM, IN_TN = 2048, 1024
FOX_TQ = 256
FOX_TK = 256
FOX_PAIR = 16
GATES_ROWS = 1024
LOG2E = 1.4426950408889634
MERGE_TM = 512
VMEM_LIMIT = 56 * 1024 * 1024


def _sigmoid(x):
    return 1.0 / (1.0 + jnp.exp(-x))


def _silu(x):
    return x * _sigmoid(x)


def _in_proj_kernel(x_ref, g_ref, w_ref, wga_ref, wgb_ref, proj_ref, gate_ref, vt_ref, h_scr):
    @pl.when(pl.program_id(1) == 0)
    def _():
        x = x_ref[...]
        r = lax.rsqrt(jnp.mean(x * x, axis=-1, keepdims=True) + EPS)
        h = ((x * r) * g_ref[...]).astype(jnp.bfloat16)
        h_scr[...] = h
        n_gate = 2 * ML_HEADS + FOX_HEADS
        wg = jnp.concatenate([wga_ref[0], wgb_ref[0], jnp.zeros((LANES - n_gate, D_MODEL), jnp.float32)],
                             axis=0).astype(jnp.bfloat16)
        gate_ref[...] = lax.dot_general(h, wg, (((1,), (1,)), ((), ())), preferred_element_type=jnp.float32)

    j = pl.program_id(1)
    is_bv = j == C_BV

    def project(act):
        res = lax.dot_general(h_scr[...], w_ref[0].astype(jnp.bfloat16), (((1,), (1,)), ((), ())),
                              preferred_element_type=jnp.float32)
        proj_ref[...] = act(res).astype(jnp.bfloat16)

    pl.when(j == C_AO)(lambda: project(lambda r: 0.5 * jnp.tanh(0.5 * r) + 0.5))
    pl.when(j == C_AZ)(lambda: project(lambda r: r * (0.5 * jnp.tanh(0.5 * r) + 0.5)))
    pl.when(jnp.logical_not(is_bv | (j == C_AO) | (j == C_AZ)))(lambda: project(lambda r: r))

    @pl.when(is_bv)
    def _():
        vt = lax.dot_general(w_ref[0].astype(jnp.bfloat16), h_scr[...], (((1,), (1,)), ((), ())),
                             preferred_element_type=jnp.float32).astype(jnp.bfloat16)
        for c in range(IN_TM // FOX_TK):
            vt_ref[c] = vt[:, c * FOX_TK:(c + 1) * FOX_TK]
        proj_ref[...] = jnp.zeros_like(proj_ref)


def _in_proj(x2, norm_g, wt, layer):
    m = x2.shape[0]
    assert 2 * ML_HEADS == 8 and FOX_HEADS == 8

    def gate_rows(row):
        return pl.BlockSpec((pl.Element(1), pl.Element(8), pl.Element(D_MODEL)), lambda i, j: (layer, row, 0))

    return pl.pallas_call(
        _in_proj_kernel,
        grid=(m // IN_TM, N_MAIN // IN_TN),
        in_specs=[
            pl.BlockSpec((IN_TM, D_MODEL), lambda i, j: (i, 0)),
            pl.BlockSpec((1, D_MODEL), lambda i, j: (0, 0)),
            pl.BlockSpec((pl.Element(1), pl.Element(IN_TN), pl.Element(D_MODEL)),
                         lambda i, j: (layer, pl.multiple_of(_w_row(j), 8), 0)),
            gate_rows(O_AI), gate_rows(O_BF),
        ],
        out_specs=[
            pl.BlockSpec((IN_TM, IN_TN), lambda i, j: (i, j)),
            pl.BlockSpec((IN_TM, LANES), lambda i, j: (i, 0)),
            pl.BlockSpec((IN_TM // FOX_TK, D_MODEL, FOX_TK), lambda i, j: (i, 0, 0)),
        ],
        out_shape=[
            jax.ShapeDtypeStruct((m, N_MAIN), jnp.bfloat16),
            jax.ShapeDtypeStruct((m, LANES), jnp.float32),
            jax.ShapeDtypeStruct((m // FOX_TK, D_MODEL, FOX_TK), jnp.bfloat16),
        ],
        scratch_shapes=[pltpu.VMEM((IN_TM, D_MODEL), jnp.bfloat16)],
        compiler_params=pltpu.CompilerParams(
            dimension_semantics=("arbitrary", "arbitrary"), vmem_limit_bytes=VMEM_LIMIT),
        name="in_proj",
    )(x2, norm_g, wt, wt, wt)


def _gates_kernel(g_ref, bias_ref, col_ref, row_ref, carry_scr):
    @pl.when(pl.program_id(1) == 0)
    def _():
        carry_scr[...] = jnp.zeros_like(carry_scr)

    row = lax.broadcasted_iota(jnp.int32, (ML_CHUNK, LANES), 0)
    lane = lax.broadcasted_iota(jnp.int32, (ML_CHUNK, LANES), 1)
    carry = carry_scr[...]
    for c in range(GATES_ROWS // ML_CHUNK):
        rs = slice(c * ML_CHUNK, (c + 1) * ML_CHUNK)
        pre = g_ref[rs, :] + bias_ref[...]
        ls = jnp.minimum(pre, 0.0) - jnp.log1p(jnp.exp(-jnp.abs(pre)))
        cs = ls
        k = 1
        while k < ML_CHUNK:
            cs = cs + jnp.where(row >= k, pltpu.roll(cs, k, 0), 0.0)
            k *= 2
        run = cs + carry
        carry = run[ML_CHUNK - 1:ML_CHUNK, :]
        out = jnp.where(lane < G_AF, pre, jnp.where(lane < G_BF, cs, run))
        row_ref[0, :, rs] = out.T[0:16, :]
        f2 = run * LOG2E
        hi = f2.astype(jnp.bfloat16).astype(jnp.float32)
        r1 = f2 - hi
        mid = r1.astype(jnp.bfloat16).astype(jnp.float32)
        lo = r1 - mid
        pieces = jnp.where(lane < G_FMID, pltpu.roll(hi, G_FHI - G_BF, 1),
                           jnp.where(lane < G_FLO, pltpu.roll(mid, G_FMID - G_BF, 1),
                                     jnp.where(lane < G_FLO + FOX_HEADS, pltpu.roll(lo, G_FLO - G_BF, 1),
                                               jnp.where(lane == G_ONE, 1.0, 0.0))))
        col_ref[rs, :] = jnp.where(lane < G_FHI, out, pieces)
    carry_scr[...] = carry


def _gates(gate_pre, gate_bias, batch, seq):
    m = gate_pre.shape[0]
    nc = seq // GATES_ROWS
    return pl.pallas_call(
        _gates_kernel,
        grid=(batch, nc),
        in_specs=[
            pl.BlockSpec((GATES_ROWS, LANES), lambda b, c: (b * nc + c, 0)),
            pl.BlockSpec((1, LANES), lambda b, c: (0, 0)),
        ],
        out_specs=[
            pl.BlockSpec((GATES_ROWS, LANES), lambda b, c: (b * nc + c, 0)),
            pl.BlockSpec((1, 16, GATES_ROWS), lambda b, c: (b, 0, c)),
        ],
        out_shape=[
            jax.ShapeDtypeStruct((m, LANES), jnp.float32),
            jax.ShapeDtypeStruct((batch, 16, seq), jnp.float32),
        ],
        scratch_shapes=[pltpu.VMEM((1, LANES), jnp.float32)],
        compiler_params=pltpu.CompilerParams(dimension_semantics=("arbitrary", "arbitrary")),
        name="gates",
    )(gate_pre, gate_bias)


def _mlstm_kernel(aq_ref, ak_ref, av_ref, ao_ref, az_ref, cw_ref, gc_ref, gr_ref, ng_ref,
                  y_ref, qbuf, kbuf, sh_scr, ct_scr, m_scr):
    L = ML_CHUNK
    dh = ML_HEAD_DIM

    n_sh = CONV_WIDTH - 1

    @pl.when(pl.program_id(1) == 0)
    def _():
        qbuf[...] = jnp.zeros_like(qbuf)
        kbuf[...] = jnp.zeros_like(kbuf)
        ct_scr[...] = jnp.zeros_like(ct_scr)
        m_scr[...] = jnp.zeros_like(m_scr)
        r = lax.broadcasted_iota(jnp.int32, (n_sh * L, L), 0)
        c = lax.broadcasted_iota(jnp.int32, (n_sh * L, L), 1)
        hit = functools.reduce(jnp.logical_or,
                               [(r >= j * L) & (r < (j + 1) * L) & (c == r - j * L - (j + 1)) for j in range(n_sh)])
        sh_scr[...] = hit.astype(jnp.float32).astype(jnp.bfloat16)

    row8 = lax.broadcasted_iota(jnp.int32, (CONV_HALO, D_MODEL), 0)

    def conv(x_ref, prev, w):
        xb = x_ref[...]
        x = xb.astype(jnp.float32)
        shifted = jnp.dot(sh_scr[...], xb, preferred_element_type=jnp.float32)
        acc = x * w[n_sh:n_sh + 1, :]
        head = jnp.zeros((CONV_HALO, D_MODEL), jnp.float32)
        tail = prev[...]
        for j in range(1, n_sh + 1):
            wj = w[n_sh - j:n_sh - j + 1, :]
            acc = acc + shifted[(j - 1) * L:j * L, :] * wj
            head = head + jnp.where(row8 < j, pltpu.roll(tail, j, 0), 0.0) * wj
        prev[...] = x[L - CONV_HALO:, :]
        return jnp.concatenate([acc[:CONV_HALO] + head, acc[CONV_HALO:]], axis=0)

    cw = cw_ref[...]
    qc = (_silu(conv(aq_ref, qbuf, cw[:, :D_MODEL])) * (dh ** -0.5)).astype(jnp.bfloat16)
    kc = _silu(conv(ak_ref, kbuf, cw[:, D_MODEL:]))

    gcol = gc_ref[...]
    grow = gr_ref[0]
    t_idx = lax.broadcasted_iota(jnp.int32, (L, L), 0)
    s_idx = lax.broadcasted_iota(jnp.int32, (L, L), 1)
    causal = s_idx <= t_idx
    ones_col = (lax.broadcasted_iota(jnp.int32, (L, LANES), 1) == 0).astype(jnp.bfloat16)

    for h in range(ML_HEADS):
        sl = slice(h * dh, (h + 1) * dh)
        q = qc[:, sl]
        kf = kc[:, sl]
        kb = kf.astype(jnp.bfloat16)
        v_aug = jnp.concatenate([av_ref[:, sl], ones_col], axis=1)
        i_col = gcol[:, G_AI + h:G_AI + h + 1]
        b_col = gcol[:, G_AF + h:G_AF + h + 1]
        i_row = grow[G_AI + h:G_AI + h + 1, :]
        b_row = grow[G_AF + h:G_AF + h + 1, :]
        m_prev = m_scr[h:h + 1, 0:1]

        d = jnp.where(causal, b_col + (i_row - b_row), NEG)
        inter = b_col + m_prev
        m_t = jnp.maximum(inter, jnp.max(d, axis=-1, keepdims=True))
        w_inter = jnp.exp(inter - m_t)
        e = jnp.exp(d - m_t)
        s = lax.dot_general(q, kb, (((1,), (1,)), ((), ())), preferred_element_type=jnp.float32)
        p = (e * s).astype(jnp.bfloat16)
        ct = ct_scr[h]
        num_aug = (w_inter * jnp.dot(q, ct.astype(jnp.bfloat16), preferred_element_type=jnp.float32)
                   + jnp.dot(p, v_aug, preferred_element_type=jnp.float32))
        num = num_aug[:, :dh]
        den = num_aug[:, dh:dh + 1]
        hh = num / jnp.maximum(jnp.abs(den), jnp.exp(-m_t))
        hh = hh * lax.rsqrt(jnp.mean(hh * hh, axis=-1, keepdims=True) + EPS)
        y = hh * ng_ref[:, sl] * ao_ref[:, sl].astype(jnp.float32) * az_ref[:, sl].astype(jnp.float32)
        y_ref[:, sl] = y.astype(jnp.bfloat16)

        b_last = b_col[L - 1:L, :]
        dec = b_last - b_col + i_col
        m_new = jnp.maximum(b_last + m_prev, jnp.max(dec, axis=0, keepdims=True))
        w_s = jnp.exp(dec - m_new)
        w_old = jnp.exp(b_last + m_prev - m_new)
        upd = jnp.dot((kf * w_s).T.astype(jnp.bfloat16), v_aug, preferred_element_type=jnp.float32)
        ct_scr[h] = w_old * ct + upd
        m_scr[h:h + 1, :] = jnp.broadcast_to(m_new, (1, LANES))


def _mlstm(proj, conv_w, gcol, grow, ml_norm_g, batch, seq):
    m = proj.shape[0]
    L = ML_CHUNK
    nc = seq // L

    def col(c):
        return pl.BlockSpec((L, D_MODEL), lambda b, i, c=c: (b * nc + i, c))

    return pl.pallas_call(
        _mlstm_kernel,
        grid=(batch, nc),
        in_specs=[
            col(C_AQ), col(C_AK), col(C_AV), col(C_AO), col(C_AZ),
            pl.BlockSpec((CONV_WIDTH, 2 * D_MODEL), lambda b, i: (0, 0)),
            pl.BlockSpec((L, LANES), lambda b, i: (b * nc + i, 0)),
            pl.BlockSpec((1, 16, L), lambda b, i: (b, 0, i)),
            pl.BlockSpec((1, D_MODEL), lambda b, i: (0, 0)),
        ],
        out_specs=pl.BlockSpec((L, D_MODEL), lambda b, i: (b * nc + i, 0)),
        out_shape=jax.ShapeDtypeStruct((m, D_MODEL), jnp.bfloat16),
        scratch_shapes=[
            pltpu.VMEM((CONV_HALO, D_MODEL), jnp.float32),
            pltpu.VMEM((CONV_HALO, D_MODEL), jnp.float32),
            pltpu.VMEM(((CONV_WIDTH - 1) * L, L), jnp.bfloat16),
            pltpu.VMEM((ML_HEADS, ML_HEAD_DIM, ML_HEAD_DIM + LANES), jnp.float32),
            pltpu.VMEM((8, LANES), jnp.float32),
        ],
        compiler_params=pltpu.CompilerParams(
            dimension_semantics=("arbitrary", "arbitrary"), vmem_limit_bytes=VMEM_LIMIT),
        name="mlstm",
    )(proj, proj, proj, proj, proj, conv_w, gcol, grow, ml_norm_g)


def _fox_kernel(q_ref, k_ref, vt_ref, z_ref, g_ref, y_ref, qa_scr, ka_scr, st_scr, *, seq):
    tq, tk = FOX_TQ, FOX_TK
    h = pl.program_id(1)
    i = pl.program_id(2)
    lane1 = lax.broadcasted_iota(jnp.int32, (1, LANES), 1)

    @pl.when(i == 0)
    def _():
        src = lax.broadcasted_iota(jnp.int32, (LANES, LANES), 0)
        dst = lax.broadcasted_iota(jnp.int32, (LANES, LANES), 1)
        piece = jnp.where(src == G_FHI + h, 0, jnp.where(src == G_FMID + h, 1,
                                                         jnp.where(src == G_FLO + h, 2, -1)))
        is_piece = piece >= 0
        one = src == G_ONE
        pq = ((is_piece & ((dst == piece) | (dst == 6))) | (one & (dst >= 3) & (dst < 6)))
        pk = ((one & ((dst < 3) | (dst == 7))).astype(jnp.float32)
              - (is_piece & (dst == piece + 3)).astype(jnp.float32))
        pq = pq.astype(jnp.float32).astype(jnp.bfloat16)
        pk = pk.astype(jnp.bfloat16)

        lane_k = lax.broadcasted_iota(jnp.int32, (tk, 2 * FOX_HEAD_DIM), 1)
        ka_scr[0:tk, :] = jnp.where(lane_k == FOX_HEAD_DIM + 3, NEG, 0.0).astype(jnp.bfloat16)

        lane_q = lax.broadcasted_iota(jnp.int32, (tk, LANES), 1)

        def build(c, carry):
            kmax, qn_vec, g_vec, fe_vec, sl_vec = carry
            r0 = pl.multiple_of(c * tk, tk)
            g = g_ref[pl.ds(r0, tk), :].astype(jnp.bfloat16)
            qx = jnp.dot(g, pq, preferred_element_type=jnp.float32)
            kx = jnp.dot(g, pk, preferred_element_type=jnp.float32)
            f = qx[:, 6:7]
            qs = (q_ref[pl.ds(r0, tk), :].astype(jnp.float32)
                  * (FOX_HEAD_DIM ** -0.5 * LOG2E)).astype(jnp.bfloat16)
            kk = k_ref[pl.ds(r0, tk), :]

            qf = qs.astype(jnp.float32)
            kf = kk.astype(jnp.float32)
            qn = jnp.sum(qf * qf, axis=1, keepdims=True)
            kn = jnp.sum(kf * kf, axis=1, keepdims=True)
            diag = jnp.sum(qf * kf, axis=1, keepdims=True)
            here = lane1 == c
            kmax = jnp.maximum(kmax, jnp.max(kn, axis=0, keepdims=True))
            qmax = jnp.max(qn, axis=0, keepdims=True)
            qn_vec = jnp.where(here, qmax, qn_vec)
            g_vec = jnp.where(here, jnp.max(f - diag, axis=0, keepdims=True), g_vec)
            fe_vec = jnp.where(here, f[tk - 1:tk, :], fe_vec)
            ref = (-jnp.sqrt(qmax * kmax)).astype(jnp.bfloat16).astype(jnp.float32)
            qx = jnp.where(lane_q == 7, ref, qx)
            sl_vec = jnp.where(here, -ref - jnp.min(diag, axis=0, keepdims=True), sl_vec)

            qa_scr[pl.ds(r0, tk), :] = jnp.concatenate([qs, qx.astype(jnp.bfloat16)], axis=1)
            ka_scr[pl.ds(r0 + tk, tk), :] = jnp.concatenate([kk, kx.astype(jnp.bfloat16)], axis=1)
            return kmax, qn_vec, g_vec, fe_vec, sl_vec

        zrow = jnp.zeros((1, LANES), jnp.float32)
        kmax, qn_vec, g_vec, fe_vec, sl_vec = lax.fori_loop(
            0, seq // tk, build, (jnp.zeros((1, 1), jnp.float32), zrow, zrow, zrow, zrow), unroll=4)
        st_scr[2:3, :] = sl_vec
        st_scr[0:1, :] = jnp.sqrt(qn_vec * kmax) + g_vec
        st_scr[1:2, :] = fe_vec

    blks = [i * FOX_PAIR + u for u in range(FOX_PAIR)]
    blk_u = blks[0] + lax.broadcasted_iota(jnp.int32, (FOX_PAIR, LANES), 0)
    lane_u = lax.broadcasted_iota(jnp.int32, (FOX_PAIR, LANES), 1)
    bound = jnp.sum(jnp.where(lane_u == blk_u, st_scr[0:1, :], 0.0), axis=1, keepdims=True)
    needed = (lane_u < blk_u) & jnp.logical_not(st_scr[1:2, :] >= bound + SKIP_MARGIN * LOG2E)
    n_back = jnp.max(blk_u - jnp.min(jnp.where(needed, lane_u, blk_u), axis=1, keepdims=True))
    qas = [qa_scr[pl.ds(pl.multiple_of(b * tq, tq), tq), :] for b in blks]

    def scores(t, diagonal):
        kbs, parts = [], []
        for u in range(FOX_PAIR):
            kb = jnp.maximum(blks[u] - t + 1, 0)
            k0 = pl.multiple_of(kb * tk, tk)
            s = lax.dot_general(ka_scr[pl.ds(k0, tk), :], qas[u], (((1,), (1,)), ((), ())),
                                preferred_element_type=jnp.float32)
            if diagonal:
                key_idx = lax.broadcasted_iota(jnp.int32, (tk, tq), 0)
                qry_idx = lax.broadcasted_iota(jnp.int32, (tk, tq), 1)
                s = jnp.where(key_idx <= qry_idx, s, NEG)
            kbs.append(kb)
            parts.append(s)
        return jnp.concatenate(parts, axis=1), kbs

    def weighted_values(p, kbs):
        p = p.astype(jnp.bfloat16)
        return jnp.concatenate(
            [jnp.dot(vt_ref[jnp.maximum(kbs[u] - 1, 0)], p[:, u * tq:(u + 1) * tq],
                     preferred_element_type=jnp.float32)
             for u in range(FOX_PAIR)], axis=1)

    def attend_online(carry, t, diagonal):
        m, l, acc = carry
        s, kbs = scores(t, diagonal)
        m_new = jnp.maximum(m, jnp.max(s, axis=0, keepdims=True))
        alpha = jnp.exp2(m - m_new)
        p = jnp.exp2(s - m_new)
        l = alpha * l + jnp.sum(p, axis=0, keepdims=True)
        return m_new, l, alpha * acc + weighted_values(p, kbs)

    def attend_fixed(carry, t, diagonal):
        l, acc = carry
        s, kbs = scores(t, diagonal)
        p = jnp.exp2(s)
        return l + jnp.sum(p, axis=0, keepdims=True), acc + weighted_values(p, kbs)

    rows = FOX_PAIR * tq
    zl = jnp.zeros((1, rows), jnp.float32)
    zacc = jnp.zeros((FOX_HEAD_DIM, rows), jnp.float32)

    def fixed_path():
        carry = attend_fixed((zl, zacc), 0, True)
        return lax.fori_loop(1, n_back + 1, lambda t, c: attend_fixed(c, t, False), carry)

    def online_path():
        carry = attend_online((jnp.full((1, rows), NEG, jnp.float32), zl, zacc), 0, True)
        _, l, acc = lax.fori_loop(1, n_back + 1, lambda t, c: attend_online(c, t, False), carry)
        return l, acc

    mine = (lane1 >= blks[0]) & (lane1 < blks[0] + FOX_PAIR)
    worst = jnp.max(jnp.where(mine, st_scr[2:3, :], 0.0))
    l, acc = lax.cond(worst <= FIXED_REF_SLACK, fixed_path, online_path)
    y_ref[...] = ((acc / l).T * _silu(z_ref[...].astype(jnp.float32))).astype(jnp.bfloat16)


def _fox(proj, vt, gcol, batch, seq):
    m = proj.shape[0]
    rows = FOX_PAIR * FOX_TQ
    nq = seq // rows
    assert FOX_TQ == FOX_TK and seq % rows == 0 and seq // FOX_TQ <= LANES
    hb = D_MODEL // FOX_HEAD_DIM

    def head(c):
        return pl.BlockSpec((seq, FOX_HEAD_DIM), lambda b, h, i, c=c: (b, c * hb + h))

    return pl.pallas_call(
        functools.partial(_fox_kernel, seq=seq),
        grid=(batch, FOX_HEADS, nq),
        in_specs=[
            head(C_BQ), head(C_BK),
            pl.BlockSpec((seq // FOX_TK, FOX_HEAD_DIM, FOX_TK), lambda b, h, i: (b, h, 0)),
            pl.BlockSpec((rows, FOX_HEAD_DIM), lambda b, h, i: (b * nq + i, C_BZ * hb + h)),
            pl.BlockSpec((seq, LANES), lambda b, h, i: (b, 0)),
        ],
        out_specs=pl.BlockSpec((rows, FOX_HEAD_DIM), lambda b, h, i: (b * nq + i, h)),
        out_shape=jax.ShapeDtypeStruct((m, D_MODEL), jnp.bfloat16),
        scratch_shapes=[
            pltpu.VMEM((seq, 2 * FOX_HEAD_DIM), jnp.bfloat16),
            pltpu.VMEM((seq + FOX_TK, 2 * FOX_HEAD_DIM), jnp.bfloat16),
            pltpu.VMEM((8, LANES), jnp.float32),
        ],
        compiler_params=pltpu.CompilerParams(
            dimension_semantics=("arbitrary", "arbitrary", "arbitrary"), vmem_limit_bytes=VMEM_LIMIT),
        name="fox",
    )(proj, proj, vt, proj, gcol)


def _merge_kernel(ya_ref, yb_ref, cu_ref, halo_ref, cz_ref, gates_ref, x_ref, pw_ref, ps_ref,
                  wb_ref, wo_ref, fg_ref, o_ref, *, seq, final):
    tm = MERGE_TM
    gd = POOL_GROUP_DIM
    i = pl.program_id(0)
    t0 = (i * tm) % seq
    pos = t0 + lax.broadcasted_iota(jnp.int32, (tm, 1), 0)
    halo = jnp.where(t0 == 0, 0.0, halo_ref[...].astype(jnp.float32))
    u = cu_ref[...].astype(jnp.float32)
    ext = jnp.concatenate([halo, u], axis=0)

    ys = []
    for g, win in enumerate(POOL_WINDOWS):
        sl = slice(g * gd, (g + 1) * gd)
        acc = ext[:, sl]
        step = 1
        while step < win:
            acc = acc + pltpu.roll(acc, step, 0)
            step *= 2
        cnt = jnp.minimum(pos + 1, win).astype(jnp.float32)
        d = acc[HALO:, :] / cnt - u[:, sl]
        ys.append(jnp.dot(d.astype(jnp.bfloat16), pw_ref[g], preferred_element_type=jnp.float32))
    yc = jnp.concatenate(ys, axis=1) * ps_ref[...]
    yc = yc * _silu(cz_ref[...].astype(jnp.float32))

    merged = jnp.zeros((tm, D_MODEL), jnp.float32)
    for n, y in enumerate((ya_ref[...], yb_ref[...], yc.astype(jnp.bfloat16))):
        yb = jnp.dot(y, wb_ref[n], preferred_element_type=jnp.float32)
        gate = _sigmoid(gates_ref[:, n * D_MODEL:(n + 1) * D_MODEL].astype(jnp.float32))
        merged = merged + gate * yb
    out = x_ref[...] + jnp.dot(merged.astype(jnp.bfloat16), wo_ref[...],
                               preferred_element_type=jnp.float32)
    if final:
        r = lax.rsqrt(jnp.mean(out * out, axis=-1, keepdims=True) + EPS)
        out = (out * r) * fg_ref[...]
    o_ref[...] = out


def _merge(ya, yb, proj, x2, pool_w, pool_scale, w_branch, w_out, final_g, seq, final, layer):
    m = x2.shape[0]
    tm = MERGE_TM
    hpt = tm // HALO

    def rowblk(c, width=1):
        return pl.BlockSpec((tm, width * D_MODEL), lambda i, c=c: (i, c))

    const2 = lambda i: (0, 0)
    layer3 = lambda i: (layer, 0, 0)
    layer4 = lambda i: (layer, 0, 0, 0)
    return pl.pallas_call(
        functools.partial(_merge_kernel, seq=seq, final=final),
        grid=(m // tm,),
        in_specs=[
            rowblk(0), rowblk(0), rowblk(C_CU),
            pl.BlockSpec((HALO, D_MODEL), lambda i: (jnp.maximum(i * hpt - 1, 0), C_CU)),
            rowblk(C_CZ),
            pl.BlockSpec((tm, N_BRANCH * D_MODEL), lambda i: (i, C_GATES // N_BRANCH)),
            rowblk(0),
            pl.BlockSpec((None, POOL_GROUPS, POOL_GROUP_DIM, POOL_GROUP_DIM), layer4),
            pl.BlockSpec((1, D_MODEL), const2),
            pl.BlockSpec((None, N_BRANCH, D_MODEL, D_MODEL), layer4),
            pl.BlockSpec((None, D_MODEL, D_MODEL), layer3),
            pl.BlockSpec((1, D_MODEL), const2),
        ],
        out_specs=pl.BlockSpec((tm, D_MODEL), lambda i: (i, 0)),
        out_shape=jax.ShapeDtypeStruct((m, D_MODEL), jnp.float32),
        compiler_params=pltpu.CompilerParams(
            dimension_semantics=("arbitrary",), vmem_limit_bytes=VMEM_LIMIT),
        name="merge",
    )(ya, yb, proj, proj, proj, proj, x2, pool_w, pool_scale, w_branch, w_out, final_g)


O_AI = 5 * D_MODEL
O_BQ = O_AI + 2 * ML_HEADS
O_BF = O_BQ + 4 * D_MODEL
O_CU = O_BF + FOX_HEADS
O_G = O_CU + 2 * D_MODEL
N_IN = O_G + N_BRANCH * D_MODEL
assert all(o % 8 == 0 for o in (O_BQ, O_CU, O_G))


def _w_row(j):
    return jnp.where(j < C_AQ, O_G + j * D_MODEL,
                     jnp.where(j < C_BQ, (j - C_AQ) * D_MODEL,
                               jnp.where(j < C_CU, O_BQ + (j - C_BQ) * D_MODEL,
                                         O_CU + (j - C_CU) * D_MODEL)))


def kernel(x, norm_g, w_in, conv_w, ml_bi, ml_bf, ml_norm_g, fox_bf, pool_w, pool_scale, w_branch,
           w_out, final_g):
    batch, seq, d = x.shape
    depth = norm_g.shape[0]
    assert d == D_MODEL and all(seq % t == 0 for t in (IN_TM, MERGE_TM, ML_CHUNK, GATES_ROWS))
    assert GATES_ROWS % ML_CHUNK == 0
    x2 = x.reshape(batch * seq, d)
    assert IN_TN == D_MODEL and w_in.shape[1:] == (D_MODEL, N_IN)
    wt = jnp.swapaxes(w_in, 1, 2)
    pool_wb = pool_w.astype(jnp.bfloat16)
    w_branchb = w_branch.astype(jnp.bfloat16)
    w_outb = w_out.astype(jnp.bfloat16)
    for l in range(depth):
        gate_bias = jnp.pad(jnp.concatenate([ml_bi[l], ml_bf[l], fox_bf[l]]),
                            (0, LANES - 2 * ML_HEADS - FOX_HEADS)).reshape(1, LANES)
        proj, gate_pre, vt = _in_proj(x2, norm_g[l].reshape(1, d), wt, l)
        gcol, grow = _gates(gate_pre, gate_bias, batch, seq)
        ya = _mlstm(proj, conv_w[l], gcol, grow, ml_norm_g[l].reshape(1, d), batch, seq)
        yb = _fox(proj, vt, gcol, batch, seq)
        x2 = _merge(ya, yb, proj, x2, pool_wb, pool_scale[l].reshape(1, d), w_branchb, w_outb,
                    final_g.reshape(1, d), seq, final=(l == depth - 1), layer=l)
    return x2.reshape(batch, seq, d)
```

```python
import functools

import jax
import jax.numpy as jnp
from jax import lax
from jax.experimental import pallas as pl
from jax.experimental.pallas import tpu as pltpu

D_MODEL = 1024
ML_HEADS = 4
ML_HEAD_DIM = 256
CONV_WIDTH = 4
FOX_HEADS = 8
FOX_HEAD_DIM = 128
POOL_GROUPS = 4
POOL_GROUP_DIM = 256
POOL_WINDOWS = (2, 4, 8, 16)
N_BRANCH = 3
EPS = 1e-6

LANES = 128
HALO = 16
CONV_HALO = 8
NEG = -1e30
FIXED_REF_SLACK = 100.0
SKIP_MARGIN = 32.0

C_GATES = 0
C_AQ, C_AK, C_AV, C_AO, C_AZ, C_BQ, C_BK, C_BV, C_BZ, C_CU, C_CZ = range(N_BRANCH, N_BRANCH + 11)
N_MAIN = 14 * D_MODEL
G_AI, G_AF, G_BF = 0, 4, 8
G_FHI, G_FMID, G_FLO, G_ONE = 16, 24, 32, 127

ML_CHUNK = 256
IN_TM, IN_TN = 2048, 1024
FOX_TQ = 256
FOX_TK = 256
FOX_PAIR = 16
GATES_ROWS = 1024
LOG2E = 1.4426950408889634
MERGE_TM = 512
VMEM_LIMIT = 56 * 1024 * 1024


def _sigmoid(x):
    return 1.0 / (1.0 + jnp.exp(-x))


def _silu(x):
    return x * _sigmoid(x)


def _in_proj_kernel(x_ref, g_ref, w_ref, wga_ref, wgb_ref, proj_ref, gate_ref, vt_ref, h_scr):
    @pl.when(pl.program_id(1) == 0)
    def _():
        x = x_ref[...]
        r = lax.rsqrt(jnp.mean(x * x, axis=-1, keepdims=True) + EPS)
        h = ((x * r) * g_ref[...]).astype(jnp.bfloat16)
        h_scr[...] = h
        n_gate = 2 * ML_HEADS + FOX_HEADS
        wg = jnp.concatenate([wga_ref[0], wgb_ref[0], jnp.zeros((LANES - n_gate, D_MODEL), jnp.float32)],
                             axis=0).astype(jnp.bfloat16)
        gate_ref[...] = lax.dot_general(h, wg, (((1,), (1,)), ((), ())), preferred_element_type=jnp.float32)

    is_bv = pl.program_id(1) == C_BV

    @pl.when(jnp.logical_not(is_bv))
    def _():
        proj_ref[...] = lax.dot_general(h_scr[...], w_ref[0].astype(jnp.bfloat16), (((1,), (1,)), ((), ())),
                                        preferred_element_type=jnp.float32).astype(jnp.bfloat16)

    @pl.when(is_bv)
    def _():
        vt = lax.dot_general(w_ref[0].astype(jnp.bfloat16), h_scr[...], (((1,), (1,)), ((), ())),
                             preferred_element_type=jnp.float32).astype(jnp.bfloat16)
        for c in range(IN_TM // FOX_TK):
            vt_ref[c] = vt[:, c * FOX_TK:(c + 1) * FOX_TK]
        proj_ref[...] = jnp.zeros_like(proj_ref)


def _in_proj(x2, norm_g, wt, layer):
    m = x2.shape[0]
    assert 2 * ML_HEADS == 8 and FOX_HEADS == 8

    def gate_rows(row):
        return pl.BlockSpec((pl.Element(1), pl.Element(8), pl.Element(D_MODEL)), lambda i, j: (layer, row, 0))

    return pl.pallas_call(
        _in_proj_kernel,
        grid=(m // IN_TM, N_MAIN // IN_TN),
        in_specs=[
            pl.BlockSpec((IN_TM, D_MODEL), lambda i, j: (i, 0)),
            pl.BlockSpec((1, D_MODEL), lambda i, j: (0, 0)),
            pl.BlockSpec((pl.Element(1), pl.Element(IN_TN), pl.Element(D_MODEL)),
                         lambda i, j: (layer, pl.multiple_of(_w_row(j), 8), 0)),
            gate_rows(O_AI), gate_rows(O_BF),
        ],
        out_specs=[
            pl.BlockSpec((IN_TM, IN_TN), lambda i, j: (i, j)),
            pl.BlockSpec((IN_TM, LANES), lambda i, j: (i, 0)),
            pl.BlockSpec((IN_TM // FOX_TK, D_MODEL, FOX_TK), lambda i, j: (i, 0, 0)),
        ],
        out_shape=[
            jax.ShapeDtypeStruct((m, N_MAIN), jnp.bfloat16),
            jax.ShapeDtypeStruct((m, LANES), jnp.float32),
            jax.ShapeDtypeStruct((m // FOX_TK, D_MODEL, FOX_TK), jnp.bfloat16),
        ],
        scratch_shapes=[pltpu.VMEM((IN_TM, D_MODEL), jnp.bfloat16)],
        compiler_params=pltpu.CompilerParams(
            dimension_semantics=("arbitrary", "arbitrary"), vmem_limit_bytes=VMEM_LIMIT),
        name="in_proj",
    )(x2, norm_g, wt, wt, wt)


def _gates_kernel(g_ref, bias_ref, col_ref, row_ref, carry_scr):
    @pl.when(pl.program_id(1) == 0)
    def _():
        carry_scr[...] = jnp.zeros_like(carry_scr)

    row = lax.broadcasted_iota(jnp.int32, (ML_CHUNK, LANES), 0)
    lane = lax.broadcasted_iota(jnp.int32, (ML_CHUNK, LANES), 1)
    carry = carry_scr[...]
    for c in range(GATES_ROWS // ML_CHUNK):
        rs = slice(c * ML_CHUNK, (c + 1) * ML_CHUNK)
        pre = g_ref[rs, :] + bias_ref[...]
        ls = jnp.minimum(pre, 0.0) - jnp.log1p(jnp.exp(-jnp.abs(pre)))
        cs = ls
        k = 1
        while k < ML_CHUNK:
            cs = cs + jnp.where(row >= k, pltpu.roll(cs, k, 0), 0.0)
            k *= 2
        run = cs + carry
        carry = run[ML_CHUNK - 1:ML_CHUNK, :]
        out = jnp.where(lane < G_AF, pre, jnp.where(lane < G_BF, cs, run))
        row_ref[0, :, rs] = out.T[0:16, :]
        f2 = run * LOG2E
        hi = f2.astype(jnp.bfloat16).astype(jnp.float32)
        r1 = f2 - hi
        mid = r1.astype(jnp.bfloat16).astype(jnp.float32)
        lo = r1 - mid
        pieces = jnp.where(lane < G_FMID, pltpu.roll(hi, G_FHI - G_BF, 1),
                           jnp.where(lane < G_FLO, pltpu.roll(mid, G_FMID - G_BF, 1),
                                     jnp.where(lane < G_FLO + FOX_HEADS, pltpu.roll(lo, G_FLO - G_BF, 1),
                                               jnp.where(lane == G_ONE, 1.0, 0.0))))
        col_ref[rs, :] = jnp.where(lane < G_FHI, out, pieces)
    carry_scr[...] = carry


def _gates(gate_pre, gate_bias, batch, seq):
    m = gate_pre.shape[0]
    nc = seq // GATES_ROWS
    return pl.pallas_call(
        _gates_kernel,
        grid=(batch, nc),
        in_specs=[
            pl.BlockSpec((GATES_ROWS, LANES), lambda b, c: (b * nc + c, 0)),
            pl.BlockSpec((1, LANES), lambda b, c: (0, 0)),
        ],
        out_specs=[
            pl.BlockSpec((GATES_ROWS, LANES), lambda b, c: (b * nc + c, 0)),
            pl.BlockSpec((1, 16, GATES_ROWS), lambda b, c: (b, 0, c)),
        ],
        out_shape=[
            jax.ShapeDtypeStruct((m, LANES), jnp.float32),
            jax.ShapeDtypeStruct((batch, 16, seq), jnp.float32),
        ],
        scratch_shapes=[pltpu.VMEM((1, LANES), jnp.float32)],
        compiler_params=pltpu.CompilerParams(dimension_semantics=("arbitrary", "arbitrary")),
        name="gates",
    )(gate_pre, gate_bias)


def _mlstm_kernel(aq_ref, ak_ref, av_ref, ao_ref, az_ref, cw_ref, gc_ref, gr_ref, ng_ref,
                  y_ref, qbuf, kbuf, sh_scr, ct_scr, m_scr):
    L = ML_CHUNK
    dh = ML_HEAD_DIM

    n_sh = CONV_WIDTH - 1

    @pl.when(pl.program_id(1) == 0)
    def _():
        qbuf[...] = jnp.zeros_like(qbuf)
        kbuf[...] = jnp.zeros_like(kbuf)
        ct_scr[...] = jnp.zeros_like(ct_scr)
        m_scr[...] = jnp.zeros_like(m_scr)
        r = lax.broadcasted_iota(jnp.int32, (n_sh * L, L), 0)
        c = lax.broadcasted_iota(jnp.int32, (n_sh * L, L), 1)
        hit = functools.reduce(jnp.logical_or,
                               [(r >= j * L) & (r < (j + 1) * L) & (c == r - j * L - (j + 1)) for j in range(n_sh)])
        sh_scr[...] = hit.astype(jnp.float32).astype(jnp.bfloat16)

    row8 = lax.broadcasted_iota(jnp.int32, (CONV_HALO, D_MODEL), 0)

    def conv(x_ref, prev, w):
        xb = x_ref[...]
        x = xb.astype(jnp.float32)
        shifted = jnp.dot(sh_scr[...], xb, preferred_element_type=jnp.float32)
        acc = x * w[n_sh:n_sh + 1, :]
        head = jnp.zeros((CONV_HALO, D_MODEL), jnp.float32)
        tail = prev[...]
        for j in range(1, n_sh + 1):
            wj = w[n_sh - j:n_sh - j + 1, :]
            acc = acc + shifted[(j - 1) * L:j * L, :] * wj
            head = head + jnp.where(row8 < j, pltpu.roll(tail, j, 0), 0.0) * wj
        prev[...] = x[L - CONV_HALO:, :]
        return jnp.concatenate([acc[:CONV_HALO] + head, acc[CONV_HALO:]], axis=0)

    cw = cw_ref[...]
    qc = (_silu(conv(aq_ref, qbuf, cw[:, :D_MODEL])) * (dh ** -0.5)).astype(jnp.bfloat16)
    kc = _silu(conv(ak_ref, kbuf, cw[:, D_MODEL:]))

    gcol = gc_ref[...]
    grow = gr_ref[0]
    t_idx = lax.broadcasted_iota(jnp.int32, (L, L), 0)
    s_idx = lax.broadcasted_iota(jnp.int32, (L, L), 1)
    causal = s_idx <= t_idx
    ones_col = (lax.broadcasted_iota(jnp.int32, (L, LANES), 1) == 0).astype(jnp.bfloat16)

    for h in range(ML_HEADS):
        sl = slice(h * dh, (h + 1) * dh)
        q = qc[:, sl]
        kf = kc[:, sl]
        kb = kf.astype(jnp.bfloat16)
        v_aug = jnp.concatenate([av_ref[:, sl], ones_col], axis=1)
        i_col = gcol[:, G_AI + h:G_AI + h + 1]
        b_col = gcol[:, G_AF + h:G_AF + h + 1]
        i_row = grow[G_AI + h:G_AI + h + 1, :]
        b_row = grow[G_AF + h:G_AF + h + 1, :]
        m_prev = m_scr[h:h + 1, 0:1]

        d = jnp.where(causal, b_col + (i_row - b_row), NEG)
        inter = b_col + m_prev
        m_t = jnp.maximum(inter, jnp.max(d, axis=-1, keepdims=True))
        w_inter = jnp.exp(inter - m_t)
        e = jnp.exp(d - m_t)
        s = lax.dot_general(q, kb, (((1,), (1,)), ((), ())), preferred_element_type=jnp.float32)
        p = (e * s).astype(jnp.bfloat16)
        ct = ct_scr[h]
        num_aug = (w_inter * jnp.dot(q, ct.astype(jnp.bfloat16), preferred_element_type=jnp.float32)
                   + jnp.dot(p, v_aug, preferred_element_type=jnp.float32))
        num = num_aug[:, :dh]
        den = num_aug[:, dh:dh + 1]
        hh = num / jnp.maximum(jnp.abs(den), jnp.exp(-m_t))
        hh = hh * lax.rsqrt(jnp.mean(hh * hh, axis=-1, keepdims=True) + EPS)
        y = (hh * ng_ref[:, sl] * _sigmoid(ao_ref[:, sl].astype(jnp.float32))
             * _silu(az_ref[:, sl].astype(jnp.float32)))
        y_ref[:, sl] = y.astype(jnp.bfloat16)

        b_last = b_col[L - 1:L, :]
        dec = b_last - b_col + i_col
        m_new = jnp.maximum(b_last + m_prev, jnp.max(dec, axis=0, keepdims=True))
        w_s = jnp.exp(dec - m_new)
        w_old = jnp.exp(b_last + m_prev - m_new)
        upd = jnp.dot((kf * w_s).T.astype(jnp.bfloat16), v_aug, preferred_element_type=jnp.float32)
        ct_scr[h] = w_old * ct + upd
        m_scr[h:h + 1, :] = jnp.broadcast_to(m_new, (1, LANES))


def _mlstm(proj, conv_w, gcol, grow, ml_norm_g, batch, seq):
    m = proj.shape[0]
    L = ML_CHUNK
    nc = seq // L

    def col(c):
        return pl.BlockSpec((L, D_MODEL), lambda b, i, c=c: (b * nc + i, c))

    return pl.pallas_call(
        _mlstm_kernel,
        grid=(batch, nc),
        in_specs=[
            col(C_AQ), col(C_AK), col(C_AV), col(C_AO), col(C_AZ),
            pl.BlockSpec((CONV_WIDTH, 2 * D_MODEL), lambda b, i: (0, 0)),
            pl.BlockSpec((L, LANES), lambda b, i: (b * nc + i, 0)),
            pl.BlockSpec((1, 16, L), lambda b, i: (b, 0, i)),
            pl.BlockSpec((1, D_MODEL), lambda b, i: (0, 0)),
        ],
        out_specs=pl.BlockSpec((L, D_MODEL), lambda b, i: (b * nc + i, 0)),
        out_shape=jax.ShapeDtypeStruct((m, D_MODEL), jnp.bfloat16),
        scratch_shapes=[
            pltpu.VMEM((CONV_HALO, D_MODEL), jnp.float32),
            pltpu.VMEM((CONV_HALO, D_MODEL), jnp.float32),
            pltpu.VMEM(((CONV_WIDTH - 1) * L, L), jnp.bfloat16),
            pltpu.VMEM((ML_HEADS, ML_HEAD_DIM, ML_HEAD_DIM + LANES), jnp.float32),
            pltpu.VMEM((8, LANES), jnp.float32),
        ],
        compiler_params=pltpu.CompilerParams(
            dimension_semantics=("arbitrary", "arbitrary"), vmem_limit_bytes=VMEM_LIMIT),
        name="mlstm",
    )(proj, proj, proj, proj, proj, conv_w, gcol, grow, ml_norm_g)


def _fox_kernel(q_ref, k_ref, vt_ref, z_ref, g_ref, y_ref, qa_scr, ka_scr, st_scr, *, seq):
    tq, tk = FOX_TQ, FOX_TK
    h = pl.program_id(1)
    i = pl.program_id(2)
    lane1 = lax.broadcasted_iota(jnp.int32, (1, LANES), 1)

    @pl.when(i == 0)
    def _():
        src = lax.broadcasted_iota(jnp.int32, (LANES, LANES), 0)
        dst = lax.broadcasted_iota(jnp.int32, (LANES, LANES), 1)
        piece = jnp.where(src == G_FHI + h, 0, jnp.where(src == G_FMID + h, 1,
                                                         jnp.where(src == G_FLO + h, 2, -1)))
        is_piece = piece >= 0
        one = src == G_ONE
        pq = ((is_piece & ((dst == piece) | (dst == 6))) | (one & (dst >= 3) & (dst < 6)))
        pk = ((one & ((dst < 3) | (dst == 7))).astype(jnp.float32)
              - (is_piece & (dst == piece + 3)).astype(jnp.float32))
        pq = pq.astype(jnp.float32).astype(jnp.bfloat16)
        pk = pk.astype(jnp.bfloat16)

        lane_k = lax.broadcasted_iota(jnp.int32, (tk, 2 * FOX_HEAD_DIM), 1)
        ka_scr[0:tk, :] = jnp.where(lane_k == FOX_HEAD_DIM + 3, NEG, 0.0).astype(jnp.bfloat16)

        lane_q = lax.broadcasted_iota(jnp.int32, (tk, LANES), 1)

        def build(c, carry):
            kmax, qn_vec, g_vec, fe_vec, sl_vec = carry
            r0 = pl.multiple_of(c * tk, tk)
            g = g_ref[pl.ds(r0, tk), :].astype(jnp.bfloat16)
            qx = jnp.dot(g, pq, preferred_element_type=jnp.float32)
            kx = jnp.dot(g, pk, preferred_element_type=jnp.float32)
            f = qx[:, 6:7]
            qs = (q_ref[pl.ds(r0, tk), :].astype(jnp.float32)
                  * (FOX_HEAD_DIM ** -0.5 * LOG2E)).astype(jnp.bfloat16)
            kk = k_ref[pl.ds(r0, tk), :]

            qf = qs.astype(jnp.float32)
            kf = kk.astype(jnp.float32)
            qn = jnp.sum(qf * qf, axis=1, keepdims=True)
            kn = jnp.sum(kf * kf, axis=1, keepdims=True)
            diag = jnp.sum(qf * kf, axis=1, keepdims=True)
            here = lane1 == c
            kmax = jnp.maximum(kmax, jnp.max(kn, axis=0, keepdims=True))
            qmax = jnp.max(qn, axis=0, keepdims=True)
            qn_vec = jnp.where(here, qmax, qn_vec)
            g_vec = jnp.where(here, jnp.max(f - diag, axis=0, keepdims=True), g_vec)
            fe_vec = jnp.where(here, f[tk - 1:tk, :], fe_vec)
            ref = (-jnp.sqrt(qmax * kmax)).astype(jnp.bfloat16).astype(jnp.float32)
            qx = jnp.where(lane_q == 7, ref, qx)
            sl_vec = jnp.where(here, -ref - jnp.min(diag, axis=0, keepdims=True), sl_vec)

            qa_scr[pl.ds(r0, tk), :] = jnp.concatenate([qs, qx.astype(jnp.bfloat16)], axis=1)
            ka_scr[pl.ds(r0 + tk, tk), :] = jnp.concatenate([kk, kx.astype(jnp.bfloat16)], axis=1)
            return kmax, qn_vec, g_vec, fe_vec, sl_vec

        zrow = jnp.zeros((1, LANES), jnp.float32)
        kmax, qn_vec, g_vec, fe_vec, sl_vec = lax.fori_loop(
            0, seq // tk, build, (jnp.zeros((1, 1), jnp.float32), zrow, zrow, zrow, zrow), unroll=8)
        st_scr[2:3, :] = sl_vec
        st_scr[0:1, :] = jnp.sqrt(qn_vec * kmax) + g_vec
        st_scr[1:2, :] = fe_vec

    blks = [i * FOX_PAIR + u for u in range(FOX_PAIR)]
    blk_u = blks[0] + lax.broadcasted_iota(jnp.int32, (FOX_PAIR, LANES), 0)
    lane_u = lax.broadcasted_iota(jnp.int32, (FOX_PAIR, LANES), 1)
    bound = jnp.sum(jnp.where(lane_u == blk_u, st_scr[0:1, :], 0.0), axis=1, keepdims=True)
    needed = (lane_u < blk_u) & jnp.logical_not(st_scr[1:2, :] >= bound + SKIP_MARGIN * LOG2E)
    n_back = jnp.max(blk_u - jnp.min(jnp.where(needed, lane_u, blk_u), axis=1, keepdims=True))
    qas = [qa_scr[pl.ds(pl.multiple_of(b * tq, tq), tq), :] for b in blks]

    def scores(t, diagonal):
        kbs, parts = [], []
        for u in range(FOX_PAIR):
            kb = jnp.maximum(blks[u] - t + 1, 0)
            k0 = pl.multiple_of(kb * tk, tk)
            s = lax.dot_general(ka_scr[pl.ds(k0, tk), :], qas[u], (((1,), (1,)), ((), ())),
                                preferred_element_type=jnp.float32)
            if diagonal:
                key_idx = lax.broadcasted_iota(jnp.int32, (tk, tq), 0)
                qry_idx = lax.broadcasted_iota(jnp.int32, (tk, tq), 1)
                s = jnp.where(key_idx <= qry_idx, s, NEG)
            kbs.append(kb)
            parts.append(s)
        return jnp.concatenate(parts, axis=1), kbs

    def weighted_values(p, kbs):
        p = p.astype(jnp.bfloat16)
        return jnp.concatenate(
            [jnp.dot(vt_ref[jnp.maximum(kbs[u] - 1, 0)], p[:, u * tq:(u + 1) * tq],
                     preferred_element_type=jnp.float32)
             for u in range(FOX_PAIR)], axis=1)

    def attend_online(carry, t, diagonal):
        m, l, acc = carry
        s, kbs = scores(t, diagonal)
        m_new = jnp.maximum(m, jnp.max(s, axis=0, keepdims=True))
        alpha = jnp.exp2(m - m_new)
        p = jnp.exp2(s - m_new)
        l = alpha * l + jnp.sum(p, axis=0, keepdims=True)
        return m_new, l, alpha * acc + weighted_values(p, kbs)

    def attend_fixed(carry, t, diagonal):
        l, acc = carry
        s, kbs = scores(t, diagonal)
        p = jnp.exp2(s)
        return l + jnp.sum(p, axis=0, keepdims=True), acc + weighted_values(p, kbs)

    rows = FOX_PAIR * tq
    zl = jnp.zeros((1, rows), jnp.float32)
    zacc = jnp.zeros((FOX_HEAD_DIM, rows), jnp.float32)

    def fixed_path():
        carry = attend_fixed((zl, zacc), 0, True)
        return lax.fori_loop(1, n_back + 1, lambda t, c: attend_fixed(c, t, False), carry)

    def online_path():
        carry = attend_online((jnp.full((1, rows), NEG, jnp.float32), zl, zacc), 0, True)
        _, l, acc = lax.fori_loop(1, n_back + 1, lambda t, c: attend_online(c, t, False), carry)
        return l, acc

    mine = (lane1 >= blks[0]) & (lane1 < blks[0] + FOX_PAIR)
    worst = jnp.max(jnp.where(mine, st_scr[2:3, :], 0.0))
    l, acc = lax.cond(worst <= FIXED_REF_SLACK, fixed_path, online_path)
    y_ref[...] = ((acc / l).T * _silu(z_ref[...].astype(jnp.float32))).astype(jnp.bfloat16)


def _fox(proj, vt, gcol, batch, seq):
    m = proj.shape[0]
    rows = FOX_PAIR * FOX_TQ
    nq = seq // rows
    assert FOX_TQ == FOX_TK and seq % rows == 0 and seq // FOX_TQ <= LANES
    hb = D_MODEL // FOX_HEAD_DIM

    def head(c):
        return pl.BlockSpec((seq, FOX_HEAD_DIM), lambda b, h, i, c=c: (b, c * hb + h))

    return pl.pallas_call(
        functools.partial(_fox_kernel, seq=seq),
        grid=(batch, FOX_HEADS, nq),
        in_specs=[
            head(C_BQ), head(C_BK),
            pl.BlockSpec((seq // FOX_TK, FOX_HEAD_DIM, FOX_TK), lambda b, h, i: (b, h, 0)),
            pl.BlockSpec((rows, FOX_HEAD_DIM), lambda b, h, i: (b * nq + i, C_BZ * hb + h)),
            pl.BlockSpec((seq, LANES), lambda b, h, i: (b, 0)),
        ],
        out_specs=pl.BlockSpec((rows, FOX_HEAD_DIM), lambda b, h, i: (b * nq + i, h)),
        out_shape=jax.ShapeDtypeStruct((m, D_MODEL), jnp.bfloat16),
        scratch_shapes=[
            pltpu.VMEM((seq, 2 * FOX_HEAD_DIM), jnp.bfloat16),
            pltpu.VMEM((seq + FOX_TK, 2 * FOX_HEAD_DIM), jnp.bfloat16),
            pltpu.VMEM((8, LANES), jnp.float32),
        ],
        compiler_params=pltpu.CompilerParams(
            dimension_semantics=("arbitrary", "arbitrary", "arbitrary"), vmem_limit_bytes=VMEM_LIMIT),
        name="fox",
    )(proj, proj, vt, proj, gcol)


def _merge_kernel(ya_ref, yb_ref, cu_ref, halo_ref, cz_ref, gates_ref, x_ref, pw_ref, ps_ref,
                  wb_ref, wo_ref, fg_ref, o_ref, *, seq, final):
    tm = MERGE_TM
    gd = POOL_GROUP_DIM
    i = pl.program_id(0)
    t0 = (i * tm) % seq
    pos = t0 + lax.broadcasted_iota(jnp.int32, (tm, 1), 0)
    halo = jnp.where(t0 == 0, 0.0, halo_ref[...].astype(jnp.float32))
    u = cu_ref[...].astype(jnp.float32)
    ext = jnp.concatenate([halo, u], axis=0)

    ys = []
    for g, win in enumerate(POOL_WINDOWS):
        sl = slice(g * gd, (g + 1) * gd)
        acc = ext[:, sl]
        step = 1
        while step < win:
            acc = acc + pltpu.roll(acc, step, 0)
            step *= 2
        cnt = jnp.minimum(pos + 1, win).astype(jnp.float32)
        d = acc[HALO:, :] / cnt - u[:, sl]
        ys.append(jnp.dot(d.astype(jnp.bfloat16), pw_ref[g], preferred_element_type=jnp.float32))
    yc = jnp.concatenate(ys, axis=1) * ps_ref[...]
    yc = yc * _silu(cz_ref[...].astype(jnp.float32))

    merged = jnp.zeros((tm, D_MODEL), jnp.float32)
    for n, y in enumerate((ya_ref[...], yb_ref[...], yc.astype(jnp.bfloat16))):
        yb = jnp.dot(y, wb_ref[n], preferred_element_type=jnp.float32)
        gate = _sigmoid(gates_ref[:, n * D_MODEL:(n + 1) * D_MODEL].astype(jnp.float32))
        merged = merged + gate * yb
    out = x_ref[...] + jnp.dot(merged.astype(jnp.bfloat16), wo_ref[...],
                               preferred_element_type=jnp.float32)
    if final:
        r = lax.rsqrt(jnp.mean(out * out, axis=-1, keepdims=True) + EPS)
        out = (out * r) * fg_ref[...]
    o_ref[...] = out


def _merge(ya, yb, proj, x2, pool_w, pool_scale, w_branch, w_out, final_g, seq, final, layer):
    m = x2.shape[0]
    tm = MERGE_TM
    hpt = tm // HALO

    def rowblk(c, width=1):
        return pl.BlockSpec((tm, width * D_MODEL), lambda i, c=c: (i, c))

    const2 = lambda i: (0, 0)
    layer3 = lambda i: (layer, 0, 0)
    layer4 = lambda i: (layer, 0, 0, 0)
    return pl.pallas_call(
        functools.partial(_merge_kernel, seq=seq, final=final),
        grid=(m // tm,),
        in_specs=[
            rowblk(0), rowblk(0), rowblk(C_CU),
            pl.BlockSpec((HALO, D_MODEL), lambda i: (jnp.maximum(i * hpt - 1, 0), C_CU)),
            rowblk(C_CZ),
            pl.BlockSpec((tm, N_BRANCH * D_MODEL), lambda i: (i, C_GATES // N_BRANCH)),
            rowblk(0),
            pl.BlockSpec((None, POOL_GROUPS, POOL_GROUP_DIM, POOL_GROUP_DIM), layer4),
            pl.BlockSpec((1, D_MODEL), const2),
            pl.BlockSpec((None, N_BRANCH, D_MODEL, D_MODEL), layer4),
            pl.BlockSpec((None, D_MODEL, D_MODEL), layer3),
            pl.BlockSpec((1, D_MODEL), const2),
        ],
        out_specs=pl.BlockSpec((tm, D_MODEL), lambda i: (i, 0)),
        out_shape=jax.ShapeDtypeStruct((m, D_MODEL), jnp.float32),
        compiler_params=pltpu.CompilerParams(
            dimension_semantics=("arbitrary",), vmem_limit_bytes=VMEM_LIMIT),
        name="merge",
    )(ya, yb, proj, proj, proj, proj, x2, pool_w, pool_scale, w_branch, w_out, final_g)


O_AI = 5 * D_MODEL
O_BQ = O_AI + 2 * ML_HEADS
O_BF = O_BQ + 4 * D_MODEL
O_CU = O_BF + FOX_HEADS
O_G = O_CU + 2 * D_MODEL
N_IN = O_G + N_BRANCH * D_MODEL
assert all(o % 8 == 0 for o in (O_BQ, O_CU, O_G))


def _w_row(j):
    return jnp.where(j < C_AQ, O_G + j * D_MODEL,
                     jnp.where(j < C_BQ, (j - C_AQ) * D_MODEL,
                               jnp.where(j < C_CU, O_BQ + (j - C_BQ) * D_MODEL,
                                         O_CU + (j - C_CU) * D_MODEL)))


def kernel(x, norm_g, w_in, conv_w, ml_bi, ml_bf, ml_norm_g, fox_bf, pool_w, pool_scale, w_branch,
           w_out, final_g):
    batch, seq, d = x.shape
    depth = norm_g.shape[0]
    assert d == D_MODEL and all(seq % t == 0 for t in (IN_TM, MERGE_TM, ML_CHUNK, GATES_ROWS))
    assert GATES_ROWS % ML_CHUNK == 0
    x2 = x.reshape(batch * seq, d)
    assert IN_TN == D_MODEL and w_in.shape[1:] == (D_MODEL, N_IN)
    wt = jnp.swapaxes(w_in, 1, 2)
    pool_wb = pool_w.astype(jnp.bfloat16)
    w_branchb = w_branch.astype(jnp.bfloat16)
    w_outb = w_out.astype(jnp.bfloat16)
    for l in range(depth):
        gate_bias = jnp.pad(jnp.concatenate([ml_bi[l], ml_bf[l], fox_bf[l]]),
                            (0, LANES - 2 * ML_HEADS - FOX_HEADS)).reshape(1, LANES)
        proj, gate_pre, vt = _in_proj(x2, norm_g[l].reshape(1, d), wt, l)
        gcol, grow = _gates(gate_pre, gate_bias, batch, seq)
        ya = _mlstm(proj, conv_w[l], gcol, grow, ml_norm_g[l].reshape(1, d), batch, seq)
        yb = _fox(proj, vt, gcol, batch, seq)
        x2 = _merge(ya, yb, proj, x2, pool_wb, pool_scale[l].reshape(1, d), w_branchb, w_outb,
                    final_g.reshape(1, d), seq, final=(l == depth - 1), layer=l)
    return x2.reshape(batch, seq, d)
```

```python
import functools

import jax
import jax.numpy as jnp
from jax import lax
from jax.experimental import pallas as pl
from jax.experimental.pallas import tpu as pltpu

D_MODEL = 1024
ML_HEADS = 4
ML_HEAD_DIM = 256
CONV_WIDTH = 4
FOX_HEADS = 8
FOX_HEAD_DIM = 128
POOL_GROUPS = 4
POOL_GROUP_DIM = 256
POOL_WINDOWS = (2, 4, 8, 16)
N_BRANCH = 3
EPS = 1e-6

LANES = 128
HALO = 16
CONV_HALO = 8
NEG = -1e30
FIXED_REF_SLACK = 100.0
SKIP_MARGIN = 32.0

C_GATES = 0
C_AQ, C_AK, C_AV, C_AO, C_AZ, C_BQ, C_BK, C_BV, C_BZ, C_CU, C_CZ = range(N_BRANCH, N_BRANCH + 11)
N_MAIN = 14 * D_MODEL
G_AI, G_AF, G_BF = 0, 4, 8
G_FHI, G_FMID, G_FLO, G_ONE = 16, 24, 32, 127

ML_CHUNK = 256
IN_TM, IN_TN = 2048, 1024
FOX_TQ = 256
FOX_TK = 256
FOX_PAIR = 16
GATES_ROWS = 1024
LOG2E = 1.4426950408889634
MERGE_TM = 512
VMEM_LIMIT = 56 * 1024 * 1024


def _sigmoid(x):
    return 1.0 / (1.0 + jnp.exp(-x))


def _silu(x):
    return x * _sigmoid(x)


def _in_proj_kernel(x_ref, g_ref, w_ref, wga_ref, wgb_ref, proj_ref, gate_ref, vt_ref, h_scr):
    @pl.when(pl.program_id(1) == 0)
    def _():
        x = x_ref[...]
        r = lax.rsqrt(jnp.mean(x * x, axis=-1, keepdims=True) + EPS)
        h = ((x * r) * g_ref[...]).astype(jnp.bfloat16)
        h_scr[...] = h
        n_gate = 2 * ML_HEADS + FOX_HEADS
        wg = jnp.concatenate([wga_ref[0], wgb_ref[0], jnp.zeros((LANES - n_gate, D_MODEL), jnp.float32)],
                             axis=0).astype(jnp.bfloat16)
        gate_ref[...] = lax.dot_general(h, wg, (((1,), (1,)), ((), ())), preferred_element_type=jnp.float32)

    is_bv = pl.program_id(1) == C_BV

    @pl.when(jnp.logical_not(is_bv))
    def _():
        proj_ref[...] = lax.dot_general(h_scr[...], w_ref[0].astype(jnp.bfloat16), (((1,), (1,)), ((), ())),
                                        preferred_element_type=jnp.float32).astype(jnp.bfloat16)

    @pl.when(is_bv)
    def _():
        vt = lax.dot_general(w_ref[0].astype(jnp.bfloat16), h_scr[...], (((1,), (1,)), ((), ())),
                             preferred_element_type=jnp.float32).astype(jnp.bfloat16)
        for c in range(IN_TM // FOX_TK):
            vt_ref[c] = vt[:, c * FOX_TK:(c + 1) * FOX_TK]
        proj_ref[...] = jnp.zeros_like(proj_ref)


def _in_proj(x2, norm_g, wt, layer):
    m = x2.shape[0]
    assert 2 * ML_HEADS == 8 and FOX_HEADS == 8

    def gate_rows(row):
        return pl.BlockSpec((pl.Element(1), pl.Element(8), pl.Element(D_MODEL)), lambda i, j: (layer, row, 0))

    return pl.pallas_call(
        _in_proj_kernel,
        grid=(m // IN_TM, N_MAIN // IN_TN),
        in_specs=[
            pl.BlockSpec((IN_TM, D_MODEL), lambda i, j: (i, 0)),
            pl.BlockSpec((1, D_MODEL), lambda i, j: (0, 0)),
            pl.BlockSpec((pl.Element(1), pl.Element(IN_TN), pl.Element(D_MODEL)),
                         lambda i, j: (layer, pl.multiple_of(_w_row(j), 8), 0)),
            gate_rows(O_AI), gate_rows(O_BF),
        ],
        out_specs=[
            pl.BlockSpec((IN_TM, IN_TN), lambda i, j: (i, j)),
            pl.BlockSpec((IN_TM, LANES), lambda i, j: (i, 0)),
            pl.BlockSpec((IN_TM // FOX_TK, D_MODEL, FOX_TK), lambda i, j: (i, 0, 0)),
        ],
        out_shape=[
            jax.ShapeDtypeStruct((m, N_MAIN), jnp.bfloat16),
            jax.ShapeDtypeStruct((m, LANES), jnp.float32),
            jax.ShapeDtypeStruct((m // FOX_TK, D_MODEL, FOX_TK), jnp.bfloat16),
        ],
        scratch_shapes=[pltpu.VMEM((IN_TM, D_MODEL), jnp.bfloat16)],
        compiler_params=pltpu.CompilerParams(
            dimension_semantics=("arbitrary", "arbitrary"), vmem_limit_bytes=VMEM_LIMIT),
        name="in_proj",
    )(x2, norm_g, wt, wt, wt)


def _gates_kernel(g_ref, bias_ref, col_ref, row_ref, carry_scr):
    @pl.when(pl.program_id(1) == 0)
    def _():
        carry_scr[...] = jnp.zeros_like(carry_scr)

    row = lax.broadcasted_iota(jnp.int32, (ML_CHUNK, LANES), 0)
    lane = lax.broadcasted_iota(jnp.int32, (ML_CHUNK, LANES), 1)
    carry = carry_scr[...]
    for c in range(GATES_ROWS // ML_CHUNK):
        rs = slice(c * ML_CHUNK, (c + 1) * ML_CHUNK)
        pre = g_ref[rs, :] + bias_ref[...]
        ls = jnp.minimum(pre, 0.0) - jnp.log1p(jnp.exp(-jnp.abs(pre)))
        cs = ls
        k = 1
        while k < ML_CHUNK:
            cs = cs + jnp.where(row >= k, pltpu.roll(cs, k, 0), 0.0)
            k *= 2
        run = cs + carry
        carry = run[ML_CHUNK - 1:ML_CHUNK, :]
        out = jnp.where(lane < G_AF, pre, jnp.where(lane < G_BF, cs, run))
        row_ref[0, :, rs] = out.T[0:16, :]
        f2 = run * LOG2E
        hi = f2.astype(jnp.bfloat16).astype(jnp.float32)
        r1 = f2 - hi
        mid = r1.astype(jnp.bfloat16).astype(jnp.float32)
        lo = r1 - mid
        pieces = jnp.where(lane < G_FMID, pltpu.roll(hi, G_FHI - G_BF, 1),
                           jnp.where(lane < G_FLO, pltpu.roll(mid, G_FMID - G_BF, 1),
                                     jnp.where(lane < G_FLO + FOX_HEADS, pltpu.roll(lo, G_FLO - G_BF, 1),
                                               jnp.where(lane == G_ONE, 1.0, 0.0))))
        col_ref[rs, :] = jnp.where(lane < G_FHI, out, pieces)
    carry_scr[...] = carry


def _gates(gate_pre, gate_bias, batch, seq):
    m = gate_pre.shape[0]
    nc = seq // GATES_ROWS
    return pl.pallas_call(
        _gates_kernel,
        grid=(batch, nc),
        in_specs=[
            pl.BlockSpec((GATES_ROWS, LANES), lambda b, c: (b * nc + c, 0)),
            pl.BlockSpec((1, LANES), lambda b, c: (0, 0)),
        ],
        out_specs=[
            pl.BlockSpec((GATES_ROWS, LANES), lambda b, c: (b * nc + c, 0)),
            pl.BlockSpec((1, 16, GATES_ROWS), lambda b, c: (b, 0, c)),
        ],
        out_shape=[
            jax.ShapeDtypeStruct((m, LANES), jnp.float32),
            jax.ShapeDtypeStruct((batch, 16, seq), jnp.float32),
        ],
        scratch_shapes=[pltpu.VMEM((1, LANES), jnp.float32)],
        compiler_params=pltpu.CompilerParams(dimension_semantics=("arbitrary", "arbitrary")),
        name="gates",
    )(gate_pre, gate_bias)


def _mlstm_kernel(aq_ref, ak_ref, av_ref, ao_ref, az_ref, cw_ref, gc_ref, gr_ref, ng_ref,
                  y_ref, qbuf, kbuf, sh_scr, ct_scr, m_scr):
    L = ML_CHUNK
    dh = ML_HEAD_DIM

    n_sh = CONV_WIDTH - 1

    @pl.when(pl.program_id(1) == 0)
    def _():
        qbuf[...] = jnp.zeros_like(qbuf)
        kbuf[...] = jnp.zeros_like(kbuf)
        ct_scr[...] = jnp.zeros_like(ct_scr)
        m_scr[...] = jnp.zeros_like(m_scr)
        r = lax.broadcasted_iota(jnp.int32, (n_sh * L, L), 0)
        c = lax.broadcasted_iota(jnp.int32, (n_sh * L, L), 1)
        hit = functools.reduce(jnp.logical_or,
                               [(r >= j * L) & (r < (j + 1) * L) & (c == r - j * L - (j + 1)) for j in range(n_sh)])
        sh_scr[...] = hit.astype(jnp.float32).astype(jnp.bfloat16)

    row8 = lax.broadcasted_iota(jnp.int32, (CONV_HALO, D_MODEL), 0)

    def conv(x_ref, prev, w):
        xb = x_ref[...]
        x = xb.astype(jnp.float32)
        shifted = jnp.dot(sh_scr[...], xb, preferred_element_type=jnp.float32)
        acc = x * w[n_sh:n_sh + 1, :]
        head = jnp.zeros((CONV_HALO, D_MODEL), jnp.float32)
        tail = prev[...]
        for j in range(1, n_sh + 1):
            wj = w[n_sh - j:n_sh - j + 1, :]
            acc = acc + shifted[(j - 1) * L:j * L, :] * wj
            head = head + jnp.where(row8 < j, pltpu.roll(tail, j, 0), 0.0) * wj
        prev[...] = x[L - CONV_HALO:, :]
        return jnp.concatenate([acc[:CONV_HALO] + head, acc[CONV_HALO:]], axis=0)

    cw = cw_ref[...]
    qc = (_silu(conv(aq_ref, qbuf, cw[:, :D_MODEL])) * (dh ** -0.5)).astype(jnp.bfloat16)
    kc = _silu(conv(ak_ref, kbuf, cw[:, D_MODEL:]))

    gcol = gc_ref[...]
    grow = gr_ref[0]
    t_idx = lax.broadcasted_iota(jnp.int32, (L, L), 0)
    s_idx = lax.broadcasted_iota(jnp.int32, (L, L), 1)
    causal = s_idx <= t_idx
    ones_col = (lax.broadcasted_iota(jnp.int32, (L, LANES), 1) == 0).astype(jnp.bfloat16)

    for h in range(ML_HEADS):
        sl = slice(h * dh, (h + 1) * dh)
        q = qc[:, sl]
        kf = kc[:, sl]
        kb = kf.astype(jnp.bfloat16)
        v_aug = jnp.concatenate([av_ref[:, sl], ones_col], axis=1)
        i_col = gcol[:, G_AI + h:G_AI + h + 1]
        b_col = gcol[:, G_AF + h:G_AF + h + 1]
        i_row = grow[G_AI + h:G_AI + h + 1, :]
        b_row = grow[G_AF + h:G_AF + h + 1, :]
        m_prev = m_scr[h:h + 1, 0:1]

        d = jnp.where(causal, b_col + (i_row - b_row), NEG)
        inter = b_col + m_prev
        m_t = jnp.maximum(inter, jnp.max(d, axis=-1, keepdims=True))
        w_inter = jnp.exp(inter - m_t)
        e = jnp.exp(d - m_t)
        s = lax.dot_general(q, kb, (((1,), (1,)), ((), ())), preferred_element_type=jnp.float32)
        p = (e * s).astype(jnp.bfloat16)
        ct = ct_scr[h]
        num_aug = (w_inter * jnp.dot(q, ct.astype(jnp.bfloat16), preferred_element_type=jnp.float32)
                   + jnp.dot(p, v_aug, preferred_element_type=jnp.float32))
        num = num_aug[:, :dh]
        den = num_aug[:, dh:dh + 1]
        hh = num / jnp.maximum(jnp.abs(den), jnp.exp(-m_t))
        hh = hh * lax.rsqrt(jnp.mean(hh * hh, axis=-1, keepdims=True) + EPS)
        y = (hh * ng_ref[:, sl] * _sigmoid(ao_ref[:, sl].astype(jnp.float32))
             * _silu(az_ref[:, sl].astype(jnp.float32)))
        y_ref[:, sl] = y.astype(jnp.bfloat16)

        b_last = b_col[L - 1:L, :]
        dec = b_last - b_col + i_col
        m_new = jnp.maximum(b_last + m_prev, jnp.max(dec, axis=0, keepdims=True))
        w_s = jnp.exp(dec - m_new)
        w_old = jnp.exp(b_last + m_prev - m_new)
        upd = jnp.dot((kf * w_s).T.astype(jnp.bfloat16), v_aug, preferred_element_type=jnp.float32)
        ct_scr[h] = w_old * ct + upd
        m_scr[h:h + 1, :] = jnp.broadcast_to(m_new, (1, LANES))


def _mlstm(proj, conv_w, gcol, grow, ml_norm_g, batch, seq):
    m = proj.shape[0]
    L = ML_CHUNK
    nc = seq // L

    def col(c):
        return pl.BlockSpec((L, D_MODEL), lambda b, i, c=c: (b * nc + i, c))

    return pl.pallas_call(
        _mlstm_kernel,
        grid=(batch, nc),
        in_specs=[
            col(C_AQ), col(C_AK), col(C_AV), col(C_AO), col(C_AZ),
            pl.BlockSpec((CONV_WIDTH, 2 * D_MODEL), lambda b, i: (0, 0)),
            pl.BlockSpec((L, LANES), lambda b, i: (b * nc + i, 0)),
            pl.BlockSpec((1, 16, L), lambda b, i: (b, 0, i)),
            pl.BlockSpec((1, D_MODEL), lambda b, i: (0, 0)),
        ],
        out_specs=pl.BlockSpec((L, D_MODEL), lambda b, i: (b * nc + i, 0)),
        out_shape=jax.ShapeDtypeStruct((m, D_MODEL), jnp.bfloat16),
        scratch_shapes=[
            pltpu.VMEM((CONV_HALO, D_MODEL), jnp.float32),
            pltpu.VMEM((CONV_HALO, D_MODEL), jnp.float32),
            pltpu.VMEM(((CONV_WIDTH - 1) * L, L), jnp.bfloat16),
            pltpu.VMEM((ML_HEADS, ML_HEAD_DIM, ML_HEAD_DIM + LANES), jnp.float32),
            pltpu.VMEM((8, LANES), jnp.float32),
        ],
        compiler_params=pltpu.CompilerParams(
            dimension_semantics=("arbitrary", "arbitrary"), vmem_limit_bytes=VMEM_LIMIT),
        name="mlstm",
    )(proj, proj, proj, proj, proj, conv_w, gcol, grow, ml_norm_g)


def _fox_kernel(q_ref, k_ref, vt_ref, z_ref, g_ref, y_ref, qa_scr, ka_scr, st_scr, *, seq):
    tq, tk = FOX_TQ, FOX_TK
    h = pl.program_id(1)
    i = pl.program_id(2)
    lane1 = lax.broadcasted_iota(jnp.int32, (1, LANES), 1)

    @pl.when(i == 0)
    def _():
        src = lax.broadcasted_iota(jnp.int32, (LANES, LANES), 0)
        dst = lax.broadcasted_iota(jnp.int32, (LANES, LANES), 1)
        piece = jnp.where(src == G_FHI + h, 0, jnp.where(src == G_FMID + h, 1,
                                                         jnp.where(src == G_FLO + h, 2, -1)))
        is_piece = piece >= 0
        one = src == G_ONE
        pq = ((is_piece & ((dst == piece) | (dst == 6))) | (one & (dst >= 3) & (dst < 6)))
        pk = ((one & ((dst < 3) | (dst == 7))).astype(jnp.float32)
              - (is_piece & (dst == piece + 3)).astype(jnp.float32))
        pq = pq.astype(jnp.float32).astype(jnp.bfloat16)
        pk = pk.astype(jnp.bfloat16)

        lane_k = lax.broadcasted_iota(jnp.int32, (tk, 2 * FOX_HEAD_DIM), 1)
        ka_scr[0:tk, :] = jnp.where(lane_k == FOX_HEAD_DIM + 3, NEG, 0.0).astype(jnp.bfloat16)

        lane_q = lax.broadcasted_iota(jnp.int32, (tk, LANES), 1)

        def build(c, carry):
            kmax, qn_vec, g_vec, fe_vec, sl_vec = carry
            r0 = pl.multiple_of(c * tk, tk)
            g = g_ref[pl.ds(r0, tk), :].astype(jnp.bfloat16)
            qx = jnp.dot(g, pq, preferred_element_type=jnp.float32)
            kx = jnp.dot(g, pk, preferred_element_type=jnp.float32)
            f = qx[:, 6:7]
            qs = (q_ref[pl.ds(r0, tk), :].astype(jnp.float32)
                  * (FOX_HEAD_DIM ** -0.5 * LOG2E)).astype(jnp.bfloat16)
            kk = k_ref[pl.ds(r0, tk), :]

            qf = qs.astype(jnp.float32)
            kf = kk.astype(jnp.float32)
            qn = jnp.sum(qf * qf, axis=1, keepdims=True)
            kn = jnp.sum(kf * kf, axis=1, keepdims=True)
            diag = jnp.sum(qf * kf, axis=1, keepdims=True)
            here = lane1 == c
            kmax = jnp.maximum(kmax, jnp.max(kn, axis=0, keepdims=True))
            qmax = jnp.max(qn, axis=0, keepdims=True)
            qn_vec = jnp.where(here, qmax, qn_vec)
            g_vec = jnp.where(here, jnp.max(f - diag, axis=0, keepdims=True), g_vec)
            fe_vec = jnp.where(here, f[tk - 1:tk, :], fe_vec)
            ref = (-jnp.sqrt(qmax * kmax)).astype(jnp.bfloat16).astype(jnp.float32)
            qx = jnp.where(lane_q == 7, ref, qx)
            sl_vec = jnp.where(here, -ref - jnp.min(diag, axis=0, keepdims=True), sl_vec)

            qa_scr[pl.ds(r0, tk), :] = jnp.concatenate([qs, qx.astype(jnp.bfloat16)], axis=1)
            ka_scr[pl.ds(r0 + tk, tk), :] = jnp.concatenate([kk, kx.astype(jnp.bfloat16)], axis=1)
            return kmax, qn_vec, g_vec, fe_vec, sl_vec

        zrow = jnp.zeros((1, LANES), jnp.float32)
        kmax, qn_vec, g_vec, fe_vec, sl_vec = lax.fori_loop(
            0, seq // tk, build, (jnp.zeros((1, 1), jnp.float32), zrow, zrow, zrow, zrow), unroll=16)
        st_scr[2:3, :] = sl_vec
        st_scr[0:1, :] = jnp.sqrt(qn_vec * kmax) + g_vec
        st_scr[1:2, :] = fe_vec

    blks = [i * FOX_PAIR + u for u in range(FOX_PAIR)]
    blk_u = blks[0] + lax.broadcasted_iota(jnp.int32, (FOX_PAIR, LANES), 0)
    lane_u = lax.broadcasted_iota(jnp.int32, (FOX_PAIR, LANES), 1)
    bound = jnp.sum(jnp.where(lane_u == blk_u, st_scr[0:1, :], 0.0), axis=1, keepdims=True)
    needed = (lane_u < blk_u) & jnp.logical_not(st_scr[1:2, :] >= bound + SKIP_MARGIN * LOG2E)
    n_back = jnp.max(blk_u - jnp.min(jnp.where(needed, lane_u, blk_u), axis=1, keepdims=True))
    qas = [qa_scr[pl.ds(pl.multiple_of(b * tq, tq), tq), :] for b in blks]

    def scores(t, diagonal):
        kbs, parts = [], []
        for u in range(FOX_PAIR):
            kb = jnp.maximum(blks[u] - t + 1, 0)
            k0 = pl.multiple_of(kb * tk, tk)
            s = lax.dot_general(ka_scr[pl.ds(k0, tk), :], qas[u], (((1,), (1,)), ((), ())),
                                preferred_element_type=jnp.float32)
            if diagonal:
                key_idx = lax.broadcasted_iota(jnp.int32, (tk, tq), 0)
                qry_idx = lax.broadcasted_iota(jnp.int32, (tk, tq), 1)
                s = jnp.where(key_idx <= qry_idx, s, NEG)
            kbs.append(kb)
            parts.append(s)
        return jnp.concatenate(parts, axis=1), kbs

    def weighted_values(p, kbs):
        p = p.astype(jnp.bfloat16)
        return jnp.concatenate(
            [jnp.dot(vt_ref[jnp.maximum(kbs[u] - 1, 0)], p[:, u * tq:(u + 1) * tq],
                     preferred_element_type=jnp.float32)
             for u in range(FOX_PAIR)], axis=1)

    def attend_online(carry, t, diagonal):
        m, l, acc = carry
        s, kbs = scores(t, diagonal)
        m_new = jnp.maximum(m, jnp.max(s, axis=0, keepdims=True))
        alpha = jnp.exp2(m - m_new)
        p = jnp.exp2(s - m_new)
        l = alpha * l + jnp.sum(p, axis=0, keepdims=True)
        return m_new, l, alpha * acc + weighted_values(p, kbs)

    def attend_fixed(carry, t, diagonal):
        l, acc = carry
        s, kbs = scores(t, diagonal)
        p = jnp.exp2(s)
        return l + jnp.sum(p, axis=0, keepdims=True), acc + weighted_values(p, kbs)

    rows = FOX_PAIR * tq
    zl = jnp.zeros((1, rows), jnp.float32)
    zacc = jnp.zeros((FOX_HEAD_DIM, rows), jnp.float32)

    def fixed_path():
        carry = attend_fixed((zl, zacc), 0, True)
        return lax.fori_loop(1, n_back + 1, lambda t, c: attend_fixed(c, t, False), carry)

    def online_path():
        carry = attend_online((jnp.full((1, rows), NEG, jnp.float32), zl, zacc), 0, True)
        _, l, acc = lax.fori_loop(1, n_back + 1, lambda t, c: attend_online(c, t, False), carry)
        return l, acc

    mine = (lane1 >= blks[0]) & (lane1 < blks[0] + FOX_PAIR)
    worst = jnp.max(jnp.where(mine, st_scr[2:3, :], 0.0))
    l, acc = lax.cond(worst <= FIXED_REF_SLACK, fixed_path, online_path)
    y_ref[...] = ((acc / l).T * _silu(z_ref[...].astype(jnp.float32))).astype(jnp.bfloat16)


def _fox(proj, vt, gcol, batch, seq):
    m = proj.shape[0]
    rows = FOX_PAIR * FOX_TQ
    nq = seq // rows
    assert FOX_TQ == FOX_TK and seq % rows == 0 and seq // FOX_TQ <= LANES
    hb = D_MODEL // FOX_HEAD_DIM

    def head(c):
        return pl.BlockSpec((seq, FOX_HEAD_DIM), lambda b, h, i, c=c: (b, c * hb + h))

    return pl.pallas_call(
        functools.partial(_fox_kernel, seq=seq),
        grid=(batch, FOX_HEADS, nq),
        in_specs=[
            head(C_BQ), head(C_BK),
            pl.BlockSpec((seq // FOX_TK, FOX_HEAD_DIM, FOX_TK), lambda b, h, i: (b, h, 0)),
            pl.BlockSpec((rows, FOX_HEAD_DIM), lambda b, h, i: (b * nq + i, C_BZ * hb + h)),
            pl.BlockSpec((seq, LANES), lambda b, h, i: (b, 0)),
        ],
        out_specs=pl.BlockSpec((rows, FOX_HEAD_DIM), lambda b, h, i: (b * nq + i, h)),
        out_shape=jax.ShapeDtypeStruct((m, D_MODEL), jnp.bfloat16),
        scratch_shapes=[
            pltpu.VMEM((seq, 2 * FOX_HEAD_DIM), jnp.bfloat16),
            pltpu.VMEM((seq + FOX_TK, 2 * FOX_HEAD_DIM), jnp.bfloat16),
            pltpu.VMEM((8, LANES), jnp.float32),
        ],
        compiler_params=pltpu.CompilerParams(
            dimension_semantics=("arbitrary", "arbitrary", "arbitrary"), vmem_limit_bytes=VMEM_LIMIT),
        name="fox",
    )(proj, proj, vt, proj, gcol)


def _merge_kernel(ya_ref, yb_ref, cu_ref, halo_ref, cz_ref, gates_ref, x_ref, pw_ref, ps_ref,
                  wb_ref, wo_ref, fg_ref, o_ref, *, seq, final):
    tm = MERGE_TM
    gd = POOL_GROUP_DIM
    i = pl.program_id(0)
    t0 = (i * tm) % seq
    pos = t0 + lax.broadcasted_iota(jnp.int32, (tm, 1), 0)
    halo = jnp.where(t0 == 0, 0.0, halo_ref[...].astype(jnp.float32))
    u = cu_ref[...].astype(jnp.float32)
    ext = jnp.concatenate([halo, u], axis=0)

    ys = []
    for g, win in enumerate(POOL_WINDOWS):
        sl = slice(g * gd, (g + 1) * gd)
        acc = ext[:, sl]
        step = 1
        while step < win:
            acc = acc + pltpu.roll(acc, step, 0)
            step *= 2
        cnt = jnp.minimum(pos + 1, win).astype(jnp.float32)
        d = acc[HALO:, :] / cnt - u[:, sl]
        ys.append(jnp.dot(d.astype(jnp.bfloat16), pw_ref[g], preferred_element_type=jnp.float32))
    yc = jnp.concatenate(ys, axis=1) * ps_ref[...]
    yc = yc * _silu(cz_ref[...].astype(jnp.float32))

    merged = jnp.zeros((tm, D_MODEL), jnp.float32)
    for n, y in enumerate((ya_ref[...], yb_ref[...], yc.astype(jnp.bfloat16))):
        yb = jnp.dot(y, wb_ref[n], preferred_element_type=jnp.float32)
        gate = _sigmoid(gates_ref[:, n * D_MODEL:(n + 1) * D_MODEL].astype(jnp.float32))
        merged = merged + gate * yb
    out = x_ref[...] + jnp.dot(merged.astype(jnp.bfloat16), wo_ref[...],
                               preferred_element_type=jnp.float32)
    if final:
        r = lax.rsqrt(jnp.mean(out * out, axis=-1, keepdims=True) + EPS)
        out = (out * r) * fg_ref[...]
    o_ref[...] = out


def _merge(ya, yb, proj, x2, pool_w, pool_scale, w_branch, w_out, final_g, seq, final, layer):
    m = x2.shape[0]
    tm = MERGE_TM
    hpt = tm // HALO

    def rowblk(c, width=1):
        return pl.BlockSpec((tm, width * D_MODEL), lambda i, c=c: (i, c))

    const2 = lambda i: (0, 0)
    layer3 = lambda i: (layer, 0, 0)
    layer4 = lambda i: (layer, 0, 0, 0)
    return pl.pallas_call(
        functools.partial(_merge_kernel, seq=seq, final=final),
        grid=(m // tm,),
        in_specs=[
            rowblk(0), rowblk(0), rowblk(C_CU),
            pl.BlockSpec((HALO, D_MODEL), lambda i: (jnp.maximum(i * hpt - 1, 0), C_CU)),
            rowblk(C_CZ),
            pl.BlockSpec((tm, N_BRANCH * D_MODEL), lambda i: (i, C_GATES // N_BRANCH)),
            rowblk(0),
            pl.BlockSpec((None, POOL_GROUPS, POOL_GROUP_DIM, POOL_GROUP_DIM), layer4),
            pl.BlockSpec((1, D_MODEL), const2),
            pl.BlockSpec((None, N_BRANCH, D_MODEL, D_MODEL), layer4),
            pl.BlockSpec((None, D_MODEL, D_MODEL), layer3),
            pl.BlockSpec((1, D_MODEL), const2),
        ],
        out_specs=pl.BlockSpec((tm, D_MODEL), lambda i: (i, 0)),
        out_shape=jax.ShapeDtypeStruct((m, D_MODEL), jnp.float32),
        compiler_params=pltpu.CompilerParams(
            dimension_semantics=("arbitrary",), vmem_limit_bytes=VMEM_LIMIT),
        name="merge",
    )(ya, yb, proj, proj, proj, proj, x2, pool_w, pool_scale, w_branch, w_out, final_g)


O_AI = 5 * D_MODEL
O_BQ = O_AI + 2 * ML_HEADS
O_BF = O_BQ + 4 * D_MODEL
O_CU = O_BF + FOX_HEADS
O_G = O_CU + 2 * D_MODEL
N_IN = O_G + N_BRANCH * D_MODEL
assert all(o % 8 == 0 for o in (O_BQ, O_CU, O_G))


def _w_row(j):
    return jnp.where(j < C_AQ, O_G + j * D_MODEL,
                     jnp.where(j < C_BQ, (j - C_AQ) * D_MODEL,
                               jnp.where(j < C_CU, O_BQ + (j - C_BQ) * D_MODEL,
                                         O_CU + (j - C_CU) * D_MODEL)))


def kernel(x, norm_g, w_in, conv_w, ml_bi, ml_bf, ml_norm_g, fox_bf, pool_w, pool_scale, w_branch,
           w_out, final_g):
    batch, seq, d = x.shape
    depth = norm_g.shape[0]
    assert d == D_MODEL and all(seq % t == 0 for t in (IN_TM, MERGE_TM, ML_CHUNK, GATES_ROWS))
    assert GATES_ROWS % ML_CHUNK == 0
    x2 = x.reshape(batch * seq, d)
    assert IN_TN == D_MODEL and w_in.shape[1:] == (D_MODEL, N_IN)
    wt = jnp.swapaxes(w_in, 1, 2)
    pool_wb = pool_w.astype(jnp.bfloat16)
    w_branchb = w_branch.astype(jnp.bfloat16)
    w_outb = w_out.astype(jnp.bfloat16)
    for l in range(depth):
        gate_bias = jnp.pad(jnp.concatenate([ml_bi[l], ml_bf[l], fox_bf[l]]),
                            (0, LANES - 2 * ML_HEADS - FOX_HEADS)).reshape(1, LANES)
        proj, gate_pre, vt = _in_proj(x2, norm_g[l].reshape(1, d), wt, l)
        gcol, grow = _gates(gate_pre, gate_bias, batch, seq)
        ya = _mlstm(proj, conv_w[l], gcol, grow, ml_norm_g[l].reshape(1, d), batch, seq)
        yb = _fox(proj, vt, gcol, batch, seq)
        x2 = _merge(ya, yb, proj, x2, pool_wb, pool_scale[l].reshape(1, d), w_branchb, w_outb,
                    final_g.reshape(1, d), seq, final=(l == depth - 1), layer=l)
    return x2.reshape(batch, seq, d)
```

```python
import functools

import jax
import jax.numpy as jnp
from jax import lax
from jax.experimental import pallas as pl
from jax.experimental.pallas import tpu as pltpu

D_MODEL = 1024
ML_HEADS = 4
ML_HEAD_DIM = 256
CONV_WIDTH = 4
FOX_HEADS = 8
FOX_HEAD_DIM = 128
POOL_GROUPS = 4
POOL_GROUP_DIM = 256
POOL_WINDOWS = (2, 4, 8, 16)
N_BRANCH = 3
EPS = 1e-6

LANES = 128
HALO = 16
CONV_HALO = 8
NEG = -1e30
FIXED_REF_SLACK = 100.0
SKIP_MARGIN = 32.0

C_GATES = 0
C_AQ, C_AK, C_AV, C_AO, C_AZ, C_BQ, C_BK, C_BV, C_BZ, C_CU, C_CZ = range(N_BRANCH, N_BRANCH + 11)
N_MAIN = 14 * D_MODEL
G_AI, G_AF, G_BF = 0, 4, 8
G_FHI, G_FMID, G_FLO, G_ONE = 16, 24, 32, 127

ML_CHUNK = 256
IN_TM, IN_TN = 2048, 1024
FOX_TQ = 256
FOX_TK = 256
FOX_PAIR = 16
GATES_ROWS = 1024
LOG2E = 1.4426950408889634
MERGE_TM = 512
VMEM_LIMIT = 56 * 1024 * 1024


def _sigmoid(x):
    return 1.0 / (1.0 + jnp.exp(-x))


def _silu(x):
    return x * _sigmoid(x)


def _in_proj_kernel(x_ref, g_ref, w_ref, wga_ref, wgb_ref, proj_ref, gate_ref, vt_ref, h_scr):
    @pl.when(pl.program_id(1) == 0)
    def _():
        x = x_ref[...]
        r = lax.rsqrt(jnp.mean(x * x, axis=-1, keepdims=True) + EPS)
        h = ((x * r) * g_ref[...]).astype(jnp.bfloat16)
        h_scr[...] = h
        n_gate = 2 * ML_HEADS + FOX_HEADS
        wg = jnp.concatenate([wga_ref[0], wgb_ref[0], jnp.zeros((LANES - n_gate, D_MODEL), jnp.float32)],
                             axis=0).astype(jnp.bfloat16)
        gate_ref[...] = lax.dot_general(h, wg, (((1,), (1,)), ((), ())), preferred_element_type=jnp.float32)

    is_bv = pl.program_id(1) == C_BV

    @pl.when(jnp.logical_not(is_bv))
    def _():
        proj_ref[...] = lax.dot_general(h_scr[...], w_ref[0].astype(jnp.bfloat16), (((1,), (1,)), ((), ())),
                                        preferred_element_type=jnp.float32).astype(jnp.bfloat16)

    @pl.when(is_bv)
    def _():
        vt = lax.dot_general(w_ref[0].astype(jnp.bfloat16), h_scr[...], (((1,), (1,)), ((), ())),
                             preferred_element_type=jnp.float32).astype(jnp.bfloat16)
        for c in range(IN_TM // FOX_TK):
            vt_ref[c] = vt[:, c * FOX_TK:(c + 1) * FOX_TK]
        proj_ref[...] = jnp.zeros_like(proj_ref)


def _in_proj(x2, norm_g, wt, layer):
    m = x2.shape[0]
    assert 2 * ML_HEADS == 8 and FOX_HEADS == 8

    def gate_rows(row):
        return pl.BlockSpec((pl.Element(1), pl.Element(8), pl.Element(D_MODEL)), lambda i, j: (layer, row, 0))

    return pl.pallas_call(
        _in_proj_kernel,
        grid=(m // IN_TM, N_MAIN // IN_TN),
        in_specs=[
            pl.BlockSpec((IN_TM, D_MODEL), lambda i, j: (i, 0)),
            pl.BlockSpec((1, D_MODEL), lambda i, j: (0, 0)),
            pl.BlockSpec((pl.Element(1), pl.Element(IN_TN), pl.Element(D_MODEL)),
                         lambda i, j: (layer, pl.multiple_of(_w_row(j), 8), 0)),
            gate_rows(O_AI), gate_rows(O_BF),
        ],
        out_specs=[
            pl.BlockSpec((IN_TM, IN_TN), lambda i, j: (i, j)),
            pl.BlockSpec((IN_TM, LANES), lambda i, j: (i, 0)),
            pl.BlockSpec((IN_TM // FOX_TK, D_MODEL, FOX_TK), lambda i, j: (i, 0, 0)),
        ],
        out_shape=[
            jax.ShapeDtypeStruct((m, N_MAIN), jnp.bfloat16),
            jax.ShapeDtypeStruct((m, LANES), jnp.float32),
            jax.ShapeDtypeStruct((m // FOX_TK, D_MODEL, FOX_TK), jnp.bfloat16),
        ],
        scratch_shapes=[pltpu.VMEM((IN_TM, D_MODEL), jnp.bfloat16)],
        compiler_params=pltpu.CompilerParams(
            dimension_semantics=("arbitrary", "arbitrary"), vmem_limit_bytes=VMEM_LIMIT),
        name="in_proj",
    )(x2, norm_g, wt, wt, wt)


def _gates_kernel(g_ref, bias_ref, col_ref, row_ref, carry_scr):
    @pl.when(pl.program_id(1) == 0)
    def _():
        carry_scr[...] = jnp.zeros_like(carry_scr)

    row = lax.broadcasted_iota(jnp.int32, (ML_CHUNK, LANES), 0)
    lane = lax.broadcasted_iota(jnp.int32, (ML_CHUNK, LANES), 1)
    carry = carry_scr[...]
    for c in range(GATES_ROWS // ML_CHUNK):
        rs = slice(c * ML_CHUNK, (c + 1) * ML_CHUNK)
        pre = g_ref[rs, :] + bias_ref[...]
        ls = jnp.minimum(pre, 0.0) - jnp.log1p(jnp.exp(-jnp.abs(pre)))
        cs = ls
        k = 1
        while k < ML_CHUNK:
            cs = cs + jnp.where(row >= k, pltpu.roll(cs, k, 0), 0.0)
            k *= 2
        run = cs + carry
        carry = run[ML_CHUNK - 1:ML_CHUNK, :]
        out = jnp.where(lane < G_AF, pre, jnp.where(lane < G_BF, cs, run))
        row_ref[0, :, rs] = out.T[0:16, :]
        f2 = run * LOG2E
        hi = f2.astype(jnp.bfloat16).astype(jnp.float32)
        r1 = f2 - hi
        mid = r1.astype(jnp.bfloat16).astype(jnp.float32)
        lo = r1 - mid
        pieces = jnp.where(lane < G_FMID, pltpu.roll(hi, G_FHI - G_BF, 1),
                           jnp.where(lane < G_FLO, pltpu.roll(mid, G_FMID - G_BF, 1),
                                     jnp.where(lane < G_FLO + FOX_HEADS, pltpu.roll(lo, G_FLO - G_BF, 1),
                                               jnp.where(lane == G_ONE, 1.0, 0.0))))
        col_ref[rs, :] = jnp.where(lane < G_FHI, out, pieces)
    carry_scr[...] = carry


def _gates(gate_pre, gate_bias, batch, seq):
    m = gate_pre.shape[0]
    nc = seq // GATES_ROWS
    return pl.pallas_call(
        _gates_kernel,
        grid=(batch, nc),
        in_specs=[
            pl.BlockSpec((GATES_ROWS, LANES), lambda b, c: (b * nc + c, 0)),
            pl.BlockSpec((1, LANES), lambda b, c: (0, 0)),
        ],
        out_specs=[
            pl.BlockSpec((GATES_ROWS, LANES), lambda b, c: (b * nc + c, 0)),
            pl.BlockSpec((1, 16, GATES_ROWS), lambda b, c: (b, 0, c)),
        ],
        out_shape=[
            jax.ShapeDtypeStruct((m, LANES), jnp.float32),
            jax.ShapeDtypeStruct((batch, 16, seq), jnp.float32),
        ],
        scratch_shapes=[pltpu.VMEM((1, LANES), jnp.float32)],
        compiler_params=pltpu.CompilerParams(dimension_semantics=("arbitrary", "arbitrary")),
        name="gates",
    )(gate_pre, gate_bias)


def _mlstm_kernel(aq_ref, ak_ref, av_ref, ao_ref, az_ref, cw_ref, gc_ref, gr_ref, ng_ref,
                  y_ref, qbuf, kbuf, sh_scr, ct_scr, m_scr):
    L = ML_CHUNK
    dh = ML_HEAD_DIM

    n_sh = CONV_WIDTH - 1

    @pl.when(pl.program_id(1) == 0)
    def _():
        qbuf[...] = jnp.zeros_like(qbuf)
        kbuf[...] = jnp.zeros_like(kbuf)
        ct_scr[...] = jnp.zeros_like(ct_scr)
        m_scr[...] = jnp.zeros_like(m_scr)
        r = lax.broadcasted_iota(jnp.int32, (n_sh * L, L), 0)
        c = lax.broadcasted_iota(jnp.int32, (n_sh * L, L), 1)
        hit = functools.reduce(jnp.logical_or,
                               [(r >= j * L) & (r < (j + 1) * L) & (c == r - j * L - (j + 1)) for j in range(n_sh)])
        sh_scr[...] = hit.astype(jnp.float32).astype(jnp.bfloat16)

    row8 = lax.broadcasted_iota(jnp.int32, (CONV_HALO, D_MODEL), 0)

    def conv(x_ref, prev, w):
        xb = x_ref[...]
        x = xb.astype(jnp.float32)
        shifted = jnp.dot(sh_scr[...], xb, preferred_element_type=jnp.float32)
        acc = x * w[n_sh:n_sh + 1, :]
        head = jnp.zeros((CONV_HALO, D_MODEL), jnp.float32)
        tail = prev[...]
        for j in range(1, n_sh + 1):
            wj = w[n_sh - j:n_sh - j + 1, :]
            acc = acc + shifted[(j - 1) * L:j * L, :] * wj
            head = head + jnp.where(row8 < j, pltpu.roll(tail, j, 0), 0.0) * wj
        prev[...] = x[L - CONV_HALO:, :]
        return jnp.concatenate([acc[:CONV_HALO] + head, acc[CONV_HALO:]], axis=0)

    cw = cw_ref[...]
    qc = (_silu(conv(aq_ref, qbuf, cw[:, :D_MODEL])) * (dh ** -0.5)).astype(jnp.bfloat16)
    kc = _silu(conv(ak_ref, kbuf, cw[:, D_MODEL:]))

    gcol = gc_ref[...]
    grow = gr_ref[0]
    t_idx = lax.broadcasted_iota(jnp.int32, (L, L), 0)
    s_idx = lax.broadcasted_iota(jnp.int32, (L, L), 1)
    causal = s_idx <= t_idx
    ones_col = (lax.broadcasted_iota(jnp.int32, (L, LANES), 1) == 0).astype(jnp.bfloat16)

    for h in range(ML_HEADS):
        sl = slice(h * dh, (h + 1) * dh)
        q = qc[:, sl]
        kf = kc[:, sl]
        kb = kf.astype(jnp.bfloat16)
        v_aug = jnp.concatenate([av_ref[:, sl], ones_col], axis=1)
        i_col = gcol[:, G_AI + h:G_AI + h + 1]
        b_col = gcol[:, G_AF + h:G_AF + h + 1]
        i_row = grow[G_AI + h:G_AI + h + 1, :]
        b_row = grow[G_AF + h:G_AF + h + 1, :]
        m_prev = m_scr[h:h + 1, 0:1]

        d = jnp.where(causal, b_col + (i_row - b_row), NEG)
        inter = b_col + m_prev
        m_t = jnp.maximum(inter, jnp.max(d, axis=-1, keepdims=True))
        w_inter = jnp.exp(inter - m_t)
        e = jnp.exp(d - m_t)
        s = lax.dot_general(q, kb, (((1,), (1,)), ((), ())), preferred_element_type=jnp.float32)
        p = (e * s).astype(jnp.bfloat16)
        ct = ct_scr[h]
        num_aug = (w_inter * jnp.dot(q, ct.astype(jnp.bfloat16), preferred_element_type=jnp.float32)
                   + jnp.dot(p, v_aug, preferred_element_type=jnp.float32))
        num = num_aug[:, :dh]
        den = num_aug[:, dh:dh + 1]
        hh = num / jnp.maximum(jnp.abs(den), jnp.exp(-m_t))
        hh = hh * lax.rsqrt(jnp.mean(hh * hh, axis=-1, keepdims=True) + EPS)
        y = (hh * ng_ref[:, sl] * _sigmoid(ao_ref[:, sl].astype(jnp.float32))
             * _silu(az_ref[:, sl].astype(jnp.float32)))
        y_ref[:, sl] = y.astype(jnp.bfloat16)

        b_last = b_col[L - 1:L, :]
        dec = b_last - b_col + i_col
        m_new = jnp.maximum(b_last + m_prev, jnp.max(dec, axis=0, keepdims=True))
        w_s = jnp.exp(dec - m_new)
        w_old = jnp.exp(b_last + m_prev - m_new)
        upd = jnp.dot((kf * w_s).T.astype(jnp.bfloat16), v_aug, preferred_element_type=jnp.float32)
        ct_scr[h] = w_old * ct + upd
        m_scr[h:h + 1, :] = jnp.broadcast_to(m_new, (1, LANES))


def _mlstm(proj, conv_w, gcol, grow, ml_norm_g, batch, seq):
    m = proj.shape[0]
    L = ML_CHUNK
    nc = seq // L

    def col(c):
        return pl.BlockSpec((L, D_MODEL), lambda b, i, c=c: (b * nc + i, c))

    return pl.pallas_call(
        _mlstm_kernel,
        grid=(batch, nc),
        in_specs=[
            col(C_AQ), col(C_AK), col(C_AV), col(C_AO), col(C_AZ),
            pl.BlockSpec((CONV_WIDTH, 2 * D_MODEL), lambda b, i: (0, 0)),
            pl.BlockSpec((L, LANES), lambda b, i: (b * nc + i, 0)),
            pl.BlockSpec((1, 16, L), lambda b, i: (b, 0, i)),
            pl.BlockSpec((1, D_MODEL), lambda b, i: (0, 0)),
        ],
        out_specs=pl.BlockSpec((L, D_MODEL), lambda b, i: (b * nc + i, 0)),
        out_shape=jax.ShapeDtypeStruct((m, D_MODEL), jnp.bfloat16),
        scratch_shapes=[
            pltpu.VMEM((CONV_HALO, D_MODEL), jnp.float32),
            pltpu.VMEM((CONV_HALO, D_MODEL), jnp.float32),
            pltpu.VMEM(((CONV_WIDTH - 1) * L, L), jnp.bfloat16),
            pltpu.VMEM((ML_HEADS, ML_HEAD_DIM, ML_HEAD_DIM + LANES), jnp.float32),
            pltpu.VMEM((8, LANES), jnp.float32),
        ],
        compiler_params=pltpu.CompilerParams(
            dimension_semantics=("arbitrary", "arbitrary"), vmem_limit_bytes=VMEM_LIMIT),
        name="mlstm",
    )(proj, proj, proj, proj, proj, conv_w, gcol, grow, ml_norm_g)


def _fox_kernel(q_ref, k_ref, vt_ref, z_ref, g_ref, y_ref, qa_scr, ka_scr, st_scr, *, seq):
    tq, tk = FOX_TQ, FOX_TK
    h = pl.program_id(1)
    i = pl.program_id(2)
    lane1 = lax.broadcasted_iota(jnp.int32, (1, LANES), 1)

    @pl.when(i == 0)
    def _():
        src = lax.broadcasted_iota(jnp.int32, (LANES, LANES), 0)
        dst = lax.broadcasted_iota(jnp.int32, (LANES, LANES), 1)
        piece = jnp.where(src == G_FHI + h, 0, jnp.where(src == G_FMID + h, 1,
                                                         jnp.where(src == G_FLO + h, 2, -1)))
        is_piece = piece >= 0
        one = src == G_ONE
        pq = ((is_piece & ((dst == piece) | (dst == 6))) | (one & (dst >= 3) & (dst < 6)))
        pk = ((one & ((dst < 3) | (dst == 7))).astype(jnp.float32)
              - (is_piece & (dst == piece + 3)).astype(jnp.float32))
        pq = pq.astype(jnp.float32).astype(jnp.bfloat16)
        pk = pk.astype(jnp.bfloat16)

        lane_k = lax.broadcasted_iota(jnp.int32, (tk, 2 * FOX_HEAD_DIM), 1)
        ka_scr[0:tk, :] = jnp.where(lane_k == FOX_HEAD_DIM + 3, NEG, 0.0).astype(jnp.bfloat16)

        lane_q = lax.broadcasted_iota(jnp.int32, (tk, LANES), 1)

        def build(c, carry):
            kmax, qn_vec, g_vec, fe_vec, sl_vec = carry
            r0 = pl.multiple_of(c * tk, tk)
            g = g_ref[pl.ds(r0, tk), :].astype(jnp.bfloat16)
            qx = jnp.dot(g, pq, preferred_element_type=jnp.float32)
            kx = jnp.dot(g, pk, preferred_element_type=jnp.float32)
            f = qx[:, 6:7]
            qs = (q_ref[pl.ds(r0, tk), :].astype(jnp.float32)
                  * (FOX_HEAD_DIM ** -0.5 * LOG2E)).astype(jnp.bfloat16)
            kk = k_ref[pl.ds(r0, tk), :]

            qf = qs.astype(jnp.float32)
            kf = kk.astype(jnp.float32)
            qn = jnp.sum(qf * qf, axis=1, keepdims=True)
            kn = jnp.sum(kf * kf, axis=1, keepdims=True)
            diag = jnp.sum(qf * kf, axis=1, keepdims=True)
            here = lane1 == c
            kmax = jnp.maximum(kmax, jnp.max(kn, axis=0, keepdims=True))
            qmax = jnp.max(qn, axis=0, keepdims=True)
            qn_vec = jnp.where(here, qmax, qn_vec)
            g_vec = jnp.where(here, jnp.max(f - diag, axis=0, keepdims=True), g_vec)
            fe_vec = jnp.where(here, f[tk - 1:tk, :], fe_vec)
            ref = (-jnp.sqrt(qmax * kmax)).astype(jnp.bfloat16).astype(jnp.float32)
            qx = jnp.where(lane_q == 7, ref, qx)
            sl_vec = jnp.where(here, -ref - jnp.min(diag, axis=0, keepdims=True), sl_vec)

            qa_scr[pl.ds(r0, tk), :] = jnp.concatenate([qs, qx.astype(jnp.bfloat16)], axis=1)
            ka_scr[pl.ds(r0 + tk, tk), :] = jnp.concatenate([kk, kx.astype(jnp.bfloat16)], axis=1)
            return kmax, qn_vec, g_vec, fe_vec, sl_vec

        zrow = jnp.zeros((1, LANES), jnp.float32)
        kmax, qn_vec, g_vec, fe_vec, sl_vec = lax.fori_loop(
            0, seq // tk, build, (jnp.zeros((1, 1), jnp.float32), zrow, zrow, zrow, zrow), unroll=32)
        st_scr[2:3, :] = sl_vec
        st_scr[0:1, :] = jnp.sqrt(qn_vec * kmax) + g_vec
        st_scr[1:2, :] = fe_vec

    blks = [i * FOX_PAIR + u for u in range(FOX_PAIR)]
    blk_u = blks[0] + lax.broadcasted_iota(jnp.int32, (FOX_PAIR, LANES), 0)
    lane_u = lax.broadcasted_iota(jnp.int32, (FOX_PAIR, LANES), 1)
    bound = jnp.sum(jnp.where(lane_u == blk_u, st_scr[0:1, :], 0.0), axis=1, keepdims=True)
    needed = (lane_u < blk_u) & jnp.logical_not(st_scr[1:2, :] >= bound + SKIP_MARGIN * LOG2E)
    n_back = jnp.max(blk_u - jnp.min(jnp.where(needed, lane_u, blk_u), axis=1, keepdims=True))
    qas = [qa_scr[pl.ds(pl.multiple_of(b * tq, tq), tq), :] for b in blks]

    def scores(t, diagonal):
        kbs, parts = [], []
        for u in range(FOX_PAIR):
            kb = jnp.maximum(blks[u] - t + 1, 0)
            k0 = pl.multiple_of(kb * tk, tk)
            s = lax.dot_general(ka_scr[pl.ds(k0, tk), :], qas[u], (((1,), (1,)), ((), ())),
                                preferred_element_type=jnp.float32)
            if diagonal:
                key_idx = lax.broadcasted_iota(jnp.int32, (tk, tq), 0)
                qry_idx = lax.broadcasted_iota(jnp.int32, (tk, tq), 1)
                s = jnp.where(key_idx <= qry_idx, s, NEG)
            kbs.append(kb)
            parts.append(s)
        return jnp.concatenate(parts, axis=1), kbs

    def weighted_values(p, kbs):
        p = p.astype(jnp.bfloat16)
        return jnp.concatenate(
            [jnp.dot(vt_ref[jnp.maximum(kbs[u] - 1, 0)], p[:, u * tq:(u + 1) * tq],
                     preferred_element_type=jnp.float32)
             for u in range(FOX_PAIR)], axis=1)

    def attend_online(carry, t, diagonal):
        m, l, acc = carry
        s, kbs = scores(t, diagonal)
        m_new = jnp.maximum(m, jnp.max(s, axis=0, keepdims=True))
        alpha = jnp.exp2(m - m_new)
        p = jnp.exp2(s - m_new)
        l = alpha * l + jnp.sum(p, axis=0, keepdims=True)
        return m_new, l, alpha * acc + weighted_values(p, kbs)

    def attend_fixed(carry, t, diagonal):
        l, acc = carry
        s, kbs = scores(t, diagonal)
        p = jnp.exp2(s)
        return l + jnp.sum(p, axis=0, keepdims=True), acc + weighted_values(p, kbs)

    rows = FOX_PAIR * tq
    zl = jnp.zeros((1, rows), jnp.float32)
    zacc = jnp.zeros((FOX_HEAD_DIM, rows), jnp.float32)

    def fixed_path():
        carry = attend_fixed((zl, zacc), 0, True)
        return lax.fori_loop(1, n_back + 1, lambda t, c: attend_fixed(c, t, False), carry)

    def online_path():
        carry = attend_online((jnp.full((1, rows), NEG, jnp.float32), zl, zacc), 0, True)
        _, l, acc = lax.fori_loop(1, n_back + 1, lambda t, c: attend_online(c, t, False), carry)
        return l, acc

    mine = (lane1 >= blks[0]) & (lane1 < blks[0] + FOX_PAIR)
    worst = jnp.max(jnp.where(mine, st_scr[2:3, :], 0.0))
    l, acc = lax.cond(worst <= FIXED_REF_SLACK, fixed_path, online_path)
    y_ref[...] = ((acc / l).T * _silu(z_ref[...].astype(jnp.float32))).astype(jnp.bfloat16)


def _fox(proj, vt, gcol, batch, seq):
    m = proj.shape[0]
    rows = FOX_PAIR * FOX_TQ
    nq = seq // rows
    assert FOX_TQ == FOX_TK and seq % rows == 0 and seq // FOX_TQ <= LANES
    hb = D_MODEL // FOX_HEAD_DIM

    def head(c):
        return pl.BlockSpec((seq, FOX_HEAD_DIM), lambda b, h, i, c=c: (b, c * hb + h))

    return pl.pallas_call(
        functools.partial(_fox_kernel, seq=seq),
        grid=(batch, FOX_HEADS, nq),
        in_specs=[
            head(C_BQ), head(C_BK),
            pl.BlockSpec((seq // FOX_TK, FOX_HEAD_DIM, FOX_TK), lambda b, h, i: (b, h, 0)),
            pl.BlockSpec((rows, FOX_HEAD_DIM), lambda b, h, i: (b * nq + i, C_BZ * hb + h)),
            pl.BlockSpec((seq, LANES), lambda b, h, i: (b, 0)),
        ],
        out_specs=pl.BlockSpec((rows, FOX_HEAD_DIM), lambda b, h, i: (b * nq + i, h)),
        out_shape=jax.ShapeDtypeStruct((m, D_MODEL), jnp.bfloat16),
        scratch_shapes=[
            pltpu.VMEM((seq, 2 * FOX_HEAD_DIM), jnp.bfloat16),
            pltpu.VMEM((seq + FOX_TK, 2 * FOX_HEAD_DIM), jnp.bfloat16),
            pltpu.VMEM((8, LANES), jnp.float32),
        ],
        compiler_params=pltpu.CompilerParams(
            dimension_semantics=("arbitrary", "arbitrary", "arbitrary"), vmem_limit_bytes=VMEM_LIMIT),
        name="fox",
    )(proj, proj, vt, proj, gcol)


def _merge_kernel(ya_ref, yb_ref, cu_ref, halo_ref, cz_ref, gates_ref, x_ref, pw_ref, ps_ref,
                  wb_ref, wo_ref, fg_ref, o_ref, *, seq, final):
    tm = MERGE_TM
    gd = POOL_GROUP_DIM
    i = pl.program_id(0)
    t0 = (i * tm) % seq
    pos = t0 + lax.broadcasted_iota(jnp.int32, (tm, 1), 0)
    halo = jnp.where(t0 == 0, 0.0, halo_ref[...].astype(jnp.float32))
    u = cu_ref[...].astype(jnp.float32)
    ext = jnp.concatenate([halo, u], axis=0)

    ys = []
    for g, win in enumerate(POOL_WINDOWS):
        sl = slice(g * gd, (g + 1) * gd)
        acc = ext[:, sl]
        step = 1
        while step < win:
            acc = acc + pltpu.roll(acc, step, 0)
            step *= 2
        cnt = jnp.minimum(pos + 1, win).astype(jnp.float32)
        d = acc[HALO:, :] / cnt - u[:, sl]
        ys.append(jnp.dot(d.astype(jnp.bfloat16), pw_ref[g], preferred_element_type=jnp.float32))
    yc = jnp.concatenate(ys, axis=1) * ps_ref[...]
    yc = yc * _silu(cz_ref[...].astype(jnp.float32))

    merged = jnp.zeros((tm, D_MODEL), jnp.float32)
    for n, y in enumerate((ya_ref[...], yb_ref[...], yc.astype(jnp.bfloat16))):
        yb = jnp.dot(y, wb_ref[n], preferred_element_type=jnp.float32)
        gate = _sigmoid(gates_ref[:, n * D_MODEL:(n + 1) * D_MODEL].astype(jnp.float32))
        merged = merged + gate * yb
    out = x_ref[...] + jnp.dot(merged.astype(jnp.bfloat16), wo_ref[...],
                               preferred_element_type=jnp.float32)
    if final:
        r = lax.rsqrt(jnp.mean(out * out, axis=-1, keepdims=True) + EPS)
        out = (out * r) * fg_ref[...]
    o_ref[...] = out


def _merge(ya, yb, proj, x2, pool_w, pool_scale, w_branch, w_out, final_g, seq, final, layer):
    m = x2.shape[0]
    tm = MERGE_TM
    hpt = tm // HALO

    def rowblk(c, width=1):
        return pl.BlockSpec((tm, width * D_MODEL), lambda i, c=c: (i, c))

    const2 = lambda i: (0, 0)
    layer3 = lambda i: (layer, 0, 0)
    layer4 = lambda i: (layer, 0, 0, 0)
    return pl.pallas_call(
        functools.partial(_merge_kernel, seq=seq, final=final),
        grid=(m // tm,),
        in_specs=[
            rowblk(0), rowblk(0), rowblk(C_CU),
            pl.BlockSpec((HALO, D_MODEL), lambda i: (jnp.maximum(i * hpt - 1, 0), C_CU)),
            rowblk(C_CZ),
            pl.BlockSpec((tm, N_BRANCH * D_MODEL), lambda i: (i, C_GATES // N_BRANCH)),
            rowblk(0),
            pl.BlockSpec((None, POOL_GROUPS, POOL_GROUP_DIM, POOL_GROUP_DIM), layer4),
            pl.BlockSpec((1, D_MODEL), const2),
            pl.BlockSpec((None, N_BRANCH, D_MODEL, D_MODEL), layer4),
            pl.BlockSpec((None, D_MODEL, D_MODEL), layer3),
            pl.BlockSpec((1, D_MODEL), const2),
        ],
        out_specs=pl.BlockSpec((tm, D_MODEL), lambda i: (i, 0)),
        out_shape=jax.ShapeDtypeStruct((m, D_MODEL), jnp.float32),
        compiler_params=pltpu.CompilerParams(
            dimension_semantics=("arbitrary",), vmem_limit_bytes=VMEM_LIMIT),
        name="merge",
    )(ya, yb, proj, proj, proj, proj, x2, pool_w, pool_scale, w_branch, w_out, final_g)


O_AI = 5 * D_MODEL
O_BQ = O_AI + 2 * ML_HEADS
O_BF = O_BQ + 4 * D_MODEL
O_CU = O_BF + FOX_HEADS
O_G = O_CU + 2 * D_MODEL
N_IN = O_G + N_BRANCH * D_MODEL
assert all(o % 8 == 0 for o in (O_BQ, O_CU, O_G))


def _w_row(j):
    return jnp.where(j < C_AQ, O_G + j * D_MODEL,
                     jnp.where(j < C_BQ, (j - C_AQ) * D_MODEL,
                               jnp.where(j < C_CU, O_BQ + (j - C_BQ) * D_MODEL,
                                         O_CU + (j - C_CU) * D_MODEL)))


def kernel(x, norm_g, w_in, conv_w, ml_bi, ml_bf, ml_norm_g, fox_bf, pool_w, pool_scale, w_branch,
           w_out, final_g):
    batch, seq, d = x.shape
    depth = norm_g.shape[0]
    assert d == D_MODEL and all(seq % t == 0 for t in (IN_TM, MERGE_TM, ML_CHUNK, GATES_ROWS))
    assert GATES_ROWS % ML_CHUNK == 0
    x2 = x.reshape(batch * seq, d)
    assert IN_TN == D_MODEL and w_in.shape[1:] == (D_MODEL, N_IN)
    wt = jnp.swapaxes(w_in, 1, 2)
    pool_wb = pool_w.astype(jnp.bfloat16)
    w_branchb = w_branch.astype(jnp.bfloat16)
    w_outb = w_out.astype(jnp.bfloat16)
    for l in range(depth):
        gate_bias = jnp.pad(jnp.concatenate([ml_bi[l], ml_bf[l], fox_bf[l]]),
                            (0, LANES - 2 * ML_HEADS - FOX_HEADS)).reshape(1, LANES)
        proj, gate_pre, vt = _in_proj(x2, norm_g[l].reshape(1, d), wt, l)
        gcol, grow = _gates(gate_pre, gate_bias, batch, seq)
        ya = _mlstm(proj, conv_w[l], gcol, grow, ml_norm_g[l].reshape(1, d), batch, seq)
        yb = _fox(proj, vt, gcol, batch, seq)
        x2 = _merge(ya, yb, proj, x2, pool_wb, pool_scale[l].reshape(1, d), w_branchb, w_outb,
                    final_g.reshape(1, d), seq, final=(l == depth - 1), layer=l)
    return x2.reshape(batch, seq, d)
```
